```python
import math
import jax, jax.numpy as jnp
from jax import lax
import numpy as np

D_MODEL = 1024
BATCH = 1
SEQ = 16384
DEPTH = 1
DEC_BATCH = 32
DEC_SEQ = 8
PAST_LEN = 16384
PAGE_SIZE = 128

D_MIX = D_MODEL
ATT_WIDTH = D_MIX // 2
CONV_WIDTH = D_MIX - ATT_WIDTH
N_HEADS = 4
HEAD_DIM = ATT_WIDTH // (2 * N_HEADS)
V_DIM = 2 * HEAD_DIM
D_IN_PROJ = 3 * ATT_WIDTH + 2 * CONV_WIDTH
CONV_K = 31
D_FF = 4 * D_MODEL
N_MEM = 256
MEM_HEADS = 4
MEM_HEAD_DIM = D_MODEL // MEM_HEADS
ROPE_THETA = 10000.0
Q_BLOCK = 128
RMS_EPS = 1e-6
LN_EPS = 1e-5

kernel_name = "hybrid_diffattn_conformer_decode_step"


def rmsnorm(x, g):
    xf = x.astype(jnp.float32)
    y = xf * lax.rsqrt(jnp.mean(xf * xf, axis=-1, keepdims=True) + RMS_EPS)
    return (y * g.astype(jnp.float32)).astype(x.dtype)


def layernorm(x, g, b):
    xf = x.astype(jnp.float32)
    mu = jnp.mean(xf, axis=-1, keepdims=True)
    var = jnp.mean(jnp.square(xf - mu), axis=-1, keepdims=True)
    y = (xf - mu) * lax.rsqrt(var + LN_EPS)
    return (y * g.astype(jnp.float32) + b.astype(jnp.float32)).astype(x.dtype)


def rope(x, pos):
    half = HEAD_DIM // 2
    inv = ROPE_THETA ** (-jnp.arange(0, half, dtype=jnp.float32) * 2.0 / HEAD_DIM)
    ang = pos.astype(jnp.float32)[:, None] * inv[None, :]
    cos = jnp.cos(ang)[None, :, None, None, :]
    sin = jnp.sin(ang)[None, :, None, None, :]
    xf = x.astype(jnp.float32)
    x1, x2 = xf[..., :half], xf[..., half:]
    out = jnp.concatenate([x1 * cos - x2 * sin, x2 * cos + x1 * sin], axis=-1)
    return out.astype(x.dtype)


def diff_attn_core(q, k, v, q_pos, k_pos, lam):
    s = jnp.einsum("bqhcd,bkhcd->bhcqk", q, k).astype(jnp.float32) * (HEAD_DIM ** -0.5)
    mask = k_pos[None, :] <= q_pos[:, None]
    s = jnp.where(mask[None, None, None], s, -jnp.inf)
    p = jax.nn.softmax(s, axis=-1)
    a = p[:, :, 0] - lam * p[:, :, 1]
    return jnp.einsum("bhqk,bkhe->bqhe", a.astype(v.dtype), v)


def causal_dwconv(u_pad, w, b):
    c = u_pad.shape[-1]
    out = lax.conv_general_dilated(
        u_pad, w.astype(u_pad.dtype)[:, None, :], window_strides=(1,), padding="VALID",
        dimension_numbers=("NWC", "WIO", "NWC"), feature_group_count=c)
    return out + b.astype(u_pad.dtype)


def memory_kv(mem, mem_norm_g, w_mk, w_mv):
    b, n, _ = mem.shape
    m = rmsnorm(mem, mem_norm_g)
    mk = (m @ w_mk).reshape(b, n, MEM_HEADS, MEM_HEAD_DIM)
    mv = (m @ w_mv).reshape(b, n, MEM_HEADS, MEM_HEAD_DIM)
    return mk, mv


def cross_attn(h, mk, mv, w_mq, w_mo):
    b, t, _ = h.shape
    q = (h @ w_mq).reshape(b, t, MEM_HEADS, MEM_HEAD_DIM)
    s = jnp.einsum("bqhd,bkhd->bhqk", q, mk.astype(q.dtype)).astype(jnp.float32) * (MEM_HEAD_DIM ** -0.5)
    p = jax.nn.softmax(s, axis=-1)
    o = jnp.einsum("bhqk,bkhd->bqhd", p.astype(q.dtype), mv.astype(q.dtype))
    return o.reshape(b, t, D_MODEL) @ w_mo


def hybrid_layer(x, pos, attend, conv_prev, mem_k, mem_v, lam_init,
                 w_in, lq1, lk1, lq2, lk2, subln_g, conv_w, conv_b, conv_ln_g, conv_ln_b, w_out,
                 mix_pre_g, mix_post_g, w_mq, w_mo, cross_pre_g, cross_post_g,
                 w_ff1, w_ff2, mlp_pre_g, mlp_post_g):
    b, t, _ = x.shape
    h = rmsnorm(x, mix_pre_g)
    proj = h @ w_in
    q, k, v, u = jnp.split(proj, [ATT_WIDTH, 2 * ATT_WIDTH, 3 * ATT_WIDTH], axis=-1)
    q = rope(q.reshape(b, t, N_HEADS, 2, HEAD_DIM), pos)
    k = rope(k.reshape(b, t, N_HEADS, 2, HEAD_DIM), pos)
    v = v.reshape(b, t, N_HEADS, V_DIM)
    lam = (jnp.exp(jnp.sum(lq1.astype(jnp.float32) * lk1.astype(jnp.float32)))
           - jnp.exp(jnp.sum(lq2.astype(jnp.float32) * lk2.astype(jnp.float32))) + lam_init)
    o = attend(q, k, v, lam)
    o = (rmsnorm(o, subln_g) * (1.0 - lam_init)).reshape(b, t, ATT_WIDTH)
    glu = u[..., :CONV_WIDTH] * jax.nn.sigmoid(u[..., CONV_WIDTH:])
    padded = jnp.concatenate([conv_prev.astype(glu.dtype), glu], axis=1)
    c = causal_dwconv(padded, conv_w, conv_b)
    c = jax.nn.silu(layernorm(c, conv_ln_g, conv_ln_b))
    mix = jnp.concatenate([o, c], axis=-1) @ w_out
    x = x + rmsnorm(mix, mix_post_g)
    x = x + rmsnorm(cross_attn(rmsnorm(x, cross_pre_g), mem_k, mem_v, w_mq, w_mo), cross_post_g)
    hf = rmsnorm(x, mlp_pre_g) @ w_ff1
    x = x + rmsnorm(jnp.square(jax.nn.relu(hf)) @ w_ff2, mlp_post_g)
    return x, k, v, padded[:, -(CONV_K - 1):]


def setup_inputs(seed: int = 0) -> dict:
    key = jax.random.key(seed)
    ks = jax.random.split(key, 40)
    f32 = jnp.float32

    def nrm(i, shape, scale):
        return scale * jax.random.normal(ks[i], shape, f32)

    n_pages = PAST_LEN // PAGE_SIZE
    n_used = DEC_BATCH * n_pages
    n_pool = n_used + max(1, n_used // 4)
    page_table = jax.random.permutation(ks[8], n_pool)[:n_used].reshape(DEC_BATCH, n_pages).astype(jnp.int32)
    return {
        "x_prompt": nrm(0, (BATCH, SEQ, D_MODEL), 1.0),
        "x_sample": nrm(1, (DEC_BATCH, DEC_SEQ, D_MODEL), 1.0),
        "mem_prompt": nrm(2, (BATCH, N_MEM, D_MODEL), 1.0),
        "cache_k": nrm(3, (DEPTH, n_pool, PAGE_SIZE, N_HEADS, 2, HEAD_DIM), 1.0),
        "cache_v": nrm(4, (DEPTH, n_pool, PAGE_SIZE, N_HEADS, V_DIM), 1.0),
        "cache_conv": nrm(5, (DEPTH, DEC_BATCH, CONV_K - 1, CONV_WIDTH), 0.5),
        "cache_mem_k": nrm(6, (DEPTH, DEC_BATCH, N_MEM, MEM_HEADS, MEM_HEAD_DIM), 1.0),
        "cache_mem_v": nrm(7, (DEPTH, DEC_BATCH, N_MEM, MEM_HEADS, MEM_HEAD_DIM), 1.0),
        "page_table": page_table,
        "w_in": nrm(9, (DEPTH, D_MODEL, D_IN_PROJ), D_MODEL ** -0.5),
        "lambda_q1": nrm(10, (DEPTH, HEAD_DIM), 0.1),
        "lambda_k1": nrm(11, (DEPTH, HEAD_DIM), 0.1),
        "lambda_q2": nrm(12, (DEPTH, HEAD_DIM), 0.1),
        "lambda_k2": nrm(13, (DEPTH, HEAD_DIM), 0.1),
        "subln_g": 1.0 + nrm(14, (DEPTH, V_DIM), 0.02),
        "conv_w": nrm(15, (DEPTH, CONV_K, CONV_WIDTH), CONV_K ** -0.5),
        "conv_b": nrm(16, (DEPTH, CONV_WIDTH), 0.02),
        "conv_ln_g": 1.0 + nrm(17, (DEPTH, CONV_WIDTH), 0.02),
        "conv_ln_b": nrm(18, (DEPTH, CONV_WIDTH), 0.02),
        "w_out": nrm(19, (DEPTH, D_MIX, D_MODEL), D_MIX ** -0.5),
        "mix_pre_g": 1.0 + nrm(20, (DEPTH, D_MODEL), 0.02),
        "mix_post_g": 1.0 + nrm(21, (DEPTH, D_MODEL), 0.02),
        "mem_norm_g": 1.0 + nrm(22, (DEPTH, D_MODEL), 0.02),
        "w_mq": nrm(23, (DEPTH, D_MODEL, D_MODEL), D_MODEL ** -0.5),
        "w_mk": nrm(24, (DEPTH, D_MODEL, D_MODEL), D_MODEL ** -0.5),
        "w_mv": nrm(25, (DEPTH, D_MODEL, D_MODEL), D_MODEL ** -0.5),
        "w_mo": nrm(26, (DEPTH, D_MODEL, D_MODEL), D_MODEL ** -0.5),
        "cross_pre_g": 1.0 + nrm(27, (DEPTH, D_MODEL), 0.02),
        "cross_post_g": 1.0 + nrm(28, (DEPTH, D_MODEL), 0.02),
        "w_ff1": nrm(29, (DEPTH, D_MODEL, D_FF), D_MODEL ** -0.5),
        "w_ff2": nrm(30, (DEPTH, D_FF, D_MODEL), D_FF ** -0.5),
        "mlp_pre_g": 1.0 + nrm(31, (DEPTH, D_MODEL), 0.02),
        "mlp_post_g": 1.0 + nrm(32, (DEPTH, D_MODEL), 0.02),
    }


def reference(x_prompt, x_sample, mem_prompt, cache_k, cache_v, cache_conv, cache_mem_k, cache_mem_v,
              page_table, w_in, lambda_q1, lambda_k1, lambda_q2, lambda_k2, subln_g, conv_w, conv_b,
              conv_ln_g, conv_ln_b, w_out, mix_pre_g, mix_post_g, mem_norm_g, w_mq, w_mk, w_mv, w_mo,
              cross_pre_g, cross_post_g, w_ff1, w_ff2, mlp_pre_g, mlp_post_g):
    pos_p = jnp.arange(SEQ, dtype=jnp.int32)
    pos_s = PAST_LEN + jnp.arange(DEC_SEQ, dtype=jnp.int32)
    pos_all = jnp.arange(PAST_LEN + DEC_SEQ, dtype=jnp.int32)
    n_blocks = SEQ // Q_BLOCK

    def attend_prompt(q, k, v, lam):
        qb = jnp.moveaxis(q.reshape(BATCH, n_blocks, Q_BLOCK, N_HEADS, 2, HEAD_DIM), 1, 0)
        pb = pos_p.reshape(n_blocks, Q_BLOCK)
        ob = lax.map(lambda a: diff_attn_core(a[0], k, v, a[1], pos_p, lam), (qb, pb))
        return jnp.moveaxis(ob, 0, 1).reshape(BATCH, SEQ, N_HEADS, V_DIM)

    y_p, y_s = x_prompt, x_sample
    k_p_l, v_p_l, conv_p_l, mk_p_l, mv_p_l, k_s_l, v_s_l, conv_s_l = [], [], [], [], [], [], [], []
    for i in range(DEPTH):
        lam_init = 0.8 - 0.6 * math.exp(-0.3 * i)
        shared = (w_in[i], lambda_q1[i], lambda_k1[i], lambda_q2[i], lambda_k2[i], subln_g[i], conv_w[i],
                  conv_b[i], conv_ln_g[i], conv_ln_b[i], w_out[i], mix_pre_g[i], mix_post_g[i],
                  w_mq[i], w_mo[i], cross_pre_g[i], cross_post_g[i], w_ff1[i], w_ff2[i],
                  mlp_pre_g[i], mlp_post_g[i])
        mk_p, mv_p = memory_kv(mem_prompt, mem_norm_g[i], w_mk[i], w_mv[i])
        conv0 = jnp.zeros((BATCH, CONV_K - 1, CONV_WIDTH), x_prompt.dtype)
        y_p, k_p, v_p, conv_p = hybrid_layer(y_p, pos_p, attend_prompt, conv0, mk_p, mv_p, lam_init, *shared)
        kp = cache_k[i][page_table].reshape(DEC_BATCH, PAST_LEN, N_HEADS, 2, HEAD_DIM)
        vp = cache_v[i][page_table].reshape(DEC_BATCH, PAST_LEN, N_HEADS, V_DIM)

        def attend_sample(q, k, v, lam, kp=kp, vp=vp):
            k_all = jnp.concatenate([kp.astype(k.dtype), k], axis=1)
            v_all = jnp.concatenate([vp.astype(v.dtype), v], axis=1)
            return diff_attn_core(q, k_all, v_all, pos_s, pos_all, lam)

        y_s, k_s, v_s, conv_s = hybrid_layer(y_s, pos_s, attend_sample, cache_conv[i], cache_mem_k[i],
                                             cache_mem_v[i], lam_init, *shared)
        k_p_l.append(k_p); v_p_l.append(v_p); conv_p_l.append(conv_p)
        mk_p_l.append(mk_p); mv_p_l.append(mv_p)
        k_s_l.append(k_s); v_s_l.append(v_s); conv_s_l.append(conv_s)

    k_prompt = jnp.stack(k_p_l, axis=0)
    v_prompt = jnp.stack(v_p_l, axis=0)
    conv_prompt = jnp.stack(conv_p_l, axis=0)
    mem_k_prompt = jnp.stack(mk_p_l, axis=0)
    mem_v_prompt = jnp.stack(mv_p_l, axis=0)
    k_sample = jnp.stack(k_s_l, axis=0)
    v_sample = jnp.stack(v_s_l, axis=0)
    conv_sample = jnp.stack(conv_s_l, axis=0)
    return (y_p, y_s, k_prompt, v_prompt, conv_prompt, mem_k_prompt, mem_v_prompt, k_sample, v_sample, conv_sample)
```

```python
import functools
import math

import jax
import jax.numpy as jnp
from jax import lax
from jax.experimental import pallas as pl
from jax.experimental.pallas import tpu as pltpu

F32 = jnp.float32
BF16 = jnp.bfloat16

RMS_EPS = 1e-6
LN_EPS = 1e-5
ROPE_THETA = 10000.0
LANES = 128
SUBLANES = 8
HEAD_DIM = 64
V_DIM = 2 * HEAD_DIM
CONV_K = 31
CONV_HALO = 32
VMEM_LIMIT = 56 * 1024 * 1024


def _rms(x, g):
    return x * lax.rsqrt(jnp.mean(x * x, axis=-1, keepdims=True) + RMS_EPS) * g


def _lam(lq1, lk1, lq2, lk2, lam_init):
    a = jnp.sum(lq1[...] * lk1[...], axis=-1, keepdims=True)
    b = jnp.sum(lq2[...] * lk2[...], axis=-1, keepdims=True)
    return jnp.exp(a) - jnp.exp(b) + lam_init


def _rope_lanes(x, cos, sin_signed, first_half):
    rot = jnp.where(first_half, pltpu.roll(x, LANES - HEAD_DIM // 2, 1), pltpu.roll(x, HEAD_DIM // 2, 1))
    return x * cos + rot * sin_signed


def _const_spec(shape):
    nd = len(shape)
    return pl.BlockSpec(shape, lambda *_: (0,) * nd, pipeline_mode=pl.Buffered(1))


def _params(sem):
    return pltpu.CompilerParams(dimension_semantics=sem, vmem_limit_bytes=VMEM_LIMIT)


def _memory_kv_kernel(mem_ref, g_ref, wk_ref, wv_ref, mk_ref, mv_ref, *, mem_heads):
    m = _rms(mem_ref[...], g_ref[...]).astype(BF16)
    n, d = mem_ref.shape
    hd = d // mem_heads
    mk = jnp.dot(m, wk_ref[...], preferred_element_type=F32)
    mv = jnp.dot(m, wv_ref[...], preferred_element_type=F32)
    n_lt = hd // LANES
    for h in range(mem_heads):
        for lt in range(n_lt):
            rows = pl.ds(lt * mem_heads + h, n, stride=mem_heads * n_lt)
            col = h * hd + lt * LANES
            mk_ref[rows, :] = mk[:, col:col + LANES]
            mv_ref[rows, :] = mv[:, col:col + LANES]


def _memory_kv(mem, g, wk, wv, *, mem_heads):
    n, d = mem.shape
    out = jax.ShapeDtypeStruct((n * d // LANES, LANES), F32)
    return pl.pallas_call(
        functools.partial(_memory_kv_kernel, mem_heads=mem_heads),
        out_shape=(out, out),
        name="memory_kv",
        compiler_params=pltpu.CompilerParams(vmem_limit_bytes=VMEM_LIMIT),
    )(mem, g, wk, wv)


def _inproj_prompt_kernel(x_ref, g_ref, w_ref, cos_ref, sin_ref, cw_ref, cb_ref, lng_ref, lnb_ref,
                          kout_ref, vout_ref, qt_ref, kst_ref, vt_ref, c_ref, tail_ref, gpad_ref,
                          *, tm, aw, cwid, n_heads, scale, row_chunk):
    i = pl.program_id(0)

    @pl.when(i == 0)
    def _():
        gpad_ref[0:CONV_HALO, :] = jnp.zeros((CONV_HALO, cwid), F32)

    h = _rms(x_ref[...], g_ref[...]).astype(BF16)
    cos = cos_ref[...]
    sin = sin_ref[...]
    lane = lax.broadcasted_iota(jnp.int32, (tm, LANES), 1)
    first_half = (lane % HEAD_DIM) < (HEAD_DIM // 2)
    lo = lane < HEAD_DIM

    q = jnp.dot(h, w_ref[:, 0:aw], preferred_element_type=F32)
    k = jnp.dot(h, w_ref[:, aw:2 * aw], preferred_element_type=F32)
    v = jnp.dot(h, w_ref[:, 2 * aw:3 * aw], preferred_element_type=F32)
    for hd in range(n_heads):
        sl = slice(hd * LANES, (hd + 1) * LANES)
        qh = _rope_lanes(q[:, sl], cos, sin, first_half) * scale
        kh = _rope_lanes(k[:, sl], cos, sin, first_half)
        kout_ref[sl, :] = kh.T
        vout_ref[pl.ds(hd, tm, stride=n_heads), :] = v[:, sl]
        qt_ref[hd, 0] = qh.T.astype(BF16)
        kst_ref[hd, 0, 0:tm, :] = jnp.where(lo, kh, 0.0).astype(BF16)
        kst_ref[hd, 0, tm:2 * tm, :] = jnp.where(lo, 0.0, kh).astype(BF16)
        vt_ref[hd, 0] = v[:, sl].T.astype(BF16)

    u1 = jnp.dot(h, w_ref[:, 3 * aw:3 * aw + cwid], preferred_element_type=F32)
    u2 = jnp.dot(h, w_ref[:, 3 * aw + cwid:3 * aw + 2 * cwid], preferred_element_type=F32)
    gpad_ref[CONV_HALO:CONV_HALO + tm, :] = u1 * jax.nn.sigmoid(u2)

    base = CONV_HALO - (CONV_K - 1)
    cb = cb_ref[...]
    lng = lng_ref[...]
    lnb = lnb_ref[...]
    for r0 in range(0, tm, row_chunk):
        acc = jnp.zeros((row_chunk, cwid), F32)
        for j in range(CONV_K):
            acc = acc + cw_ref[j:j + 1, :] * gpad_ref[pl.ds(r0 + base + j, row_chunk), :]
        acc = acc + cb
        mu = jnp.mean(acc, axis=-1, keepdims=True)
        d = acc - mu
        var = jnp.mean(d * d, axis=-1, keepdims=True)
        y = d * lax.rsqrt(var + LN_EPS) * lng + lnb
        c_ref[r0:r0 + row_chunk, :] = (y * jax.nn.sigmoid(y)).astype(c_ref.dtype)

    tail_ref[...] = gpad_ref[tm:tm + CONV_HALO, :]
    gpad_ref[0:CONV_HALO, :] = gpad_ref[tm:tm + CONV_HALO, :]


def _inproj_prompt(x, g, w_in, cos, sin, conv_w, conv_b, ln_g, ln_b, *, tm, aw, cwid, n_heads):
    t, d = x.shape
    nt = t // tm
    kern = functools.partial(_inproj_prompt_kernel, tm=tm, aw=aw, cwid=cwid, n_heads=n_heads,
                             scale=HEAD_DIM ** -0.5, row_chunk=min(tm, 32))
    row = lambda i: (i, 0)
    return pl.pallas_call(
        kern,
        grid=(nt,),
        in_specs=[
            pl.BlockSpec((tm, d), row),
            _const_spec((1, d)),
            _const_spec(w_in.shape),
            pl.BlockSpec((tm, LANES), row),
            pl.BlockSpec((tm, LANES), row),
            _const_spec(conv_w.shape),
            _const_spec((1, cwid)),
            _const_spec((1, cwid)),
            _const_spec((1, cwid)),
        ],
        out_specs=[
            pl.BlockSpec((aw, tm), lambda i: (0, i)),
            pl.BlockSpec((tm * n_heads, V_DIM), row),
            pl.BlockSpec((n_heads, 1, LANES, tm), lambda i: (0, i, 0, 0)),
            pl.BlockSpec((n_heads, 1, 2 * tm, LANES), lambda i: (0, i, 0, 0)),
            pl.BlockSpec((n_heads, 1, LANES, tm), lambda i: (0, i, 0, 0)),
            pl.BlockSpec((tm, cwid), row),
            pl.BlockSpec((CONV_HALO, cwid), lambda i: (0, 0)),
        ],
        out_shape=(
            jax.ShapeDtypeStruct((aw, t), F32),
            jax.ShapeDtypeStruct((t * n_heads, V_DIM), F32),
            jax.ShapeDtypeStruct((n_heads, nt, LANES, tm), BF16),
            jax.ShapeDtypeStruct((n_heads, nt, 2 * tm, LANES), BF16),
            jax.ShapeDtypeStruct((n_heads, nt, LANES, tm), BF16),
            jax.ShapeDtypeStruct((t, cwid), BF16),
            jax.ShapeDtypeStruct((CONV_HALO, cwid), F32),
        ),
        scratch_shapes=[pltpu.VMEM((tm + CONV_HALO, cwid), F32)],
        name="inproj_prompt",
        compiler_params=_params(("arbitrary",)),
    )(x, g, w_in, cos, sin, conv_w, conv_b, ln_g, ln_b)


def _attn_prompt_kernel(qt_ref, kst_ref, vt_ref, lq1, lk1, lq2, lk2, sg_ref, o_ref,
                        m_ref, l_ref, acc_ref, *, tb, lam_init):
    i = pl.program_id(1)
    qt = qt_ref[0, 0]
    m_ref[...] = jnp.full(m_ref.shape, -jnp.inf, F32)
    l_ref[...] = jnp.zeros(l_ref.shape, F32)
    acc_ref[...] = jnp.zeros(acc_ref.shape, F32)

    def block(j, masked):
        st = jnp.dot(kst_ref[0, j], qt, preferred_element_type=F32)
        vt = vt_ref[0, j]
        if masked:
            keep = (lax.broadcasted_iota(jnp.int32, (tb, tb), 0)
                    <= lax.broadcasted_iota(jnp.int32, (tb, tb), 1))
        for c in range(2):
            s = st[c * tb:(c + 1) * tb, :]
            if masked:
                s = jnp.where(keep, s, -jnp.inf)
            m_prev = m_ref[c]
            m_new = jnp.maximum(m_prev, jnp.max(s, axis=0, keepdims=True))
            p = jnp.exp(s - m_new)
            alpha = jnp.exp(m_prev - m_new)
            l_ref[c] = alpha * l_ref[c] + jnp.sum(p, axis=0, keepdims=True)
            acc_ref[c] = alpha * acc_ref[c] + jnp.dot(vt, p.astype(BF16), preferred_element_type=F32)
            m_ref[c] = m_new

    def body(j, carry):
        block(j, False)
        return carry

    lax.fori_loop(0, i, body, 0)
    block(i, True)

    lam = _lam(lq1, lk1, lq2, lk2, lam_init)
    a = acc_ref[0] / l_ref[0] - lam * (acc_ref[1] / l_ref[1])
    y = a * lax.rsqrt(jnp.mean(a * a, axis=0, keepdims=True) + RMS_EPS) * sg_ref[...] * (1.0 - lam_init)
    o_ref[...] = y.T.astype(o_ref.dtype)


def _attn_prompt(qt, kst, vt, lq1, lk1, lq2, lk2, subln_col, *, lam_init):
    n_heads, nt, _, tb = qt.shape
    t = nt * tb
    kern = functools.partial(_attn_prompt_kernel, tb=tb, lam_init=lam_init)
    vec = pl.BlockSpec((1, HEAD_DIM), lambda h, i: (0, 0))
    return pl.pallas_call(
        kern,
        grid=(n_heads, nt),
        in_specs=[
            pl.BlockSpec((1, 1, LANES, tb), lambda h, i: (h, i, 0, 0)),
            pl.BlockSpec((1, nt, 2 * tb, LANES), lambda h, i: (h, 0, 0, 0)),
            pl.BlockSpec((1, nt, LANES, tb), lambda h, i: (h, 0, 0, 0)),
            vec, vec, vec, vec,
            pl.BlockSpec((V_DIM, 1), lambda h, i: (0, 0)),
        ],
        out_specs=pl.BlockSpec((tb, V_DIM), lambda h, i: (i, h)),
        out_shape=jax.ShapeDtypeStruct((t, n_heads * V_DIM), BF16),
        scratch_shapes=[pltpu.VMEM((2, 1, tb), F32), pltpu.VMEM((2, 1, tb), F32),
                        pltpu.VMEM((2, V_DIM, tb), F32)],
        name="attn_prompt",
        compiler_params=_params(("arbitrary", "arbitrary")),
    )(qt, kst, vt, lq1, lk1, lq2, lk2, subln_col)


def _inproj_sample_kernel(x_ref, g_ref, w_ref, cos_ref, sin_ref, cc_ref, toep_ref, cb_ref, lng_ref, lnb_ref,
                          kout_ref, vout_ref, q_ref, glu_ref, c_ref, *, aw, cwid, n_heads, scale, n_b, n_t):
    rows = n_b * n_t
    h = _rms(x_ref[...], g_ref[...]).astype(BF16)
    cos = cos_ref[...]
    sin = sin_ref[...]
    lane = lax.broadcasted_iota(jnp.int32, (rows, LANES), 1)
    first_half = (lane % HEAD_DIM) < (HEAD_DIM // 2)
    q = jnp.dot(h, w_ref[:, 0:aw], preferred_element_type=F32)
    k = jnp.dot(h, w_ref[:, aw:2 * aw], preferred_element_type=F32)
    vout_ref[...] = jnp.dot(h, w_ref[:, 2 * aw:3 * aw], preferred_element_type=F32)
    for hd in range(n_heads):
        sl = slice(hd * LANES, (hd + 1) * LANES)
        q_ref[:, sl] = _rope_lanes(q[:, sl], cos, sin, first_half) * scale
        kout_ref[:, sl] = _rope_lanes(k[:, sl], cos, sin, first_half)
    u1 = jnp.dot(h, w_ref[:, 3 * aw:3 * aw + cwid], preferred_element_type=F32)
    u2 = jnp.dot(h, w_ref[:, 3 * aw + cwid:3 * aw + 2 * cwid], preferred_element_type=F32)
    glu_ref[...] = (u1 * jax.nn.sigmoid(u2)).reshape(n_b, n_t, cwid)

    acc = jnp.zeros((n_b, n_t, cwid), F32)
    for r in range(CONV_K - 1):
        acc = acc + cc_ref[:, r:r + 1, :] * toep_ref[r]
    for r in range(n_t):
        acc = acc + glu_ref[:, r:r + 1, :] * toep_ref[CONV_K - 1 + r]
    acc = acc + cb_ref[...]
    mu = jnp.mean(acc, axis=-1, keepdims=True)
    d = acc - mu
    var = jnp.mean(d * d, axis=-1, keepdims=True)
    y = d * lax.rsqrt(var + LN_EPS) * lng_ref[...] + lnb_ref[...]
    c_ref[...] = y * jax.nn.sigmoid(y)


def _inproj_sample(x, g, w_in, cos, sin, cache_conv, toep, conv_b, ln_g, ln_b, *, aw, cwid, n_heads, n_b, n_t):
    rows = n_b * n_t
    kern = functools.partial(_inproj_sample_kernel, aw=aw, cwid=cwid, n_heads=n_heads,
                             scale=HEAD_DIM ** -0.5, n_b=n_b, n_t=n_t)
    return pl.pallas_call(
        kern,
        out_shape=(
            jax.ShapeDtypeStruct((rows, aw), F32),
            jax.ShapeDtypeStruct((rows, aw), F32),
            jax.ShapeDtypeStruct((rows, aw), F32),
            jax.ShapeDtypeStruct((n_b, n_t, cwid), F32),
            jax.ShapeDtypeStruct((n_b, n_t, cwid), F32),
        ),
        name="inproj_sample",
        compiler_params=pltpu.CompilerParams(vmem_limit_bytes=VMEM_LIMIT),
    )(x, g, w_in, cos, sin, cache_conv, toep, conv_b, ln_g, ln_b)


def _attn_sample_kernel(pt_ref, q_ref, kn_ref, vn_ref, lq1, lk1, lq2, lk2, sg_ref, *rest,
                        n_pages, page, n_heads, n_t, lam_init):
    k_pages = rest[:n_pages]
    v_pages = rest[n_pages:2 * n_pages]
    o_ref = rest[2 * n_pages]
    kbuf, vbuf, m_ref, l_ref, acc_ref = rest[2 * n_pages + 1:]
    j = pl.program_id(1)
    aw = n_heads * V_DIM
    n_rows = n_heads * 2 * n_t

    qrep = jnp.concatenate([q_ref[...]] * (n_heads * 2), axis=0)
    sel = (lax.broadcasted_iota(jnp.int32, (n_rows, aw), 0) // n_t
           == lax.broadcasted_iota(jnp.int32, (n_rows, aw), 1) // HEAD_DIM)
    wq = jnp.where(sel, qrep, 0.0).astype(BF16)

    def update(s, vals):
        m_prev = m_ref[...]
        m_new = jnp.maximum(m_prev, jnp.max(s, axis=-1, keepdims=True))
        p = jnp.exp(s - m_new)
        alpha = jnp.exp(m_prev - m_new)
        l_ref[...] = alpha * l_ref[...] + jnp.sum(p, axis=-1, keepdims=True)
        acc_ref[...] = alpha * acc_ref[...] + jnp.dot(p.astype(BF16), vals, preferred_element_type=F32)
        m_ref[...] = m_new

    @pl.when(j == 0)
    def _():
        m_ref[...] = jnp.full(m_ref.shape, -jnp.inf, F32)
        l_ref[...] = jnp.zeros(l_ref.shape, F32)
        acc_ref[...] = jnp.zeros(acc_ref.shape, F32)
        pad = jnp.zeros((LANES - n_t, aw), F32)
        kn = jnp.concatenate([kn_ref[...], pad], axis=0).astype(BF16)
        vn = jnp.concatenate([vn_ref[...], pad], axis=0).astype(BF16)
        s = lax.dot_general(wq, kn, (((1,), (1,)), ((), ())), preferred_element_type=F32)
        col = lax.broadcasted_iota(jnp.int32, (n_rows, LANES), 1)
        qt = lax.broadcasted_iota(jnp.int32, (n_rows, LANES), 0) % n_t
        update(jnp.where(col <= qt, s, -jnp.inf), vn)

    for p in range(n_pages):
        kbuf[:, p * page:(p + 1) * page] = k_pages[p][0].astype(BF16)
        for hd in range(n_heads):
            vbuf[p * page:(p + 1) * page, hd * V_DIM:(hd + 1) * V_DIM] = (
                v_pages[p][0, pl.ds(hd, page, stride=n_heads), :].astype(BF16))
    s = jnp.dot(wq, kbuf[...], preferred_element_type=F32)
    update(s, vbuf[...])

    @pl.when(j == pl.num_programs(1) - 1)
    def _():
        lam = _lam(lq1, lk1, lq2, lk2, lam_init)
        o = acc_ref[...] / l_ref[...]
        for hd in range(n_heads):
            r0 = hd * 2 * n_t
            sl = slice(hd * V_DIM, (hd + 1) * V_DIM)
            a = o[r0:r0 + n_t, sl] - lam * o[r0 + n_t:r0 + 2 * n_t, sl]
            o_ref[:, sl] = _rms(a, sg_ref[...]) * (1.0 - lam_init)


def _attn_sample(page_table, q, k_new, v_new, cache_kt, cache_v, lq1, lk1, lq2, lk2, subln_g,
                 *, n_b, n_t, n_heads, pages_per_step, lam_init):
    n_pool, aw, page = cache_kt.shape
    n_chunks = page_table.shape[1] // pages_per_step
    n_rows = n_heads * 2 * n_t
    kern = functools.partial(_attn_sample_kernel, n_pages=pages_per_step, page=page, n_heads=n_heads,
                             n_t=n_t, lam_init=lam_init)
    tok = pl.BlockSpec((n_t, aw), lambda b, j, pt: (b, 0))
    vec = pl.BlockSpec((1, HEAD_DIM), lambda b, j, pt: (0, 0))

    def page_spec(p, shape):
        return pl.BlockSpec((1,) + shape, lambda b, j, pt: (pt[b, j * pages_per_step + p], 0, 0))

    k_specs = [page_spec(p, (aw, page)) for p in range(pages_per_step)]
    v_specs = [page_spec(p, (page * n_heads, V_DIM)) for p in range(pages_per_step)]
    grid_spec = pltpu.PrefetchScalarGridSpec(
        num_scalar_prefetch=1,
        grid=(n_b, n_chunks),
        in_specs=[tok, tok, tok, vec, vec, vec, vec, pl.BlockSpec((1, V_DIM), lambda b, j, pt: (0, 0))]
        + k_specs + v_specs,
        out_specs=pl.BlockSpec((n_t, aw), lambda b, j, pt: (b, 0)),
        scratch_shapes=[
            pltpu.VMEM((aw, pages_per_step * page), BF16),
            pltpu.VMEM((pages_per_step * page, aw), BF16),
            pltpu.VMEM((n_rows, 1), F32),
            pltpu.VMEM((n_rows, 1), F32),
            pltpu.VMEM((n_rows, aw), F32),
        ],
    )
    return pl.pallas_call(
        kern,
        grid_spec=grid_spec,
        out_shape=jax.ShapeDtypeStruct((n_b * n_t, aw), F32),
        name="attn_sample",
        compiler_params=_params(("arbitrary", "arbitrary")),
    )(page_table, q, k_new, v_new, lq1, lk1, lq2, lk2, subln_g,
      *([cache_kt] * pages_per_step), *([cache_v] * pages_per_step))


def _cross_kernel(o_ref, c_ref, x_ref, wout_ref, gpost_ref, gpre_ref, wmq_ref, mk_ref, mv_ref, wmo_ref,
                  gcpost_ref, y_ref, *, mem_heads):
    mix_in = jnp.concatenate([o_ref[...].astype(BF16), c_ref[...].astype(BF16)], axis=-1)
    mix = jnp.dot(mix_in, wout_ref[...], preferred_element_type=F32)
    x1 = x_ref[...] + _rms(mix, gpost_ref[...])
    hq = _rms(x1, gpre_ref[...]).astype(BF16)
    d = x1.shape[-1]
    hd = d // mem_heads
    q = (jnp.dot(hq, wmq_ref[...], preferred_element_type=F32) * (hd ** -0.5)).astype(BF16)
    n_lt = hd // LANES
    n_mem = mk_ref.shape[1] // (mem_heads * n_lt)

    def head_rows(ref, h):
        return jnp.concatenate(
            [ref[0, pl.ds(lt * mem_heads + h, n_mem, stride=mem_heads * n_lt), :] for lt in range(n_lt)],
            axis=-1).astype(BF16)

    outs = []
    for h in range(mem_heads):
        sl = slice(h * hd, (h + 1) * hd)
        mk = head_rows(mk_ref, h)
        mv = head_rows(mv_ref, h)
        s = lax.dot_general(q[:, sl], mk, (((1,), (1,)), ((), ())), preferred_element_type=F32)
        p = jnp.exp(s - jnp.max(s, axis=-1, keepdims=True))
        p = p / jnp.sum(p, axis=-1, keepdims=True)
        outs.append(jnp.dot(p.astype(BF16), mv, preferred_element_type=F32).astype(BF16))
    att = jnp.dot(jnp.concatenate(outs, axis=-1), wmo_ref[...], preferred_element_type=F32)
    y_ref[...] = x1 + _rms(att, gcpost_ref[...])


def _cross(o, c, x, w_out, g_post, g_pre, w_mq, mem_k, mem_v, w_mo, g_cpost, *, tm, mem_heads, per_tile_mem):
    t, d = x.shape
    aw = o.shape[1]
    cwid = c.shape[1]
    mem_rows, mem_hd = mem_k.shape[1:]
    row = lambda i: (i, 0)
    mem_map = (lambda i: (i, 0, 0)) if per_tile_mem else (lambda i: (0, 0, 0))
    return pl.pallas_call(
        functools.partial(_cross_kernel, mem_heads=mem_heads),
        grid=(t // tm,),
        in_specs=[
            pl.BlockSpec((tm, aw), row),
            pl.BlockSpec((tm, cwid), row),
            pl.BlockSpec((tm, d), row),
            _const_spec(w_out.shape),
            _const_spec((1, d)),
            _const_spec((1, d)),
            _const_spec(w_mq.shape),
            pl.BlockSpec((1, mem_rows, mem_hd), mem_map),
            pl.BlockSpec((1, mem_rows, mem_hd), mem_map),
            _const_spec(w_mo.shape),
            _const_spec((1, d)),
        ],
        out_specs=pl.BlockSpec((tm, d), row),
        out_shape=jax.ShapeDtypeStruct((t, d), F32),
        name="cross_tile%d" % tm,
        compiler_params=_params(("arbitrary",)),
    )(o, c, x, w_out, g_post, g_pre, w_mq, mem_k, mem_v, w_mo, g_cpost)


def _mlp_kernel(x_ref, gpre_ref, w1_ref, w2_ref, gpost_ref, y_ref):
    x = x_ref[...]
    h = _rms(x, gpre_ref[...]).astype(BF16)
    hf = jnp.dot(h, w1_ref[...], preferred_element_type=F32)
    r = jnp.maximum(hf, 0.0)
    f = jnp.dot((r * r).astype(BF16), w2_ref[...], preferred_element_type=F32)
    y_ref[...] = x + _rms(f, gpost_ref[...])


def _mlp(x, g_pre, w1, w2, g_post, *, tm):
    t, d = x.shape
    row = lambda i: (i, 0)
    return pl.pallas_call(
        _mlp_kernel,
        grid=(t // tm,),
        in_specs=[pl.BlockSpec((tm, d), row), _const_spec((1, d)), _const_spec(w1.shape),
                  _const_spec(w2.shape), _const_spec((1, d))],
        out_specs=pl.BlockSpec((tm, d), row),
        out_shape=jax.ShapeDtypeStruct((t, d), F32),
        name="mlp_tile%d" % tm,
        compiler_params=_params(("arbitrary",)),
    )(x, g_pre, w1, w2, g_post)


def _rope_tables(pos):
    half = HEAD_DIM // 2
    inv = ROPE_THETA ** (-jnp.arange(0, half, dtype=F32) * 2.0 / HEAD_DIM)
    ang = pos.astype(F32)[:, None] * inv[None, :]
    cos = jnp.cos(ang)
    sin = jnp.sin(ang)
    reps = LANES // HEAD_DIM
    return (jnp.tile(jnp.concatenate([cos, cos], axis=-1), (1, reps)),
            jnp.tile(jnp.concatenate([-sin, sin], axis=-1), (1, reps)))


def _conv_toeplitz(conv_w, n_t):
    n_r = CONV_K - 1 + n_t
    r = jnp.arange(n_r)[:, None]
    t = jnp.arange(n_t)[None, :]
    j = r - t
    ok = (j >= 0) & (j < CONV_K)
    return jnp.where(ok[:, :, None], conv_w[jnp.clip(j, 0, CONV_K - 1)], 0.0)


def _layer(i, depth_inputs, x_prompt, x_sample, mem_prompt, cache_k, cache_v, cache_conv, cache_mem_k,
           cache_mem_v, page_table, *, tile, pages_per_step):
    (w_in, lq1, lk1, lq2, lk2, subln_g, conv_w, conv_b, ln_g, ln_b, w_out, mix_pre_g, mix_post_g, mem_norm_g,
     w_mq, w_mk, w_mv, w_mo, cross_pre_g, cross_post_g, w_ff1, w_ff2, mlp_pre_g, mlp_post_g) = depth_inputs
    lam_init = 0.8 - 0.6 * math.exp(-0.3 * i)
    _, seq, d = x_prompt.shape
    n_b, n_t, _ = x_sample.shape
    n_heads, v_dim = cache_v.shape[-2:]
    aw = n_heads * v_dim
    cwid = cache_conv.shape[-1]
    mem_heads = cache_mem_k.shape[-2]
    n_mem = mem_prompt.shape[1]
    past = page_table.shape[1] * cache_k.shape[1]

    row2 = lambda a: a.reshape(1, -1)
    w_in_b, w_out_b, w_mq_b, w_mk_b, w_mv_b, w_mo_b, w_ff1_b, w_ff2_b = (
        w.astype(BF16) for w in (w_in, w_out, w_mq, w_mk, w_mv, w_mo, w_ff1, w_ff2))
    lams = (row2(lq1), row2(lk1), row2(lq2), row2(lk2))

    mk_p, mv_p = _memory_kv(mem_prompt[0], row2(mem_norm_g), w_mk_b, w_mv_b, mem_heads=mem_heads)
    cos_p, sin_p = _rope_tables(jnp.arange(seq, dtype=jnp.int32))
    k_p, v_p, qt, kst, vt, c_p, tail_p = _inproj_prompt(
        x_prompt[0], row2(mix_pre_g), w_in_b, cos_p, sin_p, conv_w, row2(conv_b), row2(ln_g), row2(ln_b),
        tm=tile, aw=aw, cwid=cwid, n_heads=n_heads)
    o_p = _attn_prompt(qt, kst, vt, *lams, subln_g.reshape(-1, 1), lam_init=lam_init)
    x2_p = _cross(o_p, c_p, x_prompt[0], w_out_b, row2(mix_post_g), row2(cross_pre_g), w_mq_b,
                  mk_p[None], mv_p[None], w_mo_b, row2(cross_post_g), tm=tile, mem_heads=mem_heads,
                  per_tile_mem=False)
    y_p = _mlp(x2_p, row2(mlp_pre_g), w_ff1_b, w_ff2_b, row2(mlp_post_g), tm=tile)

    pos_s = jnp.tile(past + jnp.arange(n_t, dtype=jnp.int32), n_b)
    cos_s, sin_s = _rope_tables(pos_s)
    xs = x_sample.reshape(n_b * n_t, d)
    k_s, v_s, q_s, glu_s, c_s = _inproj_sample(
        xs, row2(mix_pre_g), w_in_b, cos_s, sin_s, cache_conv, _conv_toeplitz(conv_w, n_t), row2(conv_b),
        row2(ln_g), row2(ln_b), aw=aw, cwid=cwid, n_heads=n_heads, n_b=n_b, n_t=n_t)
    n_pool, page = cache_k.shape[:2]
    cache_kt = jnp.transpose(cache_k, (0, 2, 3, 4, 1)).reshape(n_pool, aw, page)
    cache_vr = cache_v.reshape(n_pool, page * n_heads, v_dim)
    o_s = _attn_sample(page_table, q_s, k_s, v_s, cache_kt, cache_vr, *lams, row2(subln_g),
                       n_b=n_b, n_t=n_t, n_heads=n_heads, pages_per_step=pages_per_step, lam_init=lam_init)
    n_lt = d // mem_heads // LANES

    def mem_rows(a):
        a = a.reshape(a.shape[0], n_mem, mem_heads, n_lt, LANES)
        return jnp.transpose(a, (0, 1, 3, 2, 4)).reshape(a.shape[0], n_mem * n_lt * mem_heads, LANES)

    x2_s = _cross(o_s, c_s.reshape(n_b * n_t, cwid), xs, w_out_b, row2(mix_post_g), row2(cross_pre_g), w_mq_b,
                  mem_rows(cache_mem_k), mem_rows(cache_mem_v), w_mo_b,
                  row2(cross_post_g), tm=n_t, mem_heads=mem_heads, per_tile_mem=True)

    def mem_out(a):
        a = a.reshape(n_mem, n_lt, mem_heads, LANES)
        return jnp.transpose(a, (0, 2, 1, 3)).reshape(1, n_mem, mem_heads, n_lt * LANES)
    y_s = _mlp(x2_s, row2(mlp_pre_g), w_ff1_b, w_ff2_b, row2(mlp_post_g), tm=n_b * n_t)

    hd2 = (n_heads, 2, HEAD_DIM)
    outs = dict(
        y_p=y_p[None], y_s=y_s.reshape(n_b, n_t, d),
        k_p=jnp.transpose(k_p.reshape(*hd2, seq), (3, 0, 1, 2))[None],
        v_p=v_p.reshape(1, seq, n_heads, v_dim),
        conv_p=tail_p[CONV_HALO - (CONV_K - 1):][None],
        mk_p=mem_out(mk_p), mv_p=mem_out(mv_p),
        k_s=k_s.reshape(n_b, n_t, *hd2), v_s=v_s.reshape(n_b, n_t, n_heads, v_dim),
        conv_s=jnp.concatenate([cache_conv[:, n_t:], glu_s], axis=1),
    )
    return outs


def kernel(x_prompt, x_sample, mem_prompt, cache_k, cache_v, cache_conv, cache_mem_k, cache_mem_v, page_table, w_in, lambda_q1, lambda_k1, lambda_q2, lambda_k2, subln_g, conv_w, conv_b, conv_ln_g, conv_ln_b, w_out, mix_pre_g, mix_post_g, mem_norm_g, w_mq, w_mk, w_mv, w_mo, cross_pre_g, cross_post_g, w_ff1, w_ff2, mlp_pre_g, mlp_post_g):
    depth = w_in.shape[0]
    stacked = (w_in, lambda_q1, lambda_k1, lambda_q2, lambda_k2, subln_g, conv_w, conv_b, conv_ln_g, conv_ln_b,
               w_out, mix_pre_g, mix_post_g, mem_norm_g, w_mq, w_mk, w_mv, w_mo, cross_pre_g, cross_post_g,
               w_ff1, w_ff2, mlp_pre_g, mlp_post_g)
    tile = min(512, x_prompt.shape[1])
    pages_per_step = min(16, page_table.shape[1])
    y_p, y_s = x_prompt, x_sample
    per_layer = []
    for i in range(depth):
        o = _layer(i, tuple(a[i] for a in stacked), y_p, y_s, mem_prompt, cache_k[i], cache_v[i], cache_conv[i],
                   cache_mem_k[i], cache_mem_v[i], page_table, tile=tile, pages_per_step=pages_per_step)
        y_p, y_s = o["y_p"], o["y_s"]
        per_layer.append(o)
    stack = lambda name: jnp.stack([o[name] for o in per_layer], axis=0)
    return (y_p, y_s, stack("k_p"), stack("v_p"), stack("conv_p"), stack("mk_p"), stack("mv_p"),
            stack("k_s"), stack("v_s"), stack("conv_s"))
```

```python
import functools
import math

import jax
import jax.numpy as jnp
from jax import lax
from jax.experimental import pallas as pl
from jax.experimental.pallas import tpu as pltpu

F32 = jnp.float32
BF16 = jnp.bfloat16

RMS_EPS = 1e-6
LN_EPS = 1e-5
ROPE_THETA = 10000.0
LANES = 128
SUBLANES = 8
HEAD_DIM = 64
V_DIM = 2 * HEAD_DIM
CONV_K = 31
CONV_HALO = 32
SUM_ROWS = 16
VMEM_LIMIT = 56 * 1024 * 1024


def _rms(x, g):
    return x * lax.rsqrt(jnp.mean(x * x, axis=-1, keepdims=True) + RMS_EPS) * g


def _lam(lq1, lk1, lq2, lk2, lam_init):
    a = jnp.sum(lq1[...] * lk1[...], axis=-1, keepdims=True)
    b = jnp.sum(lq2[...] * lk2[...], axis=-1, keepdims=True)
    return jnp.exp(a) - jnp.exp(b) + lam_init


def _rope_lanes(x, cos, sin_signed, first_half):
    rot = jnp.where(first_half, pltpu.roll(x, LANES - HEAD_DIM // 2, 1), pltpu.roll(x, HEAD_DIM // 2, 1))
    return x * cos + rot * sin_signed


def _const_spec(shape):
    nd = len(shape)
    return pl.BlockSpec(shape, lambda *_: (0,) * nd, pipeline_mode=pl.Buffered(1))


def _params(sem):
    return pltpu.CompilerParams(dimension_semantics=sem, vmem_limit_bytes=VMEM_LIMIT)


def _memory_kv_kernel(mem_ref, g_ref, wk_ref, wv_ref, mk_ref, mv_ref, *, mem_heads):
    m = _rms(mem_ref[...], g_ref[...]).astype(BF16)
    n, d = mem_ref.shape
    hd = d // mem_heads
    mk = jnp.dot(m, wk_ref[...], preferred_element_type=F32)
    mv = jnp.dot(m, wv_ref[...], preferred_element_type=F32)
    n_lt = hd // LANES
    for h in range(mem_heads):
        for lt in range(n_lt):
            rows = pl.ds(lt * mem_heads + h, n, stride=mem_heads * n_lt)
            col = h * hd + lt * LANES
            mk_ref[rows, :] = mk[:, col:col + LANES]
            mv_ref[rows, :] = mv[:, col:col + LANES]


def _memory_kv(mem, g, wk, wv, *, mem_heads):
    n, d = mem.shape
    out = jax.ShapeDtypeStruct((n * d // LANES, LANES), F32)
    return pl.pallas_call(
        functools.partial(_memory_kv_kernel, mem_heads=mem_heads),
        out_shape=(out, out),
        name="memory_kv",
        compiler_params=pltpu.CompilerParams(vmem_limit_bytes=VMEM_LIMIT),
    )(mem, g, wk, wv)


def _inproj_prompt_kernel(x_ref, g_ref, w_ref, cos_ref, sin_ref, cw_ref, cb_ref, lng_ref, lnb_ref,
                          kout_ref, vout_ref, qt_ref, kst_ref, vt_ref, c_ref, tail_ref, gpad_ref, gsh_ref,
                          *, tm, aw, cwid, n_heads, scale, row_chunk):
    i = pl.program_id(0)

    @pl.when(i == 0)
    def _():
        gpad_ref[0:CONV_HALO, :] = jnp.zeros((CONV_HALO, cwid), F32)

    h = _rms(x_ref[...], g_ref[...]).astype(BF16)
    cos = cos_ref[...]
    sin = sin_ref[...]
    lane = lax.broadcasted_iota(jnp.int32, (tm, LANES), 1)
    first_half = (lane % HEAD_DIM) < (HEAD_DIM // 2)
    lo = lane < HEAD_DIM
    sum_rows = (lax.broadcasted_iota(jnp.int32, (SUM_ROWS, tm), 0) == 0).astype(BF16)

    q = jnp.dot(h, w_ref[:, 0:aw], preferred_element_type=F32)
    k = jnp.dot(h, w_ref[:, aw:2 * aw], preferred_element_type=F32)
    v = jnp.dot(h, w_ref[:, 2 * aw:3 * aw], preferred_element_type=F32)
    for hd in range(n_heads):
        sl = slice(hd * LANES, (hd + 1) * LANES)
        qh = _rope_lanes(q[:, sl], cos, sin, first_half) * scale
        kh = _rope_lanes(k[:, sl], cos, sin, first_half)
        kout_ref[sl, :] = kh.T
        vout_ref[pl.ds(hd, tm, stride=n_heads), :] = v[:, sl]
        qt_ref[hd, 0] = qh.T.astype(BF16)
        kst_ref[hd, 0, 0:tm, :] = jnp.where(lo, kh, 0.0).astype(BF16)
        kst_ref[hd, 0, tm:2 * tm, :] = jnp.where(lo, 0.0, kh).astype(BF16)
        vt_ref[hd, 0, 0:V_DIM, :] = v[:, sl].T.astype(BF16)
        vt_ref[hd, 0, V_DIM:V_DIM + SUM_ROWS, :] = sum_rows

    u1 = jnp.dot(h, w_ref[:, 3 * aw:3 * aw + cwid], preferred_element_type=F32)
    u2 = jnp.dot(h, w_ref[:, 3 * aw + cwid:3 * aw + 2 * cwid], preferred_element_type=F32)
    gpad_ref[CONV_HALO:CONV_HALO + tm, :] = u1 * jax.nn.sigmoid(u2)

    base = CONV_HALO - (CONV_K - 1)
    cb = cb_ref[...]
    lng = lng_ref[...]
    lnb = lnb_ref[...]
    n_sh = gsh_ref.shape[1]
    for b in range(1, SUBLANES):
        gsh_ref[b - 1] = gpad_ref[pl.ds(b, n_sh), :]
    for r0 in range(0, tm, row_chunk):
        acc = jnp.zeros((row_chunk, cwid), F32)
        for j in range(CONV_K):
            a, b = divmod(base + j, SUBLANES)
            rows = pl.ds(r0 + a * SUBLANES, row_chunk)
            tap = gpad_ref[rows, :] if b == 0 else gsh_ref[b - 1, rows, :]
            acc = acc + cw_ref[j:j + 1, :] * tap
        acc = acc + cb
        mu = jnp.mean(acc, axis=-1, keepdims=True)
        d = acc - mu
        var = jnp.mean(d * d, axis=-1, keepdims=True)
        y = d * lax.rsqrt(var + LN_EPS) * lng + lnb
        c_ref[r0:r0 + row_chunk, :] = (y * jax.nn.sigmoid(y)).astype(c_ref.dtype)

    tail_ref[...] = gpad_ref[tm:tm + CONV_HALO, :]
    gpad_ref[0:CONV_HALO, :] = gpad_ref[tm:tm + CONV_HALO, :]


def _inproj_prompt(x, g, w_in, cos, sin, conv_w, conv_b, ln_g, ln_b, *, tm, aw, cwid, n_heads):
    t, d = x.shape
    nt = t // tm
    kern = functools.partial(_inproj_prompt_kernel, tm=tm, aw=aw, cwid=cwid, n_heads=n_heads,
                             scale=HEAD_DIM ** -0.5 * math.log2(math.e), row_chunk=min(tm, 32))
    row = lambda i: (i, 0)
    return pl.pallas_call(
        kern,
        grid=(nt,),
        in_specs=[
            pl.BlockSpec((tm, d), row),
            _const_spec((1, d)),
            _const_spec(w_in.shape),
            pl.BlockSpec((tm, LANES), row),
            pl.BlockSpec((tm, LANES), row),
            _const_spec(conv_w.shape),
            _const_spec((1, cwid)),
            _const_spec((1, cwid)),
            _const_spec((1, cwid)),
        ],
        out_specs=[
            pl.BlockSpec((aw, tm), lambda i: (0, i)),
            pl.BlockSpec((tm * n_heads, V_DIM), row),
            pl.BlockSpec((n_heads, 1, LANES, tm), lambda i: (0, i, 0, 0)),
            pl.BlockSpec((n_heads, 1, 2 * tm, LANES), lambda i: (0, i, 0, 0)),
            pl.BlockSpec((n_heads, 1, V_DIM + SUM_ROWS, tm), lambda i: (0, i, 0, 0)),
            pl.BlockSpec((tm, cwid), row),
            pl.BlockSpec((CONV_HALO, cwid), lambda i: (0, 0)),
        ],
        out_shape=(
            jax.ShapeDtypeStruct((aw, t), F32),
            jax.ShapeDtypeStruct((t * n_heads, V_DIM), F32),
            jax.ShapeDtypeStruct((n_heads, nt, LANES, tm), BF16),
            jax.ShapeDtypeStruct((n_heads, nt, 2 * tm, LANES), BF16),
            jax.ShapeDtypeStruct((n_heads, nt, V_DIM + SUM_ROWS, tm), BF16),
            jax.ShapeDtypeStruct((t, cwid), BF16),
            jax.ShapeDtypeStruct((CONV_HALO, cwid), F32),
        ),
        scratch_shapes=[pltpu.VMEM((tm + CONV_HALO, cwid), F32),
                        pltpu.VMEM((SUBLANES - 1, tm + CONV_HALO - SUBLANES, cwid), F32)],
        name="inproj_prompt",
        compiler_params=_params(("arbitrary",)),
    )(x, g, w_in, cos, sin, conv_w, conv_b, ln_g, ln_b)


def _attn_prompt_kernel(qt_ref, kst_ref, vt_ref, lq1, lk1, lq2, lk2, sg_ref, o_ref,
                        m_ref, acc_ref, q_scr, sa_ref, sb_ref, *, tk, lam_init):
    i = pl.program_id(1)
    tq = 2 * tk
    q_scr[:, 0:tk] = qt_ref[0, 0]
    q_scr[:, tk:tq] = qt_ref[0, 1]
    m_ref[...] = jnp.full(m_ref.shape, -jnp.inf, F32)
    acc_ref[...] = jnp.zeros(acc_ref.shape, F32)

    def scores(j, s_ref, key_offset=None):
        st = jnp.dot(kst_ref[0, j], q_scr[...], preferred_element_type=F32)
        if key_offset is None:
            s_ref[...] = st
        else:
            keep = (lax.broadcasted_iota(jnp.int32, (tk, tq), 0) + key_offset
                    <= lax.broadcasted_iota(jnp.int32, (tk, tq), 1))
            for c in range(2):
                s_ref[c * tk:(c + 1) * tk, :] = jnp.where(keep, st[c * tk:(c + 1) * tk, :], -jnp.inf)

    def absorb(j, s_ref):
        vt = vt_ref[0, j]
        for c in range(2):
            s = s_ref[c * tk:(c + 1) * tk, :]
            m_prev = m_ref[c]
            m_new = jnp.maximum(m_prev, jnp.max(s, axis=0, keepdims=True))
            p = jnp.exp2(s - m_new).astype(BF16)
            alpha = jnp.exp2(m_prev - m_new)
            acc_ref[c] = alpha * acc_ref[c] + jnp.dot(vt, p, preferred_element_type=F32)
            m_ref[c] = m_new

    scores(2 * i, sa_ref, key_offset=0)
    scores(2 * i + 1, sb_ref, key_offset=tk)
    absorb(2 * i, sa_ref)

    def pair(t, pending):
        scores(2 * t, sa_ref)
        absorb(pending, sb_ref)
        scores(2 * t + 1, sb_ref)
        absorb(2 * t, sa_ref)
        return 2 * t + 1

    pending = lax.fori_loop(0, i, pair, 2 * i + 1)
    absorb(pending, sb_ref)

    lam = _lam(lq1, lk1, lq2, lk2, lam_init)
    o1 = acc_ref[0, 0:V_DIM, :] / acc_ref[0, V_DIM:V_DIM + 1, :]
    o2 = acc_ref[1, 0:V_DIM, :] / acc_ref[1, V_DIM:V_DIM + 1, :]
    a = o1 - lam * o2
    y = a * lax.rsqrt(jnp.mean(a * a, axis=0, keepdims=True) + RMS_EPS) * sg_ref[...] * (1.0 - lam_init)
    o_ref[...] = y.T.astype(o_ref.dtype)


def _attn_prompt(qt, kst, vt, lq1, lk1, lq2, lk2, subln_col, *, lam_init):
    n_heads, nt, _, tk = qt.shape
    assert nt % 2 == 0, "a query block spans two key blocks"
    t = nt * tk
    tq = 2 * tk
    kern = functools.partial(_attn_prompt_kernel, tk=tk, lam_init=lam_init)
    vec = pl.BlockSpec((1, HEAD_DIM), lambda h, i: (0, 0))
    return pl.pallas_call(
        kern,
        grid=(n_heads, nt // 2),
        in_specs=[
            pl.BlockSpec((1, 2, LANES, tk), lambda h, i: (h, i, 0, 0)),
            pl.BlockSpec((1, nt, 2 * tk, LANES), lambda h, i: (h, 0, 0, 0)),
            pl.BlockSpec((1, nt, V_DIM + SUM_ROWS, tk), lambda h, i: (h, 0, 0, 0)),
            vec, vec, vec, vec,
            pl.BlockSpec((V_DIM, 1), lambda h, i: (0, 0)),
        ],
        out_specs=pl.BlockSpec((tq, V_DIM), lambda h, i: (i, h)),
        out_shape=jax.ShapeDtypeStruct((t, n_heads * V_DIM), BF16),
        scratch_shapes=[pltpu.VMEM((2, 1, tq), F32),
                        pltpu.VMEM((2, V_DIM + SUM_ROWS, tq), F32),
                        pltpu.VMEM((LANES, tq), BF16),
                        pltpu.VMEM((2 * tk, tq), F32),
                        pltpu.VMEM((2 * tk, tq), F32)],
        name="attn_prompt",
        compiler_params=_params(("arbitrary", "arbitrary")),
    )(qt, kst, vt, lq1, lk1, lq2, lk2, subln_col)


def _inproj_sample_kernel(x_ref, g_ref, w_ref, cos_ref, sin_ref, cc_ref, toep_ref, cb_ref, lng_ref, lnb_ref,
                          kout_ref, vout_ref, q_ref, glu_ref, c_ref, *, aw, cwid, n_heads, scale, n_b, n_t):
    rows = n_b * n_t
    h = _rms(x_ref[...], g_ref[...]).astype(BF16)
    cos = cos_ref[...]
    sin = sin_ref[...]
    lane = lax.broadcasted_iota(jnp.int32, (rows, LANES), 1)
    first_half = (lane % HEAD_DIM) < (HEAD_DIM // 2)
    q = jnp.dot(h, w_ref[:, 0:aw], preferred_element_type=F32)
    k = jnp.dot(h, w_ref[:, aw:2 * aw], preferred_element_type=F32)
    vout_ref[...] = jnp.dot(h, w_ref[:, 2 * aw:3 * aw], preferred_element_type=F32)
    for hd in range(n_heads):
        sl = slice(hd * LANES, (hd + 1) * LANES)
        q_ref[:, sl] = _rope_lanes(q[:, sl], cos, sin, first_half) * scale
        kout_ref[:, sl] = _rope_lanes(k[:, sl], cos, sin, first_half)
    u1 = jnp.dot(h, w_ref[:, 3 * aw:3 * aw + cwid], preferred_element_type=F32)
    u2 = jnp.dot(h, w_ref[:, 3 * aw + cwid:3 * aw + 2 * cwid], preferred_element_type=F32)
    glu_ref[...] = (u1 * jax.nn.sigmoid(u2)).reshape(n_b, n_t, cwid)

    acc = jnp.zeros((n_b, n_t, cwid), F32)
    for r in range(CONV_K - 1):
        acc = acc + cc_ref[:, r:r + 1, :] * toep_ref[r]
    for r in range(n_t):
        acc = acc + glu_ref[:, r:r + 1, :] * toep_ref[CONV_K - 1 + r]
    acc = acc + cb_ref[...]
    mu = jnp.mean(acc, axis=-1, keepdims=True)
    d = acc - mu
    var = jnp.mean(d * d, axis=-1, keepdims=True)
    y = d * lax.rsqrt(var + LN_EPS) * lng_ref[...] + lnb_ref[...]
    c_ref[...] = y * jax.nn.sigmoid(y)


def _inproj_sample(x, g, w_in, cos, sin, cache_conv, toep, conv_b, ln_g, ln_b, *, aw, cwid, n_heads, n_b, n_t):
    rows = n_b * n_t
    kern = functools.partial(_inproj_sample_kernel, aw=aw, cwid=cwid, n_heads=n_heads,
                             scale=HEAD_DIM ** -0.5, n_b=n_b, n_t=n_t)
    return pl.pallas_call(
        kern,
        out_shape=(
            jax.ShapeDtypeStruct((rows, aw), F32),
            jax.ShapeDtypeStruct((rows, aw), F32),
            jax.ShapeDtypeStruct((rows, aw), F32),
            jax.ShapeDtypeStruct((n_b, n_t, cwid), F32),
            jax.ShapeDtypeStruct((n_b, n_t, cwid), F32),
        ),
        name="inproj_sample",
        compiler_params=pltpu.CompilerParams(vmem_limit_bytes=VMEM_LIMIT),
    )(x, g, w_in, cos, sin, cache_conv, toep, conv_b, ln_g, ln_b)


def _attn_sample_kernel(pt_ref, q_ref, kn_ref, vn_ref, lq1, lk1, lq2, lk2, sg_ref, *rest,
                        n_pages, page, n_heads, n_t, lam_init):
    k_pages = rest[:n_pages]
    v_pages = rest[n_pages:2 * n_pages]
    o_ref = rest[2 * n_pages]
    kbuf, vbuf, m_ref, l_ref, acc_ref = rest[2 * n_pages + 1:]
    j = pl.program_id(1)
    aw = n_heads * V_DIM
    n_rows = n_heads * 2 * n_t

    qrep = jnp.concatenate([q_ref[...]] * (n_heads * 2), axis=0)
    sel = (lax.broadcasted_iota(jnp.int32, (n_rows, aw), 0) // n_t
           == lax.broadcasted_iota(jnp.int32, (n_rows, aw), 1) // HEAD_DIM)
    wq = jnp.where(sel, qrep, 0.0).astype(BF16)

    def update(s, vals):
        m_prev = m_ref[...]
        m_new = jnp.maximum(m_prev, jnp.max(s, axis=-1, keepdims=True))
        p = jnp.exp(s - m_new)
        alpha = jnp.exp(m_prev - m_new)
        l_ref[...] = alpha * l_ref[...] + jnp.sum(p, axis=-1, keepdims=True)
        acc_ref[...] = alpha * acc_ref[...] + jnp.dot(p.astype(BF16), vals, preferred_element_type=F32)
        m_ref[...] = m_new

    @pl.when(j == 0)
    def _():
        m_ref[...] = jnp.full(m_ref.shape, -jnp.inf, F32)
        l_ref[...] = jnp.zeros(l_ref.shape, F32)
        acc_ref[...] = jnp.zeros(acc_ref.shape, F32)
        pad = jnp.zeros((LANES - n_t, aw), F32)
        kn = jnp.concatenate([kn_ref[...], pad], axis=0).astype(BF16)
        vn = jnp.concatenate([vn_ref[...], pad], axis=0).astype(BF16)
        s = lax.dot_general(wq, kn, (((1,), (1,)), ((), ())), preferred_element_type=F32)
        col = lax.broadcasted_iota(jnp.int32, (n_rows, LANES), 1)
        qt = lax.broadcasted_iota(jnp.int32, (n_rows, LANES), 0) % n_t
        update(jnp.where(col <= qt, s, -jnp.inf), vn)

    for p in range(n_pages):
        kbuf[:, p * page:(p + 1) * page] = k_pages[p][0].astype(BF16)
        for hd in range(n_heads):
            vbuf[p * page:(p + 1) * page, hd * V_DIM:(hd + 1) * V_DIM] = (
                v_pages[p][0, pl.ds(hd, page, stride=n_heads), :].astype(BF16))
    s = jnp.dot(wq, kbuf[...], preferred_element_type=F32)
    update(s, vbuf[...])

    @pl.when(j == pl.num_programs(1) - 1)
    def _():
        lam = _lam(lq1, lk1, lq2, lk2, lam_init)
        o = acc_ref[...] / l_ref[...]
        for hd in range(n_heads):
            r0 = hd * 2 * n_t
            sl = slice(hd * V_DIM, (hd + 1) * V_DIM)
            a = o[r0:r0 + n_t, sl] - lam * o[r0 + n_t:r0 + 2 * n_t, sl]
            o_ref[:, sl] = _rms(a, sg_ref[...]) * (1.0 - lam_init)


def _attn_sample(page_table, q, k_new, v_new, cache_kt, cache_v, lq1, lk1, lq2, lk2, subln_g,
                 *, n_b, n_t, n_heads, pages_per_step, lam_init):
    n_pool, aw, page = cache_kt.shape
    n_chunks = page_table.shape[1] // pages_per_step
    n_rows = n_heads * 2 * n_t
    kern = functools.partial(_attn_sample_kernel, n_pages=pages_per_step, page=page, n_heads=n_heads,
                             n_t=n_t, lam_init=lam_init)
    tok = pl.BlockSpec((n_t, aw), lambda b, j, pt: (b, 0))
    vec = pl.BlockSpec((1, HEAD_DIM), lambda b, j, pt: (0, 0))

    def page_spec(p, shape):
        return pl.BlockSpec((1,) + shape, lambda b, j, pt: (pt[b, j * pages_per_step + p], 0, 0))

    k_specs = [page_spec(p, (aw, page)) for p in range(pages_per_step)]
    v_specs = [page_spec(p, (page * n_heads, V_DIM)) for p in range(pages_per_step)]
    grid_spec = pltpu.PrefetchScalarGridSpec(
        num_scalar_prefetch=1,
        grid=(n_b, n_chunks),
        in_specs=[tok, tok, tok, vec, vec, vec, vec, pl.BlockSpec((1, V_DIM), lambda b, j, pt: (0, 0))]
        + k_specs + v_specs,
        out_specs=pl.BlockSpec((n_t, aw), lambda b, j, pt: (b, 0)),
        scratch_shapes=[
            pltpu.VMEM((aw, pages_per_step * page), BF16),
            pltpu.VMEM((pages_per_step * page, aw), BF16),
            pltpu.VMEM((n_rows, 1), F32),
            pltpu.VMEM((n_rows, 1), F32),
            pltpu.VMEM((n_rows, aw), F32),
        ],
    )
    return pl.pallas_call(
        kern,
        grid_spec=grid_spec,
        out_shape=jax.ShapeDtypeStruct((n_b * n_t, aw), F32),
        name="attn_sample",
        compiler_params=_params(("arbitrary", "arbitrary")),
    )(page_table, q, k_new, v_new, lq1, lk1, lq2, lk2, subln_g,
      *([cache_kt] * pages_per_step), *([cache_v] * pages_per_step))


def _cross_kernel(o_ref, c_ref, x_ref, wout_ref, gpost_ref, gpre_ref, wmq_ref, mk_ref, mv_ref, wmo_ref,
                  gcpost_ref, y_ref, *, mem_heads):
    mix_in = jnp.concatenate([o_ref[...].astype(BF16), c_ref[...].astype(BF16)], axis=-1)
    mix = jnp.dot(mix_in, wout_ref[...], preferred_element_type=F32)
    x1 = x_ref[...] + _rms(mix, gpost_ref[...])
    hq = _rms(x1, gpre_ref[...]).astype(BF16)
    d = x1.shape[-1]
    hd = d // mem_heads
    q = (jnp.dot(hq, wmq_ref[...], preferred_element_type=F32) * (hd ** -0.5)).astype(BF16)
    n_lt = hd // LANES
    n_mem = mk_ref.shape[1] // (mem_heads * n_lt)

    def head_rows(ref, h):
        return jnp.concatenate(
            [ref[0, pl.ds(lt * mem_heads + h, n_mem, stride=mem_heads * n_lt), :] for lt in range(n_lt)],
            axis=-1).astype(BF16)

    outs = []
    for h in range(mem_heads):
        sl = slice(h * hd, (h + 1) * hd)
        mk = head_rows(mk_ref, h)
        mv = head_rows(mv_ref, h)
        s = lax.dot_general(q[:, sl], mk, (((1,), (1,)), ((), ())), preferred_element_type=F32)
        p = jnp.exp(s - jnp.max(s, axis=-1, keepdims=True))
        p = p / jnp.sum(p, axis=-1, keepdims=True)
        outs.append(jnp.dot(p.astype(BF16), mv, preferred_element_type=F32).astype(BF16))
    att = jnp.dot(jnp.concatenate(outs, axis=-1), wmo_ref[...], preferred_element_type=F32)
    y_ref[...] = x1 + _rms(att, gcpost_ref[...])


def _cross(o, c, x, w_out, g_post, g_pre, w_mq, mem_k, mem_v, w_mo, g_cpost, *, tm, mem_heads):
    t, d = x.shape
    aw = o.shape[1]
    cwid = c.shape[1]
    mem_rows, mem_hd = mem_k.shape[1:]
    row = lambda i: (i, 0)
    return pl.pallas_call(
        functools.partial(_cross_kernel, mem_heads=mem_heads),
        grid=(t // tm,),
        in_specs=[
            pl.BlockSpec((tm, aw), row),
            pl.BlockSpec((tm, cwid), row),
            pl.BlockSpec((tm, d), row),
            _const_spec(w_out.shape),
            _const_spec((1, d)),
            _const_spec((1, d)),
            _const_spec(w_mq.shape),
            _const_spec((1, mem_rows, mem_hd)),
            _const_spec((1, mem_rows, mem_hd)),
            _const_spec(w_mo.shape),
            _const_spec((1, d)),
        ],
        out_specs=pl.BlockSpec((tm, d), row),
        out_shape=jax.ShapeDtypeStruct((t, d), F32),
        name="cross_prompt",
        compiler_params=_params(("arbitrary",)),
    )(o, c, x, w_out, g_post, g_pre, w_mq, mem_k, mem_v, w_mo, g_cpost)


def _cross_sample_kernel(o_ref, c_ref, x_ref, wout_ref, gpost_ref, gpre_ref, wmq_ref, mk_ref, mv_ref, wmo_ref,
                         gcpost_ref, y_ref, x1_scr, q_scr, att_scr, *, mem_heads, n_t):
    b = pl.program_id(0)
    d = x_ref.shape[-1]
    hd = d // mem_heads
    n_lt = hd // LANES
    n_mem = mk_ref.shape[1] // (mem_heads * n_lt)

    @pl.when(b == 0)
    def _():
        mix_in = jnp.concatenate([o_ref[...].astype(BF16), c_ref[...].astype(BF16)], axis=-1)
        mix = jnp.dot(mix_in, wout_ref[...], preferred_element_type=F32)
        x1 = x_ref[...] + _rms(mix, gpost_ref[...])
        x1_scr[...] = x1
        hq = _rms(x1, gpre_ref[...]).astype(BF16)
        q_scr[...] = jnp.dot(hq, wmq_ref[...], preferred_element_type=F32) * (hd ** -0.5)

    def head_rows(ref, h):
        return jnp.concatenate(
            [ref[0, pl.ds(lt * mem_heads + h, n_mem, stride=mem_heads * n_lt), :] for lt in range(n_lt)],
            axis=-1).astype(BF16)

    rows = pl.ds(pl.multiple_of(b * n_t, n_t), n_t)
    q = q_scr[rows, :].astype(BF16)
    for h in range(mem_heads):
        sl = slice(h * hd, (h + 1) * hd)
        s = lax.dot_general(q[:, sl], head_rows(mk_ref, h), (((1,), (1,)), ((), ())),
                            preferred_element_type=F32)
        p = jnp.exp(s - jnp.max(s, axis=-1, keepdims=True))
        p = p / jnp.sum(p, axis=-1, keepdims=True)
        att_scr[rows, sl] = jnp.dot(p.astype(BF16), head_rows(mv_ref, h), preferred_element_type=F32)

    @pl.when(b == pl.num_programs(0) - 1)
    def _():
        att = jnp.dot(att_scr[...].astype(BF16), wmo_ref[...], preferred_element_type=F32)
        y_ref[...] = x1_scr[...] + _rms(att, gcpost_ref[...])


def _cross_sample(o, c, x, w_out, g_post, g_pre, w_mq, mem_k, mem_v, w_mo, g_cpost, *, n_t, mem_heads):
    t, d = x.shape
    n_b, mem_rows, mem_hd = mem_k.shape
    assert n_t == SUBLANES and t == n_b * n_t
    mem_map = lambda b: (b, 0, 0)
    return pl.pallas_call(
        functools.partial(_cross_sample_kernel, mem_heads=mem_heads, n_t=n_t),
        grid=(n_b,),
        in_specs=[
            _const_spec(o.shape),
            _const_spec(c.shape),
            _const_spec(x.shape),
            _const_spec(w_out.shape),
            _const_spec((1, d)),
            _const_spec((1, d)),
            _const_spec(w_mq.shape),
            pl.BlockSpec((1, mem_rows, mem_hd), mem_map),
            pl.BlockSpec((1, mem_rows, mem_hd), mem_map),
            _const_spec(w_mo.shape),
            _const_spec((1, d)),
        ],
        out_specs=pl.BlockSpec((t, d), lambda b: (0, 0)),
        out_shape=jax.ShapeDtypeStruct((t, d), F32),
        scratch_shapes=[pltpu.VMEM((t, d), F32), pltpu.VMEM((t, d), F32), pltpu.VMEM((t, d), F32)],
        name="cross_sample",
        compiler_params=_params(("arbitrary",)),
    )(o, c, x, w_out, g_post, g_pre, w_mq, mem_k, mem_v, w_mo, g_cpost)


def _mlp_kernel(x_ref, gpre_ref, w1_ref, w2_ref, gpost_ref, y_ref):
    x = x_ref[...]
    h = _rms(x, gpre_ref[...]).astype(BF16)
    hf = jnp.dot(h, w1_ref[...], preferred_element_type=F32)
    r = jnp.maximum(hf, 0.0)
    f = jnp.dot((r * r).astype(BF16), w2_ref[...], preferred_element_type=F32)
    y_ref[...] = x + _rms(f, gpost_ref[...])


def _mlp(x, g_pre, w1, w2, g_post, *, tm):
    t, d = x.shape
    row = lambda i: (i, 0)
    return pl.pallas_call(
        _mlp_kernel,
        grid=(t // tm,),
        in_specs=[pl.BlockSpec((tm, d), row), _const_spec((1, d)), _const_spec(w1.shape),
                  _const_spec(w2.shape), _const_spec((1, d))],
        out_specs=pl.BlockSpec((tm, d), row),
        out_shape=jax.ShapeDtypeStruct((t, d), F32),
        name="mlp_tile%d" % tm,
        compiler_params=_params(("arbitrary",)),
    )(x, g_pre, w1, w2, g_post)


def _rope_tables(pos):
    half = HEAD_DIM // 2
    inv = ROPE_THETA ** (-jnp.arange(0, half, dtype=F32) * 2.0 / HEAD_DIM)
    ang = pos.astype(F32)[:, None] * inv[None, :]
    cos = jnp.cos(ang)
    sin = jnp.sin(ang)
    reps = LANES // HEAD_DIM
    return (jnp.tile(jnp.concatenate([cos, cos], axis=-1), (1, reps)),
            jnp.tile(jnp.concatenate([-sin, sin], axis=-1), (1, reps)))


def _conv_toeplitz(conv_w, n_t):
    n_r = CONV_K - 1 + n_t
    r = jnp.arange(n_r)[:, None]
    t = jnp.arange(n_t)[None, :]
    j = r - t
    ok = (j >= 0) & (j < CONV_K)
    return jnp.where(ok[:, :, None], conv_w[jnp.clip(j, 0, CONV_K - 1)], 0.0)


def _layer(i, depth_inputs, x_prompt, x_sample, mem_prompt, cache_k, cache_v, cache_conv, cache_mem_k,
           cache_mem_v, page_table, *, tile, pages_per_step):
    (w_in, lq1, lk1, lq2, lk2, subln_g, conv_w, conv_b, ln_g, ln_b, w_out, mix_pre_g, mix_post_g, mem_norm_g,
     w_mq, w_mk, w_mv, w_mo, cross_pre_g, cross_post_g, w_ff1, w_ff2, mlp_pre_g, mlp_post_g) = depth_inputs
    lam_init = 0.8 - 0.6 * math.exp(-0.3 * i)
    _, seq, d = x_prompt.shape
    n_b, n_t, _ = x_sample.shape
    n_heads, v_dim = cache_v.shape[-2:]
    aw = n_heads * v_dim
    cwid = cache_conv.shape[-1]
    mem_heads = cache_mem_k.shape[-2]
    n_mem = mem_prompt.shape[1]
    past = page_table.shape[1] * cache_k.shape[1]

    row2 = lambda a: a.reshape(1, -1)
    w_in_b, w_out_b, w_mq_b, w_mk_b, w_mv_b, w_mo_b, w_ff1_b, w_ff2_b = (
        w.astype(BF16) for w in (w_in, w_out, w_mq, w_mk, w_mv, w_mo, w_ff1, w_ff2))
    lams = (row2(lq1), row2(lk1), row2(lq2), row2(lk2))

    mk_p, mv_p = _memory_kv(mem_prompt[0], row2(mem_norm_g), w_mk_b, w_mv_b, mem_heads=mem_heads)
    cos_p, sin_p = _rope_tables(jnp.arange(seq, dtype=jnp.int32))
    k_p, v_p, qt, kst, vt, c_p, tail_p = _inproj_prompt(
        x_prompt[0], row2(mix_pre_g), w_in_b, cos_p, sin_p, conv_w, row2(conv_b), row2(ln_g), row2(ln_b),
        tm=tile, aw=aw, cwid=cwid, n_heads=n_heads)
    o_p = _attn_prompt(qt, kst, vt, *lams, subln_g.reshape(-1, 1), lam_init=lam_init)
    x2_p = _cross(o_p, c_p, x_prompt[0], w_out_b, row2(mix_post_g), row2(cross_pre_g), w_mq_b,
                  mk_p[None], mv_p[None], w_mo_b, row2(cross_post_g), tm=tile, mem_heads=mem_heads)
    y_p = _mlp(x2_p, row2(mlp_pre_g), w_ff1_b, w_ff2_b, row2(mlp_post_g), tm=tile)

    pos_s = jnp.tile(past + jnp.arange(n_t, dtype=jnp.int32), n_b)
    cos_s, sin_s = _rope_tables(pos_s)
    xs = x_sample.reshape(n_b * n_t, d)
    k_s, v_s, q_s, glu_s, c_s = _inproj_sample(
        xs, row2(mix_pre_g), w_in_b, cos_s, sin_s, cache_conv, _conv_toeplitz(conv_w, n_t), row2(conv_b),
        row2(ln_g), row2(ln_b), aw=aw, cwid=cwid, n_heads=n_heads, n_b=n_b, n_t=n_t)
    n_pool, page = cache_k.shape[:2]
    cache_kt = jnp.transpose(cache_k, (0, 2, 3, 4, 1)).reshape(n_pool, aw, page)
    cache_vr = cache_v.reshape(n_pool, page * n_heads, v_dim)
    o_s = _attn_sample(page_table, q_s, k_s, v_s, cache_kt, cache_vr, *lams, row2(subln_g),
                       n_b=n_b, n_t=n_t, n_heads=n_heads, pages_per_step=pages_per_step, lam_init=lam_init)
    n_lt = d // mem_heads // LANES

    def mem_rows(a):
        a = a.reshape(a.shape[0], n_mem, mem_heads, n_lt, LANES)
        return jnp.transpose(a, (0, 1, 3, 2, 4)).reshape(a.shape[0], n_mem * n_lt * mem_heads, LANES)

    x2_s = _cross_sample(o_s, c_s.reshape(n_b * n_t, cwid), xs, w_out_b, row2(mix_post_g), row2(cross_pre_g),
                         w_mq_b, mem_rows(cache_mem_k), mem_rows(cache_mem_v), w_mo_b,
                         row2(cross_post_g), n_t=n_t, mem_heads=mem_heads)

    def mem_out(a):
        a = a.reshape(n_mem, n_lt, mem_heads, LANES)
        return jnp.transpose(a, (0, 2, 1, 3)).reshape(1, n_mem, mem_heads, n_lt * LANES)
    y_s = _mlp(x2_s, row2(mlp_pre_g), w_ff1_b, w_ff2_b, row2(mlp_post_g), tm=n_b * n_t)

    hd2 = (n_heads, 2, HEAD_DIM)
    outs = dict(
        y_p=y_p[None], y_s=y_s.reshape(n_b, n_t, d),
        k_p=jnp.transpose(k_p.reshape(*hd2, seq), (3, 0, 1, 2))[None],
        v_p=v_p.reshape(1, seq, n_heads, v_dim),
        conv_p=tail_p[CONV_HALO - (CONV_K - 1):][None],
        mk_p=mem_out(mk_p), mv_p=mem_out(mv_p),
        k_s=k_s.reshape(n_b, n_t, *hd2), v_s=v_s.reshape(n_b, n_t, n_heads, v_dim),
        conv_s=jnp.concatenate([cache_conv[:, n_t:], glu_s], axis=1),
    )
    return outs


def kernel(x_prompt, x_sample, mem_prompt, cache_k, cache_v, cache_conv, cache_mem_k, cache_mem_v, page_table, w_in, lambda_q1, lambda_k1, lambda_q2, lambda_k2, subln_g, conv_w, conv_b, conv_ln_g, conv_ln_b, w_out, mix_pre_g, mix_post_g, mem_norm_g, w_mq, w_mk, w_mv, w_mo, cross_pre_g, cross_post_g, w_ff1, w_ff2, mlp_pre_g, mlp_post_g):
    depth = w_in.shape[0]
    stacked = (w_in, lambda_q1, lambda_k1, lambda_q2, lambda_k2, subln_g, conv_w, conv_b, conv_ln_g, conv_ln_b,
               w_out, mix_pre_g, mix_post_g, mem_norm_g, w_mq, w_mk, w_mv, w_mo, cross_pre_g, cross_post_g,
               w_ff1, w_ff2, mlp_pre_g, mlp_post_g)
    tile = min(512, x_prompt.shape[1])
    pages_per_step = min(16, page_table.shape[1])
    y_p, y_s = x_prompt, x_sample
    per_layer = []
    for i in range(depth):
        o = _layer(i, tuple(a[i] for a in stacked), y_p, y_s, mem_prompt, cache_k[i], cache_v[i], cache_conv[i],
                   cache_mem_k[i], cache_mem_v[i], page_table, tile=tile, pages_per_step=pages_per_step)
        y_p, y_s = o["y_p"], o["y_s"]
        per_layer.append(o)
    stack = lambda name: jnp.stack([o[name] for o in per_layer], axis=0)
    return (y_p, y_s, stack("k_p"), stack("v_p"), stack("conv_p"), stack("mk_p"), stack("mv_p"),
            stack("k_s"), stack("v_s"), stack("conv_s"))
```

```python
import functools
import math

import jax
import jax.numpy as jnp
from jax import lax
from jax.experimental import pallas as pl
from jax.experimental.pallas import tpu as pltpu

F32 = jnp.float32
BF16 = jnp.bfloat16

RMS_EPS = 1e-6
LN_EPS = 1e-5
ROPE_THETA = 10000.0
LANES = 128
SUBLANES = 8
HEAD_DIM = 64
V_DIM = 2 * HEAD_DIM
CONV_K = 31
CONV_HALO = 32
SUM_ROWS = 16
VMEM_LIMIT = 56 * 1024 * 1024


def _rms(x, g):
    return x * lax.rsqrt(jnp.mean(x * x, axis=-1, keepdims=True) + RMS_EPS) * g


def _lam(lq1, lk1, lq2, lk2, lam_init):
    a = jnp.sum(lq1[...] * lk1[...], axis=-1, keepdims=True)
    b = jnp.sum(lq2[...] * lk2[...], axis=-1, keepdims=True)
    return jnp.exp(a) - jnp.exp(b) + lam_init


def _rope_lanes(x, cos, sin_signed, first_half):
    rot = jnp.where(first_half, pltpu.roll(x, LANES - HEAD_DIM // 2, 1), pltpu.roll(x, HEAD_DIM // 2, 1))
    return x * cos + rot * sin_signed


def _const_spec(shape):
    nd = len(shape)
    return pl.BlockSpec(shape, lambda *_: (0,) * nd, pipeline_mode=pl.Buffered(1))


def _params(sem):
    return pltpu.CompilerParams(dimension_semantics=sem, vmem_limit_bytes=VMEM_LIMIT)


def _memory_kv_kernel(mem_ref, g_ref, wk_ref, wv_ref, mk_ref, mv_ref, *, mem_heads):
    m = _rms(mem_ref[...], g_ref[...]).astype(BF16)
    n, d = mem_ref.shape
    hd = d // mem_heads
    mk = jnp.dot(m, wk_ref[...], preferred_element_type=F32)
    mv = jnp.dot(m, wv_ref[...], preferred_element_type=F32)
    n_lt = hd // LANES
    for h in range(mem_heads):
        for lt in range(n_lt):
            rows = pl.ds(lt * mem_heads + h, n, stride=mem_heads * n_lt)
            col = h * hd + lt * LANES
            mk_ref[rows, :] = mk[:, col:col + LANES]
            mv_ref[rows, :] = mv[:, col:col + LANES]


def _memory_kv(mem, g, wk, wv, *, mem_heads):
    n, d = mem.shape
    out = jax.ShapeDtypeStruct((n * d // LANES, LANES), F32)
    return pl.pallas_call(
        functools.partial(_memory_kv_kernel, mem_heads=mem_heads),
        out_shape=(out, out),
        name="memory_kv",
        compiler_params=pltpu.CompilerParams(vmem_limit_bytes=VMEM_LIMIT),
    )(mem, g, wk, wv)


def _inproj_prompt_kernel(x_ref, g_ref, w_ref, cos_ref, sin_ref, cw_ref, cb_ref, lng_ref, lnb_ref,
                          kout_ref, vout_ref, qt_ref, kst_ref, vt_ref, c_ref, tail_ref, gpad_ref, gsh_ref,
                          *, tm, aw, cwid, n_heads, scale, row_chunk):
    i = pl.program_id(0)

    @pl.when(i == 0)
    def _():
        gpad_ref[0:CONV_HALO, :] = jnp.zeros((CONV_HALO, cwid), F32)

    h = _rms(x_ref[...], g_ref[...]).astype(BF16)
    cos = cos_ref[...]
    sin = sin_ref[...]
    lane = lax.broadcasted_iota(jnp.int32, (tm, LANES), 1)
    first_half = (lane % HEAD_DIM) < (HEAD_DIM // 2)
    lo = lane < HEAD_DIM
    sum_rows = (lax.broadcasted_iota(jnp.int32, (SUM_ROWS, tm), 0) == 0).astype(BF16)

    q = jnp.dot(h, w_ref[:, 0:aw], preferred_element_type=F32)
    k = jnp.dot(h, w_ref[:, aw:2 * aw], preferred_element_type=F32)
    v = jnp.dot(h, w_ref[:, 2 * aw:3 * aw], preferred_element_type=F32)
    for hd in range(n_heads):
        sl = slice(hd * LANES, (hd + 1) * LANES)
        qh = _rope_lanes(q[:, sl], cos, sin, first_half) * scale
        kh = _rope_lanes(k[:, sl], cos, sin, first_half)
        kout_ref[sl, :] = kh.T
        vout_ref[pl.ds(hd, tm, stride=n_heads), :] = v[:, sl]
        qt_ref[hd, 0] = qh.T.astype(BF16)
        kst_ref[hd, 0, 0:tm, :] = jnp.where(lo, kh, 0.0).astype(BF16)
        kst_ref[hd, 0, tm:2 * tm, :] = jnp.where(lo, 0.0, kh).astype(BF16)
        vt_ref[hd, 0, 0:V_DIM, :] = v[:, sl].T.astype(BF16)
        vt_ref[hd, 0, V_DIM:V_DIM + SUM_ROWS, :] = sum_rows

    u1 = jnp.dot(h, w_ref[:, 3 * aw:3 * aw + cwid], preferred_element_type=F32)
    u2 = jnp.dot(h, w_ref[:, 3 * aw + cwid:3 * aw + 2 * cwid], preferred_element_type=F32)
    gpad_ref[CONV_HALO:CONV_HALO + tm, :] = u1 * jax.nn.sigmoid(u2)

    base = CONV_HALO - (CONV_K - 1)
    cb = cb_ref[...]
    lng = lng_ref[...]
    lnb = lnb_ref[...]
    n_sh = gsh_ref.shape[1]
    for b in range(1, SUBLANES):
        gsh_ref[b - 1] = gpad_ref[pl.ds(b, n_sh), :]
    for r0 in range(0, tm, row_chunk):
        acc = jnp.zeros((row_chunk, cwid), F32)
        for j in range(CONV_K):
            a, b = divmod(base + j, SUBLANES)
            rows = pl.ds(r0 + a * SUBLANES, row_chunk)
            tap = gpad_ref[rows, :] if b == 0 else gsh_ref[b - 1, rows, :]
            acc = acc + cw_ref[j:j + 1, :] * tap
        acc = acc + cb
        mu = jnp.mean(acc, axis=-1, keepdims=True)
        d = acc - mu
        var = jnp.mean(d * d, axis=-1, keepdims=True)
        y = d * lax.rsqrt(var + LN_EPS) * lng + lnb
        c_ref[r0:r0 + row_chunk, :] = (y * jax.nn.sigmoid(y)).astype(c_ref.dtype)

    tail_ref[...] = gpad_ref[tm:tm + CONV_HALO, :]
    gpad_ref[0:CONV_HALO, :] = gpad_ref[tm:tm + CONV_HALO, :]


def _inproj_prompt(x, g, w_in, cos, sin, conv_w, conv_b, ln_g, ln_b, *, tm, aw, cwid, n_heads):
    t, d = x.shape
    nt = t // tm
    kern = functools.partial(_inproj_prompt_kernel, tm=tm, aw=aw, cwid=cwid, n_heads=n_heads,
                             scale=HEAD_DIM ** -0.5 * math.log2(math.e), row_chunk=min(tm, 32))
    row = lambda i: (i, 0)
    return pl.pallas_call(
        kern,
        grid=(nt,),
        in_specs=[
            pl.BlockSpec((tm, d), row),
            _const_spec((1, d)),
            _const_spec(w_in.shape),
            pl.BlockSpec((tm, LANES), row),
            pl.BlockSpec((tm, LANES), row),
            _const_spec(conv_w.shape),
            _const_spec((1, cwid)),
            _const_spec((1, cwid)),
            _const_spec((1, cwid)),
        ],
        out_specs=[
            pl.BlockSpec((aw, tm), lambda i: (0, i)),
            pl.BlockSpec((tm * n_heads, V_DIM), row),
            pl.BlockSpec((n_heads, 1, LANES, tm), lambda i: (0, i, 0, 0)),
            pl.BlockSpec((n_heads, 1, 2 * tm, LANES), lambda i: (0, i, 0, 0)),
            pl.BlockSpec((n_heads, 1, V_DIM + SUM_ROWS, tm), lambda i: (0, i, 0, 0)),
            pl.BlockSpec((tm, cwid), row),
            pl.BlockSpec((CONV_HALO, cwid), lambda i: (0, 0)),
        ],
        out_shape=(
            jax.ShapeDtypeStruct((aw, t), F32),
            jax.ShapeDtypeStruct((t * n_heads, V_DIM), F32),
            jax.ShapeDtypeStruct((n_heads, nt, LANES, tm), BF16),
            jax.ShapeDtypeStruct((n_heads, nt, 2 * tm, LANES), BF16),
            jax.ShapeDtypeStruct((n_heads, nt, V_DIM + SUM_ROWS, tm), BF16),
            jax.ShapeDtypeStruct((t, cwid), BF16),
            jax.ShapeDtypeStruct((CONV_HALO, cwid), F32),
        ),
        scratch_shapes=[pltpu.VMEM((tm + CONV_HALO, cwid), F32),
                        pltpu.VMEM((SUBLANES - 1, tm + CONV_HALO - SUBLANES, cwid), F32)],
        name="inproj_prompt",
        compiler_params=_params(("arbitrary",)),
    )(x, g, w_in, cos, sin, conv_w, conv_b, ln_g, ln_b)


def _attn_prompt_kernel(qt_ref, kst_ref, vt_ref, lq1, lk1, lq2, lk2, sg_ref, o_ref,
                        m_ref, acc_ref, q_scr, sa_ref, sb_ref, *, tk, lam_init):
    i = pl.program_id(1)
    tq = 2 * tk
    q_scr[:, 0:tk] = qt_ref[0, 0]
    q_scr[:, tk:tq] = qt_ref[0, 1]
    m_ref[...] = jnp.full(m_ref.shape, -jnp.inf, F32)
    acc_ref[...] = jnp.zeros(acc_ref.shape, F32)

    def scores(j, s_ref, key_offset=None):
        st = jnp.dot(kst_ref[0, j], q_scr[...], preferred_element_type=F32)
        if key_offset is None:
            s_ref[...] = st
        else:
            keep = (lax.broadcasted_iota(jnp.int32, (tk, tq), 0) + key_offset
                    <= lax.broadcasted_iota(jnp.int32, (tk, tq), 1))
            for c in range(2):
                s_ref[c * tk:(c + 1) * tk, :] = jnp.where(keep, st[c * tk:(c + 1) * tk, :], -jnp.inf)

    def absorb(j, s_ref):
        vt = vt_ref[0, j]
        for c in range(2):
            s = s_ref[c * tk:(c + 1) * tk, :]
            m_prev = m_ref[c]
            m_new = jnp.maximum(m_prev, jnp.max(s, axis=0, keepdims=True))
            p = jnp.exp2(s - m_new).astype(BF16)
            alpha = jnp.exp2(m_prev - m_new)
            acc_ref[c] = alpha * acc_ref[c] + jnp.dot(vt, p, preferred_element_type=F32)
            m_ref[c] = m_new

    scores(2 * i, sa_ref, key_offset=0)
    scores(2 * i + 1, sb_ref, key_offset=tk)
    absorb(2 * i, sa_ref)

    def pair(t, pending):
        scores(2 * t, sa_ref)
        absorb(pending, sb_ref)
        scores(2 * t + 1, sb_ref)
        absorb(2 * t, sa_ref)
        return 2 * t + 1

    pending = lax.fori_loop(0, i, pair, 2 * i + 1)
    absorb(pending, sb_ref)

    lam = _lam(lq1, lk1, lq2, lk2, lam_init)
    o1 = acc_ref[0, 0:V_DIM, :] / acc_ref[0, V_DIM:V_DIM + 1, :]
    o2 = acc_ref[1, 0:V_DIM, :] / acc_ref[1, V_DIM:V_DIM + 1, :]
    a = o1 - lam * o2
    y = a * lax.rsqrt(jnp.mean(a * a, axis=0, keepdims=True) + RMS_EPS) * sg_ref[...] * (1.0 - lam_init)
    o_ref[...] = y.T.astype(o_ref.dtype)


def _attn_prompt(qt, kst, vt, lq1, lk1, lq2, lk2, subln_col, *, lam_init):
    n_heads, nt, _, tk = qt.shape
    assert nt % 2 == 0, "a query block spans two key blocks"
    t = nt * tk
    tq = 2 * tk
    kern = functools.partial(_attn_prompt_kernel, tk=tk, lam_init=lam_init)
    vec = pl.BlockSpec((1, HEAD_DIM), lambda h, i: (0, 0))
    return pl.pallas_call(
        kern,
        grid=(n_heads, nt // 2),
        in_specs=[
            pl.BlockSpec((1, 2, LANES, tk), lambda h, i: (h, i, 0, 0)),
            pl.BlockSpec((1, nt, 2 * tk, LANES), lambda h, i: (h, 0, 0, 0)),
            pl.BlockSpec((1, nt, V_DIM + SUM_ROWS, tk), lambda h, i: (h, 0, 0, 0)),
            vec, vec, vec, vec,
            pl.BlockSpec((V_DIM, 1), lambda h, i: (0, 0)),
        ],
        out_specs=pl.BlockSpec((tq, V_DIM), lambda h, i: (i, h)),
        out_shape=jax.ShapeDtypeStruct((t, n_heads * V_DIM), BF16),
        scratch_shapes=[pltpu.VMEM((2, 1, tq), F32),
                        pltpu.VMEM((2, V_DIM + SUM_ROWS, tq), F32),
                        pltpu.VMEM((LANES, tq), BF16),
                        pltpu.VMEM((2 * tk, tq), F32),
                        pltpu.VMEM((2 * tk, tq), F32)],
        name="attn_prompt",
        compiler_params=_params(("arbitrary", "arbitrary")),
    )(qt, kst, vt, lq1, lk1, lq2, lk2, subln_col)


def _inproj_sample_kernel(x_ref, g_ref, w_ref, cos_ref, sin_ref, cc_ref, toep_ref, cb_ref, lng_ref, lnb_ref,
                          kout_ref, vout_ref, q_ref, glu_ref, c_ref, *, aw, cwid, n_heads, scale, n_b, n_t):
    rows = n_b * n_t
    h = _rms(x_ref[...], g_ref[...]).astype(BF16)
    cos = cos_ref[...]
    sin = sin_ref[...]
    lane = lax.broadcasted_iota(jnp.int32, (rows, LANES), 1)
    first_half = (lane % HEAD_DIM) < (HEAD_DIM // 2)
    q = jnp.dot(h, w_ref[:, 0:aw], preferred_element_type=F32)
    k = jnp.dot(h, w_ref[:, aw:2 * aw], preferred_element_type=F32)
    vout_ref[...] = jnp.dot(h, w_ref[:, 2 * aw:3 * aw], preferred_element_type=F32)
    for hd in range(n_heads):
        sl = slice(hd * LANES, (hd + 1) * LANES)
        q_ref[:, sl] = _rope_lanes(q[:, sl], cos, sin, first_half) * scale
        kout_ref[:, sl] = _rope_lanes(k[:, sl], cos, sin, first_half)
    u1 = jnp.dot(h, w_ref[:, 3 * aw:3 * aw + cwid], preferred_element_type=F32)
    u2 = jnp.dot(h, w_ref[:, 3 * aw + cwid:3 * aw + 2 * cwid], preferred_element_type=F32)
    glu_ref[...] = (u1 * jax.nn.sigmoid(u2)).reshape(n_b, n_t, cwid)

    acc = jnp.zeros((n_b, n_t, cwid), F32)
    for r in range(CONV_K - 1):
        acc = acc + cc_ref[:, r:r + 1, :] * toep_ref[r]
    for r in range(n_t):
        acc = acc + glu_ref[:, r:r + 1, :] * toep_ref[CONV_K - 1 + r]
    acc = acc + cb_ref[...]
    mu = jnp.mean(acc, axis=-1, keepdims=True)
    d = acc - mu
    var = jnp.mean(d * d, axis=-1, keepdims=True)
    y = d * lax.rsqrt(var + LN_EPS) * lng_ref[...] + lnb_ref[...]
    c_ref[...] = y * jax.nn.sigmoid(y)


def _inproj_sample(x, g, w_in, cos, sin, cache_conv, toep, conv_b, ln_g, ln_b, *, aw, cwid, n_heads, n_b, n_t):
    rows = n_b * n_t
    kern = functools.partial(_inproj_sample_kernel, aw=aw, cwid=cwid, n_heads=n_heads,
                             scale=HEAD_DIM ** -0.5, n_b=n_b, n_t=n_t)
    return pl.pallas_call(
        kern,
        out_shape=(
            jax.ShapeDtypeStruct((rows, aw), F32),
            jax.ShapeDtypeStruct((rows, aw), F32),
            jax.ShapeDtypeStruct((rows, aw), F32),
            jax.ShapeDtypeStruct((n_b, n_t, cwid), F32),
            jax.ShapeDtypeStruct((n_b, n_t, cwid), F32),
        ),
        name="inproj_sample",
        compiler_params=pltpu.CompilerParams(vmem_limit_bytes=VMEM_LIMIT),
    )(x, g, w_in, cos, sin, cache_conv, toep, conv_b, ln_g, ln_b)


def _attn_sample_kernel(pt_ref, q_ref, kn_ref, vn_ref, lq1, lk1, lq2, lk2, sg_ref, *rest,
                        n_pages, page, n_heads, n_t, lam_init):
    k_pages = rest[:n_pages]
    v_pages = rest[n_pages:2 * n_pages]
    o_ref = rest[2 * n_pages]
    kbuf, vbuf, m_ref, l_ref, acc_ref = rest[2 * n_pages + 1:]
    j = pl.program_id(1)
    aw = n_heads * V_DIM
    n_rows = n_heads * 2 * n_t

    qrep = jnp.concatenate([q_ref[...]] * (n_heads * 2), axis=0)
    sel = (lax.broadcasted_iota(jnp.int32, (n_rows, aw), 0) // n_t
           == lax.broadcasted_iota(jnp.int32, (n_rows, aw), 1) // HEAD_DIM)
    wq = jnp.where(sel, qrep, 0.0).astype(BF16)

    def update(s, vals):
        m_prev = m_ref[...]
        m_new = jnp.maximum(m_prev, jnp.max(s, axis=-1, keepdims=True))
        p = jnp.exp(s - m_new)
        alpha = jnp.exp(m_prev - m_new)
        l_ref[...] = alpha * l_ref[...] + jnp.sum(p, axis=-1, keepdims=True)
        acc_ref[...] = alpha * acc_ref[...] + jnp.dot(p.astype(BF16), vals, preferred_element_type=F32)
        m_ref[...] = m_new

    @pl.when(j == 0)
    def _():
        m_ref[...] = jnp.full(m_ref.shape, -jnp.inf, F32)
        l_ref[...] = jnp.zeros(l_ref.shape, F32)
        acc_ref[...] = jnp.zeros(acc_ref.shape, F32)
        pad = jnp.zeros((LANES - n_t, aw), F32)
        kn = jnp.concatenate([kn_ref[...], pad], axis=0).astype(BF16)
        vn = jnp.concatenate([vn_ref[...], pad], axis=0).astype(BF16)
        s = lax.dot_general(wq, kn, (((1,), (1,)), ((), ())), preferred_element_type=F32)
        col = lax.broadcasted_iota(jnp.int32, (n_rows, LANES), 1)
        qt = lax.broadcasted_iota(jnp.int32, (n_rows, LANES), 0) % n_t
        update(jnp.where(col <= qt, s, -jnp.inf), vn)

    for p in range(n_pages):
        kbuf[:, p * page:(p + 1) * page] = k_pages[p][0].astype(BF16)
        for hd in range(n_heads):
            vbuf[p * page:(p + 1) * page, hd * V_DIM:(hd + 1) * V_DIM] = (
                v_pages[p][0, pl.ds(hd, page, stride=n_heads), :].astype(BF16))
    s = jnp.dot(wq, kbuf[...], preferred_element_type=F32)
    update(s, vbuf[...])

    @pl.when(j == pl.num_programs(1) - 1)
    def _():
        lam = _lam(lq1, lk1, lq2, lk2, lam_init)
        o = acc_ref[...] / l_ref[...]
        for hd in range(n_heads):
            r0 = hd * 2 * n_t
            sl = slice(hd * V_DIM, (hd + 1) * V_DIM)
            a = o[r0:r0 + n_t, sl] - lam * o[r0 + n_t:r0 + 2 * n_t, sl]
            o_ref[:, sl] = _rms(a, sg_ref[...]) * (1.0 - lam_init)


def _attn_sample(page_table, q, k_new, v_new, cache_kt, cache_v, lq1, lk1, lq2, lk2, subln_g,
                 *, n_b, n_t, n_heads, pages_per_step, lam_init):
    n_pool, aw, page = cache_kt.shape
    n_chunks = page_table.shape[1] // pages_per_step
    n_rows = n_heads * 2 * n_t
    kern = functools.partial(_attn_sample_kernel, n_pages=pages_per_step, page=page, n_heads=n_heads,
                             n_t=n_t, lam_init=lam_init)
    tok = pl.BlockSpec((n_t, aw), lambda b, j, pt: (b, 0))
    vec = pl.BlockSpec((1, HEAD_DIM), lambda b, j, pt: (0, 0))

    def page_spec(p, shape):
        return pl.BlockSpec((1,) + shape, lambda b, j, pt: (pt[b, j * pages_per_step + p], 0, 0))

    k_specs = [page_spec(p, (aw, page)) for p in range(pages_per_step)]
    v_specs = [page_spec(p, (page * n_heads, V_DIM)) for p in range(pages_per_step)]
    grid_spec = pltpu.PrefetchScalarGridSpec(
        num_scalar_prefetch=1,
        grid=(n_b, n_chunks),
        in_specs=[tok, tok, tok, vec, vec, vec, vec, pl.BlockSpec((1, V_DIM), lambda b, j, pt: (0, 0))]
        + k_specs + v_specs,
        out_specs=pl.BlockSpec((n_t, aw), lambda b, j, pt: (b, 0)),
        scratch_shapes=[
            pltpu.VMEM((aw, pages_per_step * page), BF16),
            pltpu.VMEM((pages_per_step * page, aw), BF16),
            pltpu.VMEM((n_rows, 1), F32),
            pltpu.VMEM((n_rows, 1), F32),
            pltpu.VMEM((n_rows, aw), F32),
        ],
    )
    return pl.pallas_call(
        kern,
        grid_spec=grid_spec,
        out_shape=jax.ShapeDtypeStruct((n_b * n_t, aw), F32),
        name="attn_sample",
        compiler_params=_params(("arbitrary", "arbitrary")),
    )(page_table, q, k_new, v_new, lq1, lk1, lq2, lk2, subln_g,
      *([cache_kt] * pages_per_step), *([cache_v] * pages_per_step))


def _prompt_attn_step(qt_ref, kst_ref, vt_ref, lam, sg_ref, o_ref, m_ref, acc_ref, q_scr, sa_ref, sb_ref, pend_ref,
                      *, i, s, n_sub, tk, lam_init):
    tq = 2 * tk

    def scores(j, s_ref, key_offset=None):
        st = jnp.dot(kst_ref[0, j], q_scr[...], preferred_element_type=F32)
        if key_offset is None:
            s_ref[...] = st
        else:
            keep = (lax.broadcasted_iota(jnp.int32, (tk, tq), 0) + key_offset
                    <= lax.broadcasted_iota(jnp.int32, (tk, tq), 1))
            for c in range(2):
                s_ref[c * tk:(c + 1) * tk, :] = jnp.where(keep, st[c * tk:(c + 1) * tk, :], -jnp.inf)

    def absorb(j, s_ref):
        vt = vt_ref[0, j]
        for c in range(2):
            sc = s_ref[c * tk:(c + 1) * tk, :]
            m_prev = m_ref[c]
            m_new = jnp.maximum(m_prev, jnp.max(sc, axis=0, keepdims=True))
            p = jnp.exp2(sc - m_new).astype(BF16)
            alpha = jnp.exp2(m_prev - m_new)
            acc_ref[c] = alpha * acc_ref[c] + jnp.dot(vt, p, preferred_element_type=F32)
            m_ref[c] = m_new

    @pl.when(s == 0)
    def _():
        q_scr[:, 0:tk] = qt_ref[0, 0]
        q_scr[:, tk:tq] = qt_ref[0, 1]
        m_ref[...] = jnp.full(m_ref.shape, -jnp.inf, F32)
        acc_ref[...] = jnp.zeros(acc_ref.shape, F32)
        scores(2 * i, sa_ref, key_offset=0)
        scores(2 * i + 1, sb_ref, key_offset=tk)
        absorb(2 * i, sa_ref)
        pend_ref[0] = 2 * i + 1

    def pair(t, pending):
        scores(2 * t, sa_ref)
        absorb(pending, sb_ref)
        scores(2 * t + 1, sb_ref)
        absorb(2 * t, sa_ref)
        return 2 * t + 1

    pending = lax.fori_loop((i * s) // n_sub, (i * (s + 1)) // n_sub, pair, pend_ref[0])
    pend_ref[0] = pending

    @pl.when(s == n_sub - 1)
    def _():
        absorb(pending, sb_ref)
        o1 = acc_ref[0, 0:V_DIM, :] / acc_ref[0, V_DIM:V_DIM + 1, :]
        o2 = acc_ref[1, 0:V_DIM, :] / acc_ref[1, V_DIM:V_DIM + 1, :]
        a = o1 - lam * o2
        y = a * lax.rsqrt(jnp.mean(a * a, axis=0, keepdims=True) + RMS_EPS) * sg_ref[...] * (1.0 - lam_init)
        o_ref[...] = y.T.astype(o_ref.dtype)


def _sample_attn_step(q_ref, kn_ref, vn_ref, lam, sg_ref, k_pages, v_pages, o_ref, kbuf, vbuf, m_ref, l_ref, acc_ref,
                      *, j, n_chunks, page, n_heads, n_t, lam_init):
    n_pages = len(k_pages)
    aw = n_heads * V_DIM
    n_rows = n_heads * 2 * n_t

    qrep = jnp.concatenate([q_ref[...]] * (n_heads * 2), axis=0)
    sel = (lax.broadcasted_iota(jnp.int32, (n_rows, aw), 0) // n_t
           == lax.broadcasted_iota(jnp.int32, (n_rows, aw), 1) // HEAD_DIM)
    wq = jnp.where(sel, qrep, 0.0).astype(BF16)

    def update(sc, vals):
        m_prev = m_ref[...]
        m_new = jnp.maximum(m_prev, jnp.max(sc, axis=-1, keepdims=True))
        p = jnp.exp(sc - m_new)
        alpha = jnp.exp(m_prev - m_new)
        l_ref[...] = alpha * l_ref[...] + jnp.sum(p, axis=-1, keepdims=True)
        acc_ref[...] = alpha * acc_ref[...] + jnp.dot(p.astype(BF16), vals, preferred_element_type=F32)
        m_ref[...] = m_new

    @pl.when(j == 0)
    def _():
        m_ref[...] = jnp.full(m_ref.shape, -jnp.inf, F32)
        l_ref[...] = jnp.zeros(l_ref.shape, F32)
        acc_ref[...] = jnp.zeros(acc_ref.shape, F32)
        pad = jnp.zeros((LANES - n_t, aw), F32)
        kn = jnp.concatenate([kn_ref[...], pad], axis=0).astype(BF16)
        vn = jnp.concatenate([vn_ref[...], pad], axis=0).astype(BF16)
        sc = lax.dot_general(wq, kn, (((1,), (1,)), ((), ())), preferred_element_type=F32)
        col = lax.broadcasted_iota(jnp.int32, (n_rows, LANES), 1)
        qt = lax.broadcasted_iota(jnp.int32, (n_rows, LANES), 0) % n_t
        update(jnp.where(col <= qt, sc, -jnp.inf), vn)

    for p in range(n_pages):
        kbuf[:, p * page:(p + 1) * page] = k_pages[p][0].astype(BF16)
        for hd in range(n_heads):
            vbuf[p * page:(p + 1) * page, hd * V_DIM:(hd + 1) * V_DIM] = (
                v_pages[p][0, pl.ds(hd, page, stride=n_heads), :].astype(BF16))
    update(jnp.dot(wq, kbuf[...], preferred_element_type=F32), vbuf[...])

    @pl.when(j == n_chunks - 1)
    def _():
        o = acc_ref[...] / l_ref[...]
        for hd in range(n_heads):
            r0 = hd * 2 * n_t
            sl = slice(hd * V_DIM, (hd + 1) * V_DIM)
            a = o[r0:r0 + n_t, sl] - lam * o[r0 + n_t:r0 + 2 * n_t, sl]
            o_ref[:, sl] = _rms(a, sg_ref[...]) * (1.0 - lam_init)


def _attn_kernel(pt_ref, qt_ref, kst_ref, vt_ref, lq1, lk1, lq2, lk2, sgcol_ref, sgrow_ref, qs_ref, kn_ref, vn_ref,
                 *rest, n_pages, n_sub, n_chunks, tk, page, n_heads, n_t, lam_init):
    k_pages = rest[:n_pages]
    v_pages = rest[n_pages:2 * n_pages]
    op_ref, os_ref = rest[2 * n_pages:2 * n_pages + 2]
    (m_ref, acc_ref, q_scr, sa_ref, sb_ref, pend_ref,
     kbuf, vbuf, ms_ref, ls_ref, accs_ref) = rest[2 * n_pages + 2:]
    i = pl.program_id(1)
    s = pl.program_id(2)
    step = (pl.program_id(0) * pl.num_programs(1) + i) * n_sub + s
    lam = _lam(lq1, lk1, lq2, lk2, lam_init)
    _sample_attn_step(qs_ref, kn_ref, vn_ref, lam, sgrow_ref, k_pages, v_pages, os_ref, kbuf, vbuf,
                      ms_ref, ls_ref, accs_ref, j=step % n_chunks, n_chunks=n_chunks, page=page,
                      n_heads=n_heads, n_t=n_t, lam_init=lam_init)
    _prompt_attn_step(qt_ref, kst_ref, vt_ref, lam, sgcol_ref, op_ref, m_ref, acc_ref, q_scr, sa_ref, sb_ref,
                      pend_ref, i=i, s=s, n_sub=n_sub, tk=tk, lam_init=lam_init)


def _attention(page_table, qt, kst, vt, lq1, lk1, lq2, lk2, subln_g, q_s, k_new, v_new, cache_kt, cache_v,
               *, n_b, n_t, pages_per_step, lam_init):
    n_heads, nt, _, tk = qt.shape
    assert nt % 2 == 0, "a query block spans two key blocks"
    nq = nt // 2
    tq = 2 * tk
    t = nt * tk
    n_pool, aw, page = cache_kt.shape
    n_chunks = page_table.shape[1] // pages_per_step
    n_sub = (n_b * n_chunks) // (n_heads * nq)
    assert n_sub >= 1 and n_sub * n_heads * nq == n_b * n_chunks, "sample chunks must tile the prompt grid"
    n_rows = n_heads * 2 * n_t
    kern = functools.partial(_attn_kernel, n_pages=pages_per_step, n_sub=n_sub, n_chunks=n_chunks, tk=tk,
                             page=page, n_heads=n_heads, n_t=n_t, lam_init=lam_init)

    def step_of(h, i, s):
        return (h * nq + i) * n_sub + s

    vec = pl.BlockSpec((1, HEAD_DIM), lambda h, i, s, pt: (0, 0))
    tok = pl.BlockSpec((n_t, aw), lambda h, i, s, pt: (step_of(h, i, s) // n_chunks, 0))

    def page_spec(p, shape):
        def index(h, i, s, pt):
            g = step_of(h, i, s)
            return (pt[g // n_chunks, (g % n_chunks) * pages_per_step + p], 0, 0)
        return pl.BlockSpec((1,) + shape, index)

    k_specs = [page_spec(p, (aw, page)) for p in range(pages_per_step)]
    v_specs = [page_spec(p, (page * n_heads, V_DIM)) for p in range(pages_per_step)]
    grid_spec = pltpu.PrefetchScalarGridSpec(
        num_scalar_prefetch=1,
        grid=(n_heads, nq, n_sub),
        in_specs=[
            pl.BlockSpec((1, 2, LANES, tk), lambda h, i, s, pt: (h, i, 0, 0)),
            pl.BlockSpec((1, nt, 2 * tk, LANES), lambda h, i, s, pt: (h, 0, 0, 0), pipeline_mode=pl.Buffered(1)),
            pl.BlockSpec((1, nt, V_DIM + SUM_ROWS, tk), lambda h, i, s, pt: (h, 0, 0, 0),
                         pipeline_mode=pl.Buffered(1)),
            vec, vec, vec, vec,
            pl.BlockSpec((V_DIM, 1), lambda h, i, s, pt: (0, 0)),
            pl.BlockSpec((1, V_DIM), lambda h, i, s, pt: (0, 0)),
            tok, tok, tok,
        ] + k_specs + v_specs,
        out_specs=[
            pl.BlockSpec((tq, V_DIM), lambda h, i, s, pt: (i, h)),
            tok,
        ],
        scratch_shapes=[
            pltpu.VMEM((2, 1, tq), F32),
            pltpu.VMEM((2, V_DIM + SUM_ROWS, tq), F32),
            pltpu.VMEM((LANES, tq), BF16),
            pltpu.VMEM((2 * tk, tq), F32),
            pltpu.VMEM((2 * tk, tq), F32),
            pltpu.SMEM((1,), jnp.int32),
            pltpu.VMEM((aw, pages_per_step * page), BF16),
            pltpu.VMEM((pages_per_step * page, aw), BF16),
            pltpu.VMEM((n_rows, 1), F32),
            pltpu.VMEM((n_rows, 1), F32),
            pltpu.VMEM((n_rows, aw), F32),
        ],
    )
    return pl.pallas_call(
        kern,
        grid_spec=grid_spec,
        out_shape=(jax.ShapeDtypeStruct((t, n_heads * V_DIM), BF16),
                   jax.ShapeDtypeStruct((n_b * n_t, aw), F32)),
        name="attention",
        compiler_params=_params(("arbitrary", "arbitrary", "arbitrary")),
    )(page_table, qt, kst, vt, lq1, lk1, lq2, lk2, subln_g.reshape(-1, 1), subln_g.reshape(1, -1),
      q_s, k_new, v_new, *([cache_kt] * pages_per_step), *([cache_v] * pages_per_step))


def _cross_kernel(o_ref, c_ref, x_ref, wout_ref, gpost_ref, gpre_ref, wmq_ref, mk_ref, mv_ref, wmo_ref,
                  gcpost_ref, y_ref, *, mem_heads, row_splits):
    tm, d = x_ref.shape
    hd = d // mem_heads
    n_lt = hd // LANES
    n_mem = mk_ref.shape[1] // (mem_heads * n_lt)

    def head_rows(ref, h):
        return jnp.concatenate(
            [ref[0, pl.ds(lt * mem_heads + h, n_mem, stride=mem_heads * n_lt), :] for lt in range(n_lt)],
            axis=-1).astype(BF16)

    mks = [head_rows(mk_ref, h) for h in range(mem_heads)]
    mvs = [head_rows(mv_ref, h) for h in range(mem_heads)]
    rs = tm // row_splits
    groups = [slice(r * rs, (r + 1) * rs) for r in range(row_splits)]
    scale = hd ** -0.5
    mix = [jnp.dot(jnp.concatenate([o_ref[g, :].astype(BF16), c_ref[g, :].astype(BF16)], axis=-1),
                   wout_ref[...], preferred_element_type=F32) for g in groups]
    x1 = [x_ref[g, :] + _rms(m, gpost_ref[...]) for g, m in zip(groups, mix)]
    q = [(jnp.dot(_rms(x, gpre_ref[...]).astype(BF16), wmq_ref[...], preferred_element_type=F32)
          * scale).astype(BF16) for x in x1]
    outs = [[] for _ in groups]
    for h in range(mem_heads):
        sl = slice(h * hd, (h + 1) * hd)
        s = [lax.dot_general(qr[:, sl], mks[h], (((1,), (1,)), ((), ())), preferred_element_type=F32)
             for qr in q]
        for r, sr in enumerate(s):
            p = jnp.exp(sr - jnp.max(sr, axis=-1, keepdims=True))
            p = p / jnp.sum(p, axis=-1, keepdims=True)
            outs[r].append(jnp.dot(p.astype(BF16), mvs[h], preferred_element_type=F32).astype(BF16))
    att = [jnp.dot(jnp.concatenate(o, axis=-1), wmo_ref[...], preferred_element_type=F32) for o in outs]
    for g, x, a in zip(groups, x1, att):
        y_ref[g, :] = x + _rms(a, gcpost_ref[...])


def _cross(o, c, x, w_out, g_post, g_pre, w_mq, mem_k, mem_v, w_mo, g_cpost, *, tm, mem_heads):
    t, d = x.shape
    aw = o.shape[1]
    cwid = c.shape[1]
    mem_rows, mem_hd = mem_k.shape[1:]
    row = lambda i: (i, 0)
    return pl.pallas_call(
        functools.partial(_cross_kernel, mem_heads=mem_heads, row_splits=2),
        grid=(t // tm,),
        in_specs=[
            pl.BlockSpec((tm, aw), row),
            pl.BlockSpec((tm, cwid), row),
            pl.BlockSpec((tm, d), row),
            _const_spec(w_out.shape),
            _const_spec((1, d)),
            _const_spec((1, d)),
            _const_spec(w_mq.shape),
            _const_spec((1, mem_rows, mem_hd)),
            _const_spec((1, mem_rows, mem_hd)),
            _const_spec(w_mo.shape),
            _const_spec((1, d)),
        ],
        out_specs=pl.BlockSpec((tm, d), row),
        out_shape=jax.ShapeDtypeStruct((t, d), F32),
        name="cross_prompt",
        compiler_params=_params(("arbitrary",)),
    )(o, c, x, w_out, g_post, g_pre, w_mq, mem_k, mem_v, w_mo, g_cpost)


def _cross_sample_kernel(o_ref, c_ref, x_ref, wout_ref, gpost_ref, gpre_ref, wmq_ref, mk_ref, mv_ref, wmo_ref,
                         gcpost_ref, y_ref, x1_scr, q_scr, att_scr, *, mem_heads, n_t):
    b = pl.program_id(0)
    d = x_ref.shape[-1]
    hd = d // mem_heads
    n_lt = hd // LANES
    n_mem = mk_ref.shape[1] // (mem_heads * n_lt)

    @pl.when(b == 0)
    def _():
        mix_in = jnp.concatenate([o_ref[...].astype(BF16), c_ref[...].astype(BF16)], axis=-1)
        mix = jnp.dot(mix_in, wout_ref[...], preferred_element_type=F32)
        x1 = x_ref[...] + _rms(mix, gpost_ref[...])
        x1_scr[...] = x1
        hq = _rms(x1, gpre_ref[...]).astype(BF16)
        q_scr[...] = jnp.dot(hq, wmq_ref[...], preferred_element_type=F32) * (hd ** -0.5)

    def head_rows(ref, h):
        return jnp.concatenate(
            [ref[0, pl.ds(lt * mem_heads + h, n_mem, stride=mem_heads * n_lt), :] for lt in range(n_lt)],
            axis=-1).astype(BF16)

    rows = pl.ds(pl.multiple_of(b * n_t, n_t), n_t)
    q = q_scr[rows, :].astype(BF16)
    for h in range(mem_heads):
        sl = slice(h * hd, (h + 1) * hd)
        s = lax.dot_general(q[:, sl], head_rows(mk_ref, h), (((1,), (1,)), ((), ())),
                            preferred_element_type=F32)
        p = jnp.exp(s - jnp.max(s, axis=-1, keepdims=True))
        p = p / jnp.sum(p, axis=-1, keepdims=True)
        att_scr[rows, sl] = jnp.dot(p.astype(BF16), head_rows(mv_ref, h), preferred_element_type=F32)

    @pl.when(b == pl.num_programs(0) - 1)
    def _():
        att = jnp.dot(att_scr[...].astype(BF16), wmo_ref[...], preferred_element_type=F32)
        y_ref[...] = x1_scr[...] + _rms(att, gcpost_ref[...])


def _cross_sample(o, c, x, w_out, g_post, g_pre, w_mq, mem_k, mem_v, w_mo, g_cpost, *, n_t, mem_heads):
    t, d = x.shape
    n_b, mem_rows, mem_hd = mem_k.shape
    assert n_t == SUBLANES and t == n_b * n_t
    mem_map = lambda b: (b, 0, 0)
    return pl.pallas_call(
        functools.partial(_cross_sample_kernel, mem_heads=mem_heads, n_t=n_t),
        grid=(n_b,),
        in_specs=[
            _const_spec(o.shape),
            _const_spec(c.shape),
            _const_spec(x.shape),
            _const_spec(w_out.shape),
            _const_spec((1, d)),
            _const_spec((1, d)),
            _const_spec(w_mq.shape),
            pl.BlockSpec((1, mem_rows, mem_hd), mem_map),
            pl.BlockSpec((1, mem_rows, mem_hd), mem_map),
            _const_spec(w_mo.shape),
            _const_spec((1, d)),
        ],
        out_specs=pl.BlockSpec((t, d), lambda b: (0, 0)),
        out_shape=jax.ShapeDtypeStruct((t, d), F32),
        scratch_shapes=[pltpu.VMEM((t, d), F32), pltpu.VMEM((t, d), F32), pltpu.VMEM((t, d), F32)],
        name="cross_sample",
        compiler_params=_params(("arbitrary",)),
    )(o, c, x, w_out, g_post, g_pre, w_mq, mem_k, mem_v, w_mo, g_cpost)


def _mlp_kernel(x_ref, gpre_ref, w1_ref, w2_ref, gpost_ref, y_ref):
    x = x_ref[...]
    h = _rms(x, gpre_ref[...]).astype(BF16)
    hf = jnp.dot(h, w1_ref[...], preferred_element_type=F32)
    r = jnp.maximum(hf, 0.0)
    f = jnp.dot((r * r).astype(BF16), w2_ref[...], preferred_element_type=F32)
    y_ref[...] = x + _rms(f, gpost_ref[...])


def _mlp(x, g_pre, w1, w2, g_post, *, tm):
    t, d = x.shape
    row = lambda i: (i, 0)
    return pl.pallas_call(
        _mlp_kernel,
        grid=(t // tm,),
        in_specs=[pl.BlockSpec((tm, d), row), _const_spec((1, d)), _const_spec(w1.shape),
                  _const_spec(w2.shape), _const_spec((1, d))],
        out_specs=pl.BlockSpec((tm, d), row),
        out_shape=jax.ShapeDtypeStruct((t, d), F32),
        name="mlp_tile%d" % tm,
        compiler_params=_params(("arbitrary",)),
    )(x, g_pre, w1, w2, g_post)


def _rope_tables(pos):
    half = HEAD_DIM // 2
    inv = ROPE_THETA ** (-jnp.arange(0, half, dtype=F32) * 2.0 / HEAD_DIM)
    ang = pos.astype(F32)[:, None] * inv[None, :]
    cos = jnp.cos(ang)
    sin = jnp.sin(ang)
    reps = LANES // HEAD_DIM
    return (jnp.tile(jnp.concatenate([cos, cos], axis=-1), (1, reps)),
            jnp.tile(jnp.concatenate([-sin, sin], axis=-1), (1, reps)))


def _conv_toeplitz(conv_w, n_t):
    n_r = CONV_K - 1 + n_t
    r = jnp.arange(n_r)[:, None]
    t = jnp.arange(n_t)[None, :]
    j = r - t
    ok = (j >= 0) & (j < CONV_K)
    return jnp.where(ok[:, :, None], conv_w[jnp.clip(j, 0, CONV_K - 1)], 0.0)


def _layer(i, depth_inputs, x_prompt, x_sample, mem_prompt, cache_k, cache_v, cache_conv, cache_mem_k,
           cache_mem_v, page_table, *, tile, pages_per_step):
    (w_in, lq1, lk1, lq2, lk2, subln_g, conv_w, conv_b, ln_g, ln_b, w_out, mix_pre_g, mix_post_g, mem_norm_g,
     w_mq, w_mk, w_mv, w_mo, cross_pre_g, cross_post_g, w_ff1, w_ff2, mlp_pre_g, mlp_post_g) = depth_inputs
    lam_init = 0.8 - 0.6 * math.exp(-0.3 * i)
    _, seq, d = x_prompt.shape
    n_b, n_t, _ = x_sample.shape
    n_heads, v_dim = cache_v.shape[-2:]
    aw = n_heads * v_dim
    cwid = cache_conv.shape[-1]
    mem_heads = cache_mem_k.shape[-2]
    n_mem = mem_prompt.shape[1]
    past = page_table.shape[1] * cache_k.shape[1]

    row2 = lambda a: a.reshape(1, -1)
    w_in_b, w_out_b, w_mq_b, w_mk_b, w_mv_b, w_mo_b, w_ff1_b, w_ff2_b = (
        w.astype(BF16) for w in (w_in, w_out, w_mq, w_mk, w_mv, w_mo, w_ff1, w_ff2))
    lams = (row2(lq1), row2(lk1), row2(lq2), row2(lk2))

    mk_p, mv_p = _memory_kv(mem_prompt[0], row2(mem_norm_g), w_mk_b, w_mv_b, mem_heads=mem_heads)
    cos_p, sin_p = _rope_tables(jnp.arange(seq, dtype=jnp.int32))
    k_p, v_p, qt, kst, vt, c_p, tail_p = _inproj_prompt(
        x_prompt[0], row2(mix_pre_g), w_in_b, cos_p, sin_p, conv_w, row2(conv_b), row2(ln_g), row2(ln_b),
        tm=tile, aw=aw, cwid=cwid, n_heads=n_heads)

    pos_s = jnp.tile(past + jnp.arange(n_t, dtype=jnp.int32), n_b)
    cos_s, sin_s = _rope_tables(pos_s)
    xs = x_sample.reshape(n_b * n_t, d)
    k_s, v_s, q_s, glu_s, c_s = _inproj_sample(
        xs, row2(mix_pre_g), w_in_b, cos_s, sin_s, cache_conv, _conv_toeplitz(conv_w, n_t), row2(conv_b),
        row2(ln_g), row2(ln_b), aw=aw, cwid=cwid, n_heads=n_heads, n_b=n_b, n_t=n_t)
    n_pool, page = cache_k.shape[:2]
    cache_kt = jnp.transpose(cache_k, (0, 2, 3, 4, 1)).reshape(n_pool, aw, page)
    cache_vr = cache_v.reshape(n_pool, page * n_heads, v_dim)
    o_p, o_s = _attention(page_table, qt, kst, vt, *lams, subln_g, q_s, k_s, v_s, cache_kt, cache_vr,
                          n_b=n_b, n_t=n_t, pages_per_step=pages_per_step, lam_init=lam_init)
    x2_p = _cross(o_p, c_p, x_prompt[0], w_out_b, row2(mix_post_g), row2(cross_pre_g), w_mq_b,
                  mk_p[None], mv_p[None], w_mo_b, row2(cross_post_g), tm=tile, mem_heads=mem_heads)
    y_p = _mlp(x2_p, row2(mlp_pre_g), w_ff1_b, w_ff2_b, row2(mlp_post_g), tm=tile)
    n_lt = d // mem_heads // LANES

    def mem_rows(a):
        a = a.reshape(a.shape[0], n_mem, mem_heads, n_lt, LANES)
        return jnp.transpose(a, (0, 1, 3, 2, 4)).reshape(a.shape[0], n_mem * n_lt * mem_heads, LANES)

    x2_s = _cross_sample(o_s, c_s.reshape(n_b * n_t, cwid), xs, w_out_b, row2(mix_post_g), row2(cross_pre_g),
                         w_mq_b, mem_rows(cache_mem_k), mem_rows(cache_mem_v), w_mo_b,
                         row2(cross_post_g), n_t=n_t, mem_heads=mem_heads)

    def mem_out(a):
        a = a.reshape(n_mem, n_lt, mem_heads, LANES)
        return jnp.transpose(a, (0, 2, 1, 3)).reshape(1, n_mem, mem_heads, n_lt * LANES)
    y_s = _mlp(x2_s, row2(mlp_pre_g), w_ff1_b, w_ff2_b, row2(mlp_post_g), tm=n_b * n_t)

    hd2 = (n_heads, 2, HEAD_DIM)
    outs = dict(
        y_p=y_p[None], y_s=y_s.reshape(n_b, n_t, d),
        k_p=jnp.transpose(k_p.reshape(*hd2, seq), (3, 0, 1, 2))[None],
        v_p=v_p.reshape(1, seq, n_heads, v_dim),
        conv_p=tail_p[CONV_HALO - (CONV_K - 1):][None],
        mk_p=mem_out(mk_p), mv_p=mem_out(mv_p),
        k_s=k_s.reshape(n_b, n_t, *hd2), v_s=v_s.reshape(n_b, n_t, n_heads, v_dim),
        conv_s=jnp.concatenate([cache_conv[:, n_t:], glu_s], axis=1),
    )
    return outs


def kernel(x_prompt, x_sample, mem_prompt, cache_k, cache_v, cache_conv, cache_mem_k, cache_mem_v, page_table, w_in, lambda_q1, lambda_k1, lambda_q2, lambda_k2, subln_g, conv_w, conv_b, conv_ln_g, conv_ln_b, w_out, mix_pre_g, mix_post_g, mem_norm_g, w_mq, w_mk, w_mv, w_mo, cross_pre_g, cross_post_g, w_ff1, w_ff2, mlp_pre_g, mlp_post_g):
    depth = w_in.shape[0]
    stacked = (w_in, lambda_q1, lambda_k1, lambda_q2, lambda_k2, subln_g, conv_w, conv_b, conv_ln_g, conv_ln_b,
               w_out, mix_pre_g, mix_post_g, mem_norm_g, w_mq, w_mk, w_mv, w_mo, cross_pre_g, cross_post_g,
               w_ff1, w_ff2, mlp_pre_g, mlp_post_g)
    tile = min(512, x_prompt.shape[1])
    pages_per_step = min(16, page_table.shape[1])
    y_p, y_s = x_prompt, x_sample
    per_layer = []
    for i in range(depth):
        o = _layer(i, tuple(a[i] for a in stacked), y_p, y_s, mem_prompt, cache_k[i], cache_v[i], cache_conv[i],
                   cache_mem_k[i], cache_mem_v[i], page_table, tile=tile, pages_per_step=pages_per_step)
        y_p, y_s = o["y_p"], o["y_s"]
        per_layer.append(o)
    stack = lambda name: jnp.stack([o[name] for o in per_layer], axis=0)
    return (y_p, y_s, stack("k_p"), stack("v_p"), stack("conv_p"), stack("mk_p"), stack("mv_p"),
            stack("k_s"), stack("v_s"), stack("conv_s"))
```

```python
import functools
import math

import jax
import numpy as np
import jax.numpy as jnp
from jax import lax
from jax.experimental import pallas as pl
from jax.experimental.pallas import tpu as pltpu

F32 = jnp.float32
BF16 = jnp.bfloat16

RMS_EPS = 1e-6
LN_EPS = 1e-5
ROPE_THETA = 10000.0
LANES = 128
SUBLANES = 8
HEAD_DIM = 64
V_DIM = 2 * HEAD_DIM
CONV_K = 31
CONV_HALO = 32
SUM_ROWS = 16
VMEM_LIMIT = 56 * 1024 * 1024


def _rms(x, g):
    return x * lax.rsqrt(jnp.mean(x * x, axis=-1, keepdims=True) + RMS_EPS) * g


def _lam(lq1, lk1, lq2, lk2, lam_init):
    a = jnp.sum(lq1[...] * lk1[...], axis=-1, keepdims=True)
    b = jnp.sum(lq2[...] * lk2[...], axis=-1, keepdims=True)
    return jnp.exp(a) - jnp.exp(b) + lam_init


def _rope_lanes(x, cos, sin_signed, first_half):
    rot = jnp.where(first_half, pltpu.roll(x, LANES - HEAD_DIM // 2, 1), pltpu.roll(x, HEAD_DIM // 2, 1))
    return x * cos + rot * sin_signed


def _const_spec(shape):
    nd = len(shape)
    return pl.BlockSpec(shape, lambda *_: (0,) * nd, pipeline_mode=pl.Buffered(1))


def _params(sem):
    return pltpu.CompilerParams(dimension_semantics=sem, vmem_limit_bytes=VMEM_LIMIT)


def _memory_kv_kernel(mem_ref, g_ref, wk_ref, wv_ref, mk_ref, mv_ref, *, mem_heads):
    m = _rms(mem_ref[...], g_ref[...]).astype(BF16)
    n, d = mem_ref.shape
    hd = d // mem_heads
    mk = jnp.dot(m, wk_ref[...], preferred_element_type=F32)
    mv = jnp.dot(m, wv_ref[...], preferred_element_type=F32)
    n_lt = hd // LANES
    for h in range(mem_heads):
        for lt in range(n_lt):
            rows = pl.ds(lt * mem_heads + h, n, stride=mem_heads * n_lt)
            col = h * hd + lt * LANES
            mk_ref[rows, :] = mk[:, col:col + LANES]
            mv_ref[rows, :] = mv[:, col:col + LANES]


def _memory_kv(mem, g, wk, wv, *, mem_heads):
    n, d = mem.shape
    out = jax.ShapeDtypeStruct((n * d // LANES, LANES), F32)
    return pl.pallas_call(
        functools.partial(_memory_kv_kernel, mem_heads=mem_heads),
        out_shape=(out, out),
        name="memory_kv",
        compiler_params=pltpu.CompilerParams(vmem_limit_bytes=VMEM_LIMIT),
    )(mem, g, wk, wv)


def _inproj_prompt_kernel(x_ref, g_ref, w_ref, cos_ref, sin_ref, cw_ref, cb_ref, lng_ref, lnb_ref,
                          kout_ref, vout_ref, qt_ref, kst_ref, vt_ref, c_ref, tail_ref, gpad_ref, gsh_ref,
                          *, tm, aw, cwid, n_heads, scale, row_chunk):
    i = pl.program_id(0)

    @pl.when(i == 0)
    def _():
        gpad_ref[0:CONV_HALO, :] = jnp.zeros((CONV_HALO, cwid), F32)

    h = _rms(x_ref[...], g_ref[...]).astype(BF16)
    cos = cos_ref[...]
    sin = sin_ref[...]
    lane = lax.broadcasted_iota(jnp.int32, (tm, LANES), 1)
    first_half = (lane % HEAD_DIM) < (HEAD_DIM // 2)
    lo = lane < HEAD_DIM
    sum_rows = (lax.broadcasted_iota(jnp.int32, (SUM_ROWS, tm), 0) == 0).astype(BF16)

    q = jnp.dot(h, w_ref[:, 0:aw], preferred_element_type=F32)
    k = jnp.dot(h, w_ref[:, aw:2 * aw], preferred_element_type=F32)
    v = jnp.dot(h, w_ref[:, 2 * aw:3 * aw], preferred_element_type=F32)
    for hd in range(n_heads):
        sl = slice(hd * LANES, (hd + 1) * LANES)
        qh = _rope_lanes(q[:, sl], cos, sin, first_half) * scale
        kh = _rope_lanes(k[:, sl], cos, sin, first_half)
        kout_ref[sl, :] = kh.T
        vout_ref[pl.ds(hd, tm, stride=n_heads), :] = v[:, sl]
        qt_ref[hd, 0] = qh.T.astype(BF16)
        kst_ref[hd, 0, 0:tm, :] = jnp.where(lo, kh, 0.0).astype(BF16)
        kst_ref[hd, 0, tm:2 * tm, :] = jnp.where(lo, 0.0, kh).astype(BF16)
        vt_ref[hd, 0, 0:V_DIM, :] = v[:, sl].T.astype(BF16)
        vt_ref[hd, 0, V_DIM:V_DIM + SUM_ROWS, :] = sum_rows

    u1 = jnp.dot(h, w_ref[:, 3 * aw:3 * aw + cwid], preferred_element_type=F32)
    u2 = jnp.dot(h, w_ref[:, 3 * aw + cwid:3 * aw + 2 * cwid], preferred_element_type=F32)
    gpad_ref[CONV_HALO:CONV_HALO + tm, :] = u1 * jax.nn.sigmoid(u2)

    base = CONV_HALO - (CONV_K - 1)
    cb = cb_ref[...]
    lng = lng_ref[...]
    lnb = lnb_ref[...]
    n_sh = gsh_ref.shape[1]
    for b in range(1, SUBLANES):
        gsh_ref[b - 1] = gpad_ref[pl.ds(b, n_sh), :]
    for r0 in range(0, tm, row_chunk):
        acc = jnp.zeros((row_chunk, cwid), F32)
        for j in range(CONV_K):
            a, b = divmod(base + j, SUBLANES)
            rows = pl.ds(r0 + a * SUBLANES, row_chunk)
            tap = gpad_ref[rows, :] if b == 0 else gsh_ref[b - 1, rows, :]
            acc = acc + cw_ref[j:j + 1, :] * tap
        acc = acc + cb
        mu = jnp.mean(acc, axis=-1, keepdims=True)
        d = acc - mu
        var = jnp.mean(d * d, axis=-1, keepdims=True)
        y = d * lax.rsqrt(var + LN_EPS) * lng + lnb
        c_ref[r0:r0 + row_chunk, :] = (y * jax.nn.sigmoid(y)).astype(c_ref.dtype)

    tail_ref[...] = gpad_ref[tm:tm + CONV_HALO, :]
    gpad_ref[0:CONV_HALO, :] = gpad_ref[tm:tm + CONV_HALO, :]


def _inproj_prompt(x, g, w_in, cos, sin, conv_w, conv_b, ln_g, ln_b, *, tm, aw, cwid, n_heads):
    t, d = x.shape
    nt = t // tm
    kern = functools.partial(_inproj_prompt_kernel, tm=tm, aw=aw, cwid=cwid, n_heads=n_heads,
                             scale=HEAD_DIM ** -0.5 * math.log2(math.e), row_chunk=min(tm, 32))
    row = lambda i: (i, 0)
    return pl.pallas_call(
        kern,
        grid=(nt,),
        in_specs=[
            pl.BlockSpec((tm, d), row),
            _const_spec((1, d)),
            _const_spec(w_in.shape),
            pl.BlockSpec((tm, LANES), row),
            pl.BlockSpec((tm, LANES), row),
            _const_spec(conv_w.shape),
            _const_spec((1, cwid)),
            _const_spec((1, cwid)),
            _const_spec((1, cwid)),
        ],
        out_specs=[
            pl.BlockSpec((aw, tm), lambda i: (0, i)),
            pl.BlockSpec((tm * n_heads, V_DIM), row),
            pl.BlockSpec((n_heads, 1, LANES, tm), lambda i: (0, i, 0, 0)),
            pl.BlockSpec((n_heads, 1, 2 * tm, LANES), lambda i: (0, i, 0, 0)),
            pl.BlockSpec((n_heads, 1, V_DIM + SUM_ROWS, tm), lambda i: (0, i, 0, 0)),
            pl.BlockSpec((tm, cwid), row),
            pl.BlockSpec((CONV_HALO, cwid), lambda i: (0, 0)),
        ],
        out_shape=(
            jax.ShapeDtypeStruct((aw, t), F32),
            jax.ShapeDtypeStruct((t * n_heads, V_DIM), F32),
            jax.ShapeDtypeStruct((n_heads, nt, LANES, tm), BF16),
            jax.ShapeDtypeStruct((n_heads, nt, 2 * tm, LANES), BF16),
            jax.ShapeDtypeStruct((n_heads, nt, V_DIM + SUM_ROWS, tm), BF16),
            jax.ShapeDtypeStruct((t, cwid), BF16),
            jax.ShapeDtypeStruct((CONV_HALO, cwid), F32),
        ),
        scratch_shapes=[pltpu.VMEM((tm + CONV_HALO, cwid), F32),
                        pltpu.VMEM((SUBLANES - 1, tm + CONV_HALO - SUBLANES, cwid), F32)],
        name="inproj_prompt",
        compiler_params=_params(("arbitrary",)),
    )(x, g, w_in, cos, sin, conv_w, conv_b, ln_g, ln_b)


def _inproj_sample_kernel(x_ref, g_ref, w_ref, cos_ref, sin_ref, cc_ref, toep_ref, cb_ref, lng_ref, lnb_ref,
                          kout_ref, vout_ref, q_ref, glu_ref, c_ref, *, aw, cwid, n_heads, scale, n_b, n_t):
    rows = n_b * n_t
    h = _rms(x_ref[...], g_ref[...]).astype(BF16)
    cos = cos_ref[...]
    sin = sin_ref[...]
    lane = lax.broadcasted_iota(jnp.int32, (rows, LANES), 1)
    first_half = (lane % HEAD_DIM) < (HEAD_DIM // 2)
    q = jnp.dot(h, w_ref[:, 0:aw], preferred_element_type=F32)
    k = jnp.dot(h, w_ref[:, aw:2 * aw], preferred_element_type=F32)
    vout_ref[...] = jnp.dot(h, w_ref[:, 2 * aw:3 * aw], preferred_element_type=F32)
    for hd in range(n_heads):
        sl = slice(hd * LANES, (hd + 1) * LANES)
        q_ref[:, sl] = _rope_lanes(q[:, sl], cos, sin, first_half) * scale
        kout_ref[:, sl] = _rope_lanes(k[:, sl], cos, sin, first_half)
    u1 = jnp.dot(h, w_ref[:, 3 * aw:3 * aw + cwid], preferred_element_type=F32)
    u2 = jnp.dot(h, w_ref[:, 3 * aw + cwid:3 * aw + 2 * cwid], preferred_element_type=F32)
    glu_ref[...] = (u1 * jax.nn.sigmoid(u2)).reshape(n_b, n_t, cwid)

    acc = jnp.zeros((n_b, n_t, cwid), F32)
    for r in range(CONV_K - 1):
        acc = acc + cc_ref[:, r:r + 1, :] * toep_ref[r]
    for r in range(n_t):
        acc = acc + glu_ref[:, r:r + 1, :] * toep_ref[CONV_K - 1 + r]
    acc = acc + cb_ref[...]
    mu = jnp.mean(acc, axis=-1, keepdims=True)
    d = acc - mu
    var = jnp.mean(d * d, axis=-1, keepdims=True)
    y = d * lax.rsqrt(var + LN_EPS) * lng_ref[...] + lnb_ref[...]
    c_ref[...] = y * jax.nn.sigmoid(y)


def _inproj_sample(x, g, w_in, cos, sin, cache_conv, toep, conv_b, ln_g, ln_b, *, aw, cwid, n_heads, n_b, n_t):
    rows = n_b * n_t
    kern = functools.partial(_inproj_sample_kernel, aw=aw, cwid=cwid, n_heads=n_heads,
                             scale=HEAD_DIM ** -0.5, n_b=n_b, n_t=n_t)
    return pl.pallas_call(
        kern,
        out_shape=(
            jax.ShapeDtypeStruct((rows, aw), F32),
            jax.ShapeDtypeStruct((rows, aw), F32),
            jax.ShapeDtypeStruct((rows, aw), F32),
            jax.ShapeDtypeStruct((n_b, n_t, cwid), F32),
            jax.ShapeDtypeStruct((n_b, n_t, cwid), F32),
        ),
        name="inproj_sample",
        compiler_params=pltpu.CompilerParams(vmem_limit_bytes=VMEM_LIMIT),
    )(x, g, w_in, cos, sin, cache_conv, toep, conv_b, ln_g, ln_b)


def _masked_queries(q, n_heads, n_t):
    n_rows = n_heads * 2 * n_t
    aw = q.shape[-1]
    qrep = jnp.concatenate([q] * (n_heads * 2), axis=0)
    sel = (lax.broadcasted_iota(jnp.int32, (n_rows, aw), 0) // n_t
           == lax.broadcasted_iota(jnp.int32, (n_rows, aw), 1) // HEAD_DIM)
    return jnp.where(sel, qrep, 0.0).astype(BF16)


def _sample_init_kernel(q_ref, kn_ref, vn_ref, m_ref, l_ref, acc_ref, *, n_heads, n_t, group):
    aw = q_ref.shape[-1]
    n_rows = n_heads * 2 * n_t
    pad = jnp.zeros((LANES - n_t, aw), F32)
    col = lax.broadcasted_iota(jnp.int32, (n_rows, LANES), 1)
    qt = lax.broadcasted_iota(jnp.int32, (n_rows, LANES), 0) % n_t
    for b in range(group):
        rows = slice(b * n_t, (b + 1) * n_t)
        wq = _masked_queries(q_ref[rows, :], n_heads, n_t)
        kn = jnp.concatenate([kn_ref[rows, :], pad], axis=0).astype(BF16)
        vn = jnp.concatenate([vn_ref[rows, :], pad], axis=0).astype(BF16)
        sc = lax.dot_general(wq, kn, (((1,), (1,)), ((), ())), preferred_element_type=F32)
        sc = jnp.where(col <= qt, sc, -jnp.inf)
        m = jnp.max(sc, axis=-1, keepdims=True)
        p = jnp.exp(sc - m)
        m_ref[b] = m
        l_ref[b] = jnp.sum(p, axis=-1, keepdims=True)
        acc_ref[b] = jnp.dot(p.astype(BF16), vn, preferred_element_type=F32)


def _sample_init(q, k_new, v_new, *, n_b, n_t, n_heads):
    aw = q.shape[-1]
    n_rows = n_heads * 2 * n_t
    group = math.gcd(n_b, SUBLANES)
    tok = pl.BlockSpec((group * n_t, aw), lambda g: (g, 0))
    stat = pl.BlockSpec((group, n_rows, 1), lambda g: (g, 0, 0))
    return pl.pallas_call(
        functools.partial(_sample_init_kernel, n_heads=n_heads, n_t=n_t, group=group),
        grid=(n_b // group,),
        in_specs=[tok, tok, tok],
        out_specs=[stat, stat, pl.BlockSpec((group, n_rows, aw), lambda g: (g, 0, 0))],
        out_shape=(jax.ShapeDtypeStruct((n_b, n_rows, 1), F32), jax.ShapeDtypeStruct((n_b, n_rows, 1), F32),
                   jax.ShapeDtypeStruct((n_b, n_rows, aw), F32)),
        name="sample_init",
        compiler_params=_params(("arbitrary",)),
    )(q, k_new, v_new)


def _attn_kernel(pt_ref, th_ref, ti_ref, tt_ref, tl_ref, te_ref, qt_ref, kst_ref, vt_ref, lq1, lk1, lq2, lk2, sgcol_ref,
                 sgrow_ref, qs_ref, m0_ref, l0_ref, a0_ref, *rest,
                 n_pages, n_units, units_per_entry, tk, page, n_heads, n_t, lam_init):
    k_pages = rest[:n_pages]
    v_pages = rest[n_pages:2 * n_pages]
    op_ref, os_ref = rest[2 * n_pages:2 * n_pages + 2]
    (m_ref, acc_ref, q_scr, sa_ref, sb_ref, pend_ref,
     kbuf, vbuf, ms_ref, ls_ref, accs_ref) = rest[2 * n_pages + 2:]
    g = pl.program_id(0)
    i = ti_ref[g]
    t = tt_ref[g]
    tq = 2 * tk
    lam = _lam(lq1, lk1, lq2, lk2, lam_init)

    unit = jnp.minimum(g, n_units - 1)
    active = g < n_units
    first = unit % units_per_entry == 0

    def sample_scores():
        for p in range(n_pages):
            kbuf[:, p * page:(p + 1) * page] = k_pages[p][0].astype(BF16)
            for hd in range(n_heads):
                vbuf[p * page:(p + 1) * page, hd * V_DIM:(hd + 1) * V_DIM] = (
                    v_pages[p][0, pl.ds(hd, page, stride=n_heads), :].astype(BF16))
        wq = _masked_queries(qs_ref[...], n_heads, n_t)
        sc = jnp.dot(wq, kbuf[...], preferred_element_type=F32)
        return jnp.where(active, sc, -jnp.inf)

    def sample_softmax(sc):
        m_prev = jnp.where(first, m0_ref[0], ms_ref[...])
        l_prev = jnp.where(first, l0_ref[0], ls_ref[...])
        m_new = jnp.maximum(m_prev, jnp.max(sc, axis=-1, keepdims=True))
        p = jnp.exp(sc - m_new)
        alpha = jnp.exp(m_prev - m_new)
        l_new = alpha * l_prev + jnp.sum(p, axis=-1, keepdims=True)
        ms_ref[...] = m_new
        ls_ref[...] = l_new
        return p.astype(BF16), alpha, l_new

    def sample_values(p, alpha, l_new):
        acc = (alpha * jnp.where(first, a0_ref[0], accs_ref[...])
               + jnp.dot(p, vbuf[...], preferred_element_type=F32))
        accs_ref[...] = acc
        o = acc / l_new
        for hd in range(n_heads):
            r0 = hd * 2 * n_t
            sl = slice(hd * V_DIM, (hd + 1) * V_DIM)
            a = o[r0:r0 + n_t, sl] - lam * o[r0 + n_t:r0 + 2 * n_t, sl]
            os_ref[:, sl] = _rms(a, sgrow_ref[...]) * (1.0 - lam_init)

    def scores(j, s_ref, key_offset=None):
        st = jnp.dot(kst_ref[0, j], q_scr[...], preferred_element_type=F32)
        if key_offset is None:
            s_ref[...] = st
        else:
            keep = (lax.broadcasted_iota(jnp.int32, (tk, tq), 0) + key_offset
                    <= lax.broadcasted_iota(jnp.int32, (tk, tq), 1))
            for c in range(2):
                s_ref[c * tk:(c + 1) * tk, :] = jnp.where(keep, st[c * tk:(c + 1) * tk, :], -jnp.inf)

    def absorb(j, s_ref):
        vt = vt_ref[0, j]
        for c in range(2):
            sc = s_ref[c * tk:(c + 1) * tk, :]
            m_prev = m_ref[c]
            m_new = jnp.maximum(m_prev, jnp.max(sc, axis=0, keepdims=True))
            p = jnp.exp2(sc - m_new).astype(BF16)
            alpha = jnp.exp2(m_prev - m_new)
            acc_ref[c] = alpha * acc_ref[c] + jnp.dot(vt, p, preferred_element_type=F32)
            m_ref[c] = m_new

    @pl.when(t < 0)
    def _():
        q_scr[:, 0:tk] = qt_ref[0, 0]
        q_scr[:, tk:tq] = qt_ref[0, 1]
        m_ref[...] = jnp.full(m_ref.shape, -jnp.inf, F32)
        acc_ref[...] = jnp.zeros(acc_ref.shape, F32)
        sc = sample_scores()
        scores(2 * i, sa_ref, key_offset=0)
        soft = sample_softmax(sc)
        scores(2 * i + 1, sb_ref, key_offset=tk)
        sample_values(*soft)
        absorb(2 * i, sa_ref)
        pend_ref[0] = 2 * i + 1

    @pl.when(t >= 0)
    def _():
        pending = pend_ref[0]
        sc = sample_scores()
        scores(2 * t, sa_ref)
        soft = sample_softmax(sc)
        absorb(pending, sb_ref)
        sample_values(*soft)
        scores(2 * t + 1, sb_ref)
        absorb(2 * t, sa_ref)
        pend_ref[0] = 2 * t + 1

    @pl.when(tl_ref[g] == 1)
    def _():
        absorb(pend_ref[0], sb_ref)
        o1 = acc_ref[0, 0:V_DIM, :] / acc_ref[0, V_DIM:V_DIM + 1, :]
        o2 = acc_ref[1, 0:V_DIM, :] / acc_ref[1, V_DIM:V_DIM + 1, :]
        a = o1 - lam * o2
        y = a * lax.rsqrt(jnp.mean(a * a, axis=0, keepdims=True) + RMS_EPS) * sgcol_ref[...] * (1.0 - lam_init)
        op_ref[...] = y.T.astype(op_ref.dtype)


def _prompt_tasks(n_heads, nq):
    tasks = [(h, i, t, int(t == i - 1)) for h in range(n_heads) for i in range(nq) for t in range(-1, i)]
    return tuple(np.asarray(col, np.int32) for col in zip(*tasks))


def _attention(page_table, qt, kst, vt, lq1, lk1, lq2, lk2, subln_g, q_s, k_new, v_new, cache_kt, cache_v,
               *, n_b, n_t, lam_init):
    n_heads, nt, _, tk = qt.shape
    assert nt % 2 == 0, "a query block spans two key blocks"
    nq = nt // 2
    tq = 2 * tk
    t = nt * tk
    n_pool, aw, page = cache_kt.shape
    n_rows = n_heads * 2 * n_t
    th, ti, tt, tl = _prompt_tasks(n_heads, nq)
    n_tasks = len(th)
    n_pages_entry = page_table.shape[1]
    pages_per_unit = min(p for p in range(1, n_pages_entry + 1)
                         if n_pages_entry % p == 0 and n_b * (n_pages_entry // p) <= n_tasks)
    units_per_entry = n_pages_entry // pages_per_unit
    n_units = n_b * units_per_entry
    m0, l0, a0 = _sample_init(q_s, k_new, v_new, n_b=n_b, n_t=n_t, n_heads=n_heads)
    kern = functools.partial(_attn_kernel, n_pages=pages_per_unit, n_units=n_units,
                             units_per_entry=units_per_entry, tk=tk, page=page, n_heads=n_heads, n_t=n_t,
                             lam_init=lam_init)

    unit_of_step = np.minimum(np.arange(n_tasks), n_units - 1)
    te = (unit_of_step // units_per_entry).astype(np.int32)
    step_pages = page_table.reshape(n_units, pages_per_unit)[unit_of_step].reshape(-1)

    vec = pl.BlockSpec((1, HEAD_DIM), lambda g, *_: (0, 0))
    tok = pl.BlockSpec((n_t, aw), lambda g, pt, th, ti, tt, tl, te: (te[g], 0))
    stat = pl.BlockSpec((1, n_rows, 1), lambda g, pt, th, ti, tt, tl, te: (te[g], 0, 0))

    def page_spec(p, shape):
        return pl.BlockSpec((1,) + shape, lambda g, pt, *_: (pt[g * pages_per_unit + p], 0, 0))

    k_specs = [page_spec(p, (aw, page)) for p in range(pages_per_unit)]
    v_specs = [page_spec(p, (page * n_heads, V_DIM)) for p in range(pages_per_unit)]
    head_map = lambda g, pt, th, *_: (th[g], 0, 0, 0)
    grid_spec = pltpu.PrefetchScalarGridSpec(
        num_scalar_prefetch=6,
        grid=(n_tasks,),
        in_specs=[
            pl.BlockSpec((1, 2, LANES, tk), lambda g, pt, th, ti, *_: (th[g], ti[g], 0, 0)),
            pl.BlockSpec((1, nt, 2 * tk, LANES), head_map, pipeline_mode=pl.Buffered(1)),
            pl.BlockSpec((1, nt, V_DIM + SUM_ROWS, tk), head_map, pipeline_mode=pl.Buffered(1)),
            vec, vec, vec, vec,
            pl.BlockSpec((V_DIM, 1), lambda g, *_: (0, 0)),
            pl.BlockSpec((1, V_DIM), lambda g, *_: (0, 0)),
            tok, stat, stat,
            pl.BlockSpec((1, n_rows, aw), lambda g, pt, th, ti, tt, tl, te: (te[g], 0, 0)),
        ] + k_specs + v_specs,
        out_specs=[
            pl.BlockSpec((tq, V_DIM), lambda g, pt, th, ti, *_: (ti[g], th[g])),
            tok,
        ],
        scratch_shapes=[
            pltpu.VMEM((2, 1, tq), F32),
            pltpu.VMEM((2, V_DIM + SUM_ROWS, tq), F32),
            pltpu.VMEM((LANES, tq), BF16),
            pltpu.VMEM((2 * tk, tq), F32),
            pltpu.VMEM((2 * tk, tq), F32),
            pltpu.SMEM((1,), jnp.int32),
            pltpu.VMEM((aw, pages_per_unit * page), BF16),
            pltpu.VMEM((pages_per_unit * page, aw), BF16),
            pltpu.VMEM((n_rows, 1), F32),
            pltpu.VMEM((n_rows, 1), F32),
            pltpu.VMEM((n_rows, aw), F32),
        ],
    )
    return pl.pallas_call(
        kern,
        grid_spec=grid_spec,
        out_shape=(jax.ShapeDtypeStruct((t, n_heads * V_DIM), BF16),
                   jax.ShapeDtypeStruct((n_b * n_t, aw), F32)),
        name="attention",
        compiler_params=_params(("arbitrary",)),
    )(step_pages, th, ti, tt, tl, te, qt, kst, vt, lq1, lk1, lq2, lk2, subln_g.reshape(-1, 1),
      subln_g.reshape(1, -1), q_s, m0, l0, a0, *([cache_kt] * pages_per_unit), *([cache_v] * pages_per_unit))


def _cross_kernel(o_ref, c_ref, x_ref, wout_ref, gpost_ref, gpre_ref, wmq_ref, mk_ref, mv_ref, wmo_ref,
                  gcpost_ref, y_ref, *, mem_heads, row_splits):
    tm, d = x_ref.shape
    hd = d // mem_heads
    n_lt = hd // LANES
    n_mem = mk_ref.shape[1] // (mem_heads * n_lt)

    def head_rows(ref, h):
        return jnp.concatenate(
            [ref[0, pl.ds(lt * mem_heads + h, n_mem, stride=mem_heads * n_lt), :] for lt in range(n_lt)],
            axis=-1).astype(BF16)

    mks = [head_rows(mk_ref, h) for h in range(mem_heads)]
    mvs = [head_rows(mv_ref, h) for h in range(mem_heads)]
    rs = tm // row_splits
    groups = [slice(r * rs, (r + 1) * rs) for r in range(row_splits)]
    scale = hd ** -0.5
    mix = [jnp.dot(jnp.concatenate([o_ref[g, :].astype(BF16), c_ref[g, :].astype(BF16)], axis=-1),
                   wout_ref[...], preferred_element_type=F32) for g in groups]
    x1 = [x_ref[g, :] + _rms(m, gpost_ref[...]) for g, m in zip(groups, mix)]
    q = [(jnp.dot(_rms(x, gpre_ref[...]).astype(BF16), wmq_ref[...], preferred_element_type=F32)
          * scale).astype(BF16) for x in x1]
    outs = [[] for _ in groups]
    for h in range(mem_heads):
        sl = slice(h * hd, (h + 1) * hd)
        s = [lax.dot_general(qr[:, sl], mks[h], (((1,), (1,)), ((), ())), preferred_element_type=F32)
             for qr in q]
        for r, sr in enumerate(s):
            p = jnp.exp(sr - jnp.max(sr, axis=-1, keepdims=True))
            p = p / jnp.sum(p, axis=-1, keepdims=True)
            outs[r].append(jnp.dot(p.astype(BF16), mvs[h], preferred_element_type=F32).astype(BF16))
    att = [jnp.dot(jnp.concatenate(o, axis=-1), wmo_ref[...], preferred_element_type=F32) for o in outs]
    for g, x, a in zip(groups, x1, att):
        y_ref[g, :] = x + _rms(a, gcpost_ref[...])


def _cross(o, c, x, w_out, g_post, g_pre, w_mq, mem_k, mem_v, w_mo, g_cpost, *, tm, mem_heads):
    t, d = x.shape
    aw = o.shape[1]
    cwid = c.shape[1]
    mem_rows, mem_hd = mem_k.shape[1:]
    row = lambda i: (i, 0)
    return pl.pallas_call(
        functools.partial(_cross_kernel, mem_heads=mem_heads, row_splits=1),
        grid=(t // tm,),
        in_specs=[
            pl.BlockSpec((tm, aw), row),
            pl.BlockSpec((tm, cwid), row),
            pl.BlockSpec((tm, d), row),
            _const_spec(w_out.shape),
            _const_spec((1, d)),
            _const_spec((1, d)),
            _const_spec(w_mq.shape),
            _const_spec((1, mem_rows, mem_hd)),
            _const_spec((1, mem_rows, mem_hd)),
            _const_spec(w_mo.shape),
            _const_spec((1, d)),
        ],
        out_specs=pl.BlockSpec((tm, d), row),
        out_shape=jax.ShapeDtypeStruct((t, d), F32),
        name="cross_prompt",
        compiler_params=_params(("arbitrary",)),
    )(o, c, x, w_out, g_post, g_pre, w_mq, mem_k, mem_v, w_mo, g_cpost)


def _cross_sample_kernel(o_ref, c_ref, x_ref, wout_ref, gpost_ref, gpre_ref, wmq_ref, mk_ref, mv_ref, wmo_ref,
                         gcpost_ref, y_ref, x1_scr, q_scr, att_scr, *, mem_heads, n_t):
    b = pl.program_id(0)
    d = x_ref.shape[-1]
    hd = d // mem_heads
    n_lt = hd // LANES
    n_mem = mk_ref.shape[1] // (mem_heads * n_lt)

    @pl.when(b == 0)
    def _():
        mix_in = jnp.concatenate([o_ref[...].astype(BF16), c_ref[...].astype(BF16)], axis=-1)
        mix = jnp.dot(mix_in, wout_ref[...], preferred_element_type=F32)
        x1 = x_ref[...] + _rms(mix, gpost_ref[...])
        x1_scr[...] = x1
        hq = _rms(x1, gpre_ref[...]).astype(BF16)
        q_scr[...] = jnp.dot(hq, wmq_ref[...], preferred_element_type=F32) * (hd ** -0.5)

    def head_rows(ref, h):
        return jnp.concatenate(
            [ref[0, pl.ds(lt * mem_heads + h, n_mem, stride=mem_heads * n_lt), :] for lt in range(n_lt)],
            axis=-1).astype(BF16)

    rows = pl.ds(pl.multiple_of(b * n_t, n_t), n_t)
    q = q_scr[rows, :].astype(BF16)
    for h in range(mem_heads):
        sl = slice(h * hd, (h + 1) * hd)
        s = lax.dot_general(q[:, sl], head_rows(mk_ref, h), (((1,), (1,)), ((), ())),
                            preferred_element_type=F32)
        p = jnp.exp(s - jnp.max(s, axis=-1, keepdims=True))
        p = p / jnp.sum(p, axis=-1, keepdims=True)
        att_scr[rows, sl] = jnp.dot(p.astype(BF16), head_rows(mv_ref, h), preferred_element_type=F32)

    @pl.when(b == pl.num_programs(0) - 1)
    def _():
        att = jnp.dot(att_scr[...].astype(BF16), wmo_ref[...], preferred_element_type=F32)
        y_ref[...] = x1_scr[...] + _rms(att, gcpost_ref[...])


def _cross_sample(o, c, x, w_out, g_post, g_pre, w_mq, mem_k, mem_v, w_mo, g_cpost, *, n_t, mem_heads):
    t, d = x.shape
    n_b, mem_rows, mem_hd = mem_k.shape
    assert n_t == SUBLANES and t == n_b * n_t
    mem_map = lambda b: (b, 0, 0)
    return pl.pallas_call(
        functools.partial(_cross_sample_kernel, mem_heads=mem_heads, n_t=n_t),
        grid=(n_b,),
        in_specs=[
            _const_spec(o.shape),
            _const_spec(c.shape),
            _const_spec(x.shape),
            _const_spec(w_out.shape),
            _const_spec((1, d)),
            _const_spec((1, d)),
            _const_spec(w_mq.shape),
            pl.BlockSpec((1, mem_rows, mem_hd), mem_map),
            pl.BlockSpec((1, mem_rows, mem_hd), mem_map),
            _const_spec(w_mo.shape),
            _const_spec((1, d)),
        ],
        out_specs=pl.BlockSpec((t, d), lambda b: (0, 0)),
        out_shape=jax.ShapeDtypeStruct((t, d), F32),
        scratch_shapes=[pltpu.VMEM((t, d), F32), pltpu.VMEM((t, d), F32), pltpu.VMEM((t, d), F32)],
        name="cross_sample",
        compiler_params=_params(("arbitrary",)),
    )(o, c, x, w_out, g_post, g_pre, w_mq, mem_k, mem_v, w_mo, g_cpost)


def _mlp_kernel(x_ref, gpre_ref, w1_ref, w2_ref, gpost_ref, y_ref):
    x = x_ref[...]
    h = _rms(x, gpre_ref[...]).astype(BF16)
    hf = jnp.dot(h, w1_ref[...], preferred_element_type=F32)
    r = jnp.maximum(hf, 0.0)
    f = jnp.dot((r * r).astype(BF16), w2_ref[...], preferred_element_type=F32)
    y_ref[...] = x + _rms(f, gpost_ref[...])


def _mlp(x, g_pre, w1, w2, g_post, *, tm):
    t, d = x.shape
    row = lambda i: (i, 0)
    return pl.pallas_call(
        _mlp_kernel,
        grid=(t // tm,),
        in_specs=[pl.BlockSpec((tm, d), row), _const_spec((1, d)), _const_spec(w1.shape),
                  _const_spec(w2.shape), _const_spec((1, d))],
        out_specs=pl.BlockSpec((tm, d), row),
        out_shape=jax.ShapeDtypeStruct((t, d), F32),
        name="mlp_tile%d" % tm,
        compiler_params=_params(("arbitrary",)),
    )(x, g_pre, w1, w2, g_post)


def _rope_tables(pos):
    half = HEAD_DIM // 2
    inv = ROPE_THETA ** (-jnp.arange(0, half, dtype=F32) * 2.0 / HEAD_DIM)
    ang = pos.astype(F32)[:, None] * inv[None, :]
    cos = jnp.cos(ang)
    sin = jnp.sin(ang)
    reps = LANES // HEAD_DIM
    return (jnp.tile(jnp.concatenate([cos, cos], axis=-1), (1, reps)),
            jnp.tile(jnp.concatenate([-sin, sin], axis=-1), (1, reps)))


def _conv_toeplitz(conv_w, n_t):
    n_r = CONV_K - 1 + n_t
    r = jnp.arange(n_r)[:, None]
    t = jnp.arange(n_t)[None, :]
    j = r - t
    ok = (j >= 0) & (j < CONV_K)
    return jnp.where(ok[:, :, None], conv_w[jnp.clip(j, 0, CONV_K - 1)], 0.0)


def _layer(i, depth_inputs, x_prompt, x_sample, mem_prompt, cache_k, cache_v, cache_conv, cache_mem_k,
           cache_mem_v, page_table, *, tile):
    (w_in, lq1, lk1, lq2, lk2, subln_g, conv_w, conv_b, ln_g, ln_b, w_out, mix_pre_g, mix_post_g, mem_norm_g,
     w_mq, w_mk, w_mv, w_mo, cross_pre_g, cross_post_g, w_ff1, w_ff2, mlp_pre_g, mlp_post_g) = depth_inputs
    lam_init = 0.8 - 0.6 * math.exp(-0.3 * i)
    _, seq, d = x_prompt.shape
    n_b, n_t, _ = x_sample.shape
    n_heads, v_dim = cache_v.shape[-2:]
    aw = n_heads * v_dim
    cwid = cache_conv.shape[-1]
    mem_heads = cache_mem_k.shape[-2]
    n_mem = mem_prompt.shape[1]
    past = page_table.shape[1] * cache_k.shape[1]

    row2 = lambda a: a.reshape(1, -1)
    w_in_b, w_out_b, w_mq_b, w_mk_b, w_mv_b, w_mo_b, w_ff1_b, w_ff2_b = (
        w.astype(BF16) for w in (w_in, w_out, w_mq, w_mk, w_mv, w_mo, w_ff1, w_ff2))
    lams = (row2(lq1), row2(lk1), row2(lq2), row2(lk2))

    mk_p, mv_p = _memory_kv(mem_prompt[0], row2(mem_norm_g), w_mk_b, w_mv_b, mem_heads=mem_heads)
    cos_p, sin_p = _rope_tables(jnp.arange(seq, dtype=jnp.int32))
    k_p, v_p, qt, kst, vt, c_p, tail_p = _inproj_prompt(
        x_prompt[0], row2(mix_pre_g), w_in_b, cos_p, sin_p, conv_w, row2(conv_b), row2(ln_g), row2(ln_b),
        tm=tile, aw=aw, cwid=cwid, n_heads=n_heads)

    pos_s = jnp.tile(past + jnp.arange(n_t, dtype=jnp.int32), n_b)
    cos_s, sin_s = _rope_tables(pos_s)
    xs = x_sample.reshape(n_b * n_t, d)
    k_s, v_s, q_s, glu_s, c_s = _inproj_sample(
        xs, row2(mix_pre_g), w_in_b, cos_s, sin_s, cache_conv, _conv_toeplitz(conv_w, n_t), row2(conv_b),
        row2(ln_g), row2(ln_b), aw=aw, cwid=cwid, n_heads=n_heads, n_b=n_b, n_t=n_t)
    n_pool, page = cache_k.shape[:2]
    cache_kt = jnp.transpose(cache_k, (0, 2, 3, 4, 1)).reshape(n_pool, aw, page)
    cache_vr = cache_v.reshape(n_pool, page * n_heads, v_dim)
    o_p, o_s = _attention(page_table, qt, kst, vt, *lams, subln_g, q_s, k_s, v_s, cache_kt, cache_vr,
                          n_b=n_b, n_t=n_t, lam_init=lam_init)
    x2_p = _cross(o_p, c_p, x_prompt[0], w_out_b, row2(mix_post_g), row2(cross_pre_g), w_mq_b,
                  mk_p[None], mv_p[None], w_mo_b, row2(cross_post_g), tm=tile, mem_heads=mem_heads)
    y_p = _mlp(x2_p, row2(mlp_pre_g), w_ff1_b, w_ff2_b, row2(mlp_post_g), tm=tile)
    n_lt = d // mem_heads // LANES

    def mem_rows(a):
        a = a.reshape(a.shape[0], n_mem, mem_heads, n_lt, LANES)
        return jnp.transpose(a, (0, 1, 3, 2, 4)).reshape(a.shape[0], n_mem * n_lt * mem_heads, LANES)

    x2_s = _cross_sample(o_s, c_s.reshape(n_b * n_t, cwid), xs, w_out_b, row2(mix_post_g), row2(cross_pre_g),
                         w_mq_b, mem_rows(cache_mem_k), mem_rows(cache_mem_v), w_mo_b,
                         row2(cross_post_g), n_t=n_t, mem_heads=mem_heads)

    def mem_out(a):
        a = a.reshape(n_mem, n_lt, mem_heads, LANES)
        return jnp.transpose(a, (0, 2, 1, 3)).reshape(1, n_mem, mem_heads, n_lt * LANES)
    y_s = _mlp(x2_s, row2(mlp_pre_g), w_ff1_b, w_ff2_b, row2(mlp_post_g), tm=n_b * n_t)

    hd2 = (n_heads, 2, HEAD_DIM)
    outs = dict(
        y_p=y_p[None], y_s=y_s.reshape(n_b, n_t, d),
        k_p=jnp.transpose(k_p.reshape(*hd2, seq), (3, 0, 1, 2))[None],
        v_p=v_p.reshape(1, seq, n_heads, v_dim),
        conv_p=tail_p[CONV_HALO - (CONV_K - 1):][None],
        mk_p=mem_out(mk_p), mv_p=mem_out(mv_p),
        k_s=k_s.reshape(n_b, n_t, *hd2), v_s=v_s.reshape(n_b, n_t, n_heads, v_dim),
        conv_s=jnp.concatenate([cache_conv[:, n_t:], glu_s], axis=1),
    )
    return outs


def kernel(x_prompt, x_sample, mem_prompt, cache_k, cache_v, cache_conv, cache_mem_k, cache_mem_v, page_table, w_in, lambda_q1, lambda_k1, lambda_q2, lambda_k2, subln_g, conv_w, conv_b, conv_ln_g, conv_ln_b, w_out, mix_pre_g, mix_post_g, mem_norm_g, w_mq, w_mk, w_mv, w_mo, cross_pre_g, cross_post_g, w_ff1, w_ff2, mlp_pre_g, mlp_post_g):
    depth = w_in.shape[0]
    stacked = (w_in, lambda_q1, lambda_k1, lambda_q2, lambda_k2, subln_g, conv_w, conv_b, conv_ln_g, conv_ln_b,
               w_out, mix_pre_g, mix_post_g, mem_norm_g, w_mq, w_mk, w_mv, w_mo, cross_pre_g, cross_post_g,
               w_ff1, w_ff2, mlp_pre_g, mlp_post_g)
    tile = min(512, x_prompt.shape[1])
    y_p, y_s = x_prompt, x_sample
    per_layer = []
    for i in range(depth):
        o = _layer(i, tuple(a[i] for a in stacked), y_p, y_s, mem_prompt, cache_k[i], cache_v[i], cache_conv[i],
                   cache_mem_k[i], cache_mem_v[i], page_table, tile=tile)
        y_p, y_s = o["y_p"], o["y_s"]
        per_layer.append(o)
    stack = lambda name: jnp.stack([o[name] for o in per_layer], axis=0)
    return (y_p, y_s, stack("k_p"), stack("v_p"), stack("conv_p"), stack("mk_p"), stack("mv_p"),
            stack("k_s"), stack("v_s"), stack("conv_s"))
```

```python
import functools
import math

import jax
import numpy as np
import jax.numpy as jnp
from jax import lax
from jax.experimental import pallas as pl
from jax.experimental.pallas import tpu as pltpu

F32 = jnp.float32
BF16 = jnp.bfloat16

RMS_EPS = 1e-6
LN_EPS = 1e-5
ROPE_THETA = 10000.0
LANES = 128
SUBLANES = 8
HEAD_DIM = 64
V_DIM = 2 * HEAD_DIM
CONV_K = 31
CONV_HALO = 32
SUM_ROWS = 16
VMEM_LIMIT = 56 * 1024 * 1024


def _rms(x, g):
    return x * lax.rsqrt(jnp.mean(x * x, axis=-1, keepdims=True) + RMS_EPS) * g


def _lam(lam_ref, lam_init):
    a = jnp.sum(lam_ref[0:1, :] * lam_ref[1:2, :], axis=-1, keepdims=True)
    b = jnp.sum(lam_ref[2:3, :] * lam_ref[3:4, :], axis=-1, keepdims=True)
    return jnp.exp(a) - jnp.exp(b) + lam_init


def _rope_lanes(x, cos, sin_signed, first_half):
    rot = jnp.where(first_half, pltpu.roll(x, LANES - HEAD_DIM // 2, 1), pltpu.roll(x, HEAD_DIM // 2, 1))
    return x * cos + rot * sin_signed


def _const_spec(shape):
    nd = len(shape)
    return pl.BlockSpec(shape, lambda *_: (0,) * nd, pipeline_mode=pl.Buffered(1))


def _params(sem):
    return pltpu.CompilerParams(dimension_semantics=sem, vmem_limit_bytes=VMEM_LIMIT)


def _memory_kv_kernel(mem_ref, g_ref, wk_ref, wv_ref, mk_ref, mv_ref, *, mem_heads):
    m = _rms(mem_ref[...], g_ref[...]).astype(BF16)
    n, d = mem_ref.shape
    hd = d // mem_heads
    mk = jnp.dot(m, wk_ref[...], preferred_element_type=F32)
    mv = jnp.dot(m, wv_ref[...], preferred_element_type=F32)
    n_lt = hd // LANES
    for h in range(mem_heads):
        for lt in range(n_lt):
            rows = pl.ds(lt * mem_heads + h, n, stride=mem_heads * n_lt)
            col = h * hd + lt * LANES
            mk_ref[rows, :] = mk[:, col:col + LANES]
            mv_ref[rows, :] = mv[:, col:col + LANES]


def _memory_kv(mem, g, wk, wv, *, mem_heads):
    n, d = mem.shape
    out = jax.ShapeDtypeStruct((n * d // LANES, LANES), F32)
    return pl.pallas_call(
        functools.partial(_memory_kv_kernel, mem_heads=mem_heads),
        out_shape=(out, out),
        name="memory_kv",
        compiler_params=pltpu.CompilerParams(vmem_limit_bytes=VMEM_LIMIT),
    )(mem, g, wk, wv)


def _inproj_prompt_kernel(x_ref, g_ref, w_ref, rope_tile_ref, rope_row_ref, cw_ref, cb_ref, lng_ref, lnb_ref,
                          kout_ref, vout_ref, qt_ref, kst_ref, vt_ref, c_ref, tail_ref, gpad_ref, gsh_ref,
                          *, tm, aw, cwid, n_heads, scale, row_chunk):
    i = pl.program_id(0)

    @pl.when(i == 0)
    def _():
        gpad_ref[0:CONV_HALO, :] = jnp.zeros((CONV_HALO, cwid), F32)

    h = _rms(x_ref[...], g_ref[...]).astype(BF16)
    cos_a = rope_tile_ref[0, pl.ds(i, 1), :]
    sin_a = rope_tile_ref[1, pl.ds(i, 1), :]
    cos_b = rope_row_ref[0]
    sin_b = rope_row_ref[1]
    cos = cos_a * cos_b - sin_a * sin_b
    sin = sin_a * cos_b + cos_a * sin_b
    lane = lax.broadcasted_iota(jnp.int32, (tm, LANES), 1)
    first_half = (lane % HEAD_DIM) < (HEAD_DIM // 2)
    lo = lane < HEAD_DIM
    sum_rows = (lax.broadcasted_iota(jnp.int32, (SUM_ROWS, tm), 0) == 0).astype(BF16)

    q = jnp.dot(h, w_ref[:, 0:aw], preferred_element_type=F32)
    k = jnp.dot(h, w_ref[:, aw:2 * aw], preferred_element_type=F32)
    v = jnp.dot(h, w_ref[:, 2 * aw:3 * aw], preferred_element_type=F32)
    for hd in range(n_heads):
        sl = slice(hd * LANES, (hd + 1) * LANES)
        qh = _rope_lanes(q[:, sl], cos, sin, first_half) * scale
        kh = _rope_lanes(k[:, sl], cos, sin, first_half)
        kout_ref[sl, :] = kh.T
        vout_ref[pl.ds(hd, tm, stride=n_heads), :] = v[:, sl]
        qt_ref[hd, 0] = qh.T.astype(BF16)
        kst_ref[hd, 0, 0:tm, :] = jnp.where(lo, kh, 0.0).astype(BF16)
        kst_ref[hd, 0, tm:2 * tm, :] = jnp.where(lo, 0.0, kh).astype(BF16)
        vt_ref[hd, 0, 0:V_DIM, :] = v[:, sl].T.astype(BF16)
        vt_ref[hd, 0, V_DIM:V_DIM + SUM_ROWS, :] = sum_rows

    u1 = jnp.dot(h, w_ref[:, 3 * aw:3 * aw + cwid], preferred_element_type=F32)
    u2 = jnp.dot(h, w_ref[:, 3 * aw + cwid:3 * aw + 2 * cwid], preferred_element_type=F32)
    gpad_ref[CONV_HALO:CONV_HALO + tm, :] = u1 * jax.nn.sigmoid(u2)

    base = CONV_HALO - (CONV_K - 1)
    cb = cb_ref[...]
    lng = lng_ref[...]
    lnb = lnb_ref[...]
    n_sh = gsh_ref.shape[1]
    for b in range(1, SUBLANES):
        gsh_ref[b - 1] = gpad_ref[pl.ds(b, n_sh), :]
    for r0 in range(0, tm, row_chunk):
        acc = jnp.zeros((row_chunk, cwid), F32)
        for j in range(CONV_K):
            a, b = divmod(base + j, SUBLANES)
            rows = pl.ds(r0 + a * SUBLANES, row_chunk)
            tap = gpad_ref[rows, :] if b == 0 else gsh_ref[b - 1, rows, :]
            acc = acc + cw_ref[j:j + 1, :] * tap
        acc = acc + cb
        mu = jnp.mean(acc, axis=-1, keepdims=True)
        d = acc - mu
        var = jnp.mean(d * d, axis=-1, keepdims=True)
        y = d * lax.rsqrt(var + LN_EPS) * lng + lnb
        c_ref[r0:r0 + row_chunk, :] = (y * jax.nn.sigmoid(y)).astype(c_ref.dtype)

    tail_ref[...] = gpad_ref[tm:tm + CONV_HALO, :]
    gpad_ref[0:CONV_HALO, :] = gpad_ref[tm:tm + CONV_HALO, :]


def _inproj_prompt(x, g, w_in, rope_tile, rope_row, conv_w, conv_b, ln_g, ln_b, *, tm, aw, cwid, n_heads):
    t, d = x.shape
    nt = t // tm
    kern = functools.partial(_inproj_prompt_kernel, tm=tm, aw=aw, cwid=cwid, n_heads=n_heads,
                             scale=HEAD_DIM ** -0.5 * math.log2(math.e), row_chunk=min(tm, 32))
    row = lambda i: (i, 0)
    per_head = lambda i: (0, i, 0, 0)
    return pl.pallas_call(
        kern,
        grid=(nt,),
        in_specs=[
            pl.BlockSpec((tm, d), row),
            _const_spec((1, d)),
            _const_spec(w_in.shape),
            _const_spec(rope_tile.shape),
            _const_spec(rope_row.shape),
            _const_spec(conv_w.shape),
            _const_spec((1, cwid)),
            _const_spec((1, cwid)),
            _const_spec((1, cwid)),
        ],
        out_specs=[
            pl.BlockSpec((aw, tm), lambda i: (0, i)),
            pl.BlockSpec((tm * n_heads, V_DIM), row),
            pl.BlockSpec((n_heads, 1, LANES, tm), per_head),
            pl.BlockSpec((n_heads, 1, 2 * tm, LANES), per_head),
            pl.BlockSpec((n_heads, 1, V_DIM + SUM_ROWS, tm), per_head),
            pl.BlockSpec((tm, cwid), row),
            pl.BlockSpec((CONV_HALO, cwid), lambda i: (0, 0)),
        ],
        out_shape=(
            jax.ShapeDtypeStruct((aw, t), F32),
            jax.ShapeDtypeStruct((t * n_heads, V_DIM), F32),
            jax.ShapeDtypeStruct((n_heads, nt, LANES, tm), BF16),
            jax.ShapeDtypeStruct((n_heads, nt, 2 * tm, LANES), BF16),
            jax.ShapeDtypeStruct((n_heads, nt, V_DIM + SUM_ROWS, tm), BF16),
            jax.ShapeDtypeStruct((t, cwid), BF16),
            jax.ShapeDtypeStruct((CONV_HALO, cwid), F32),
        ),
        scratch_shapes=[pltpu.VMEM((tm + CONV_HALO, cwid), F32),
                        pltpu.VMEM((SUBLANES - 1, tm + CONV_HALO - SUBLANES, cwid), F32)],
        name="inproj_prompt",
        compiler_params=_params(("arbitrary",)),
    )(x, g, w_in, rope_tile, rope_row, conv_w, conv_b, ln_g, ln_b)


def _inproj_sample_kernel(x_ref, g_ref, w_ref, cos_ref, sin_ref, cc_ref, toep_ref, cb_ref, lng_ref, lnb_ref,
                          kout_ref, vout_ref, q_ref, glu_ref, c_ref, *, aw, cwid, n_heads, scale, n_b, n_t):
    rows = n_b * n_t
    h = _rms(x_ref[...], g_ref[...]).astype(BF16)
    cos = cos_ref[...]
    sin = sin_ref[...]
    lane = lax.broadcasted_iota(jnp.int32, (rows, LANES), 1)
    first_half = (lane % HEAD_DIM) < (HEAD_DIM // 2)
    q = jnp.dot(h, w_ref[:, 0:aw], preferred_element_type=F32)
    k = jnp.dot(h, w_ref[:, aw:2 * aw], preferred_element_type=F32)
    vout_ref[...] = jnp.dot(h, w_ref[:, 2 * aw:3 * aw], preferred_element_type=F32)
    for hd in range(n_heads):
        sl = slice(hd * LANES, (hd + 1) * LANES)
        q_ref[:, sl] = _rope_lanes(q[:, sl], cos, sin, first_half) * scale
        kout_ref[:, sl] = _rope_lanes(k[:, sl], cos, sin, first_half)
    u1 = jnp.dot(h, w_ref[:, 3 * aw:3 * aw + cwid], preferred_element_type=F32)
    u2 = jnp.dot(h, w_ref[:, 3 * aw + cwid:3 * aw + 2 * cwid], preferred_element_type=F32)
    glu_ref[...] = (u1 * jax.nn.sigmoid(u2)).reshape(n_b, n_t, cwid)

    acc = jnp.zeros((n_b, n_t, cwid), F32)
    for r in range(CONV_K - 1):
        acc = acc + cc_ref[:, r:r + 1, :] * toep_ref[r]
    for r in range(n_t):
        acc = acc + glu_ref[:, r:r + 1, :] * toep_ref[CONV_K - 1 + r]
    acc = acc + cb_ref[...]
    mu = jnp.mean(acc, axis=-1, keepdims=True)
    d = acc - mu
    var = jnp.mean(d * d, axis=-1, keepdims=True)
    y = d * lax.rsqrt(var + LN_EPS) * lng_ref[...] + lnb_ref[...]
    c_ref[...] = y * jax.nn.sigmoid(y)


def _inproj_sample(x, g, w_in, cos, sin, cache_conv, toep, conv_b, ln_g, ln_b, *, aw, cwid, n_heads, n_b, n_t):
    rows = n_b * n_t
    kern = functools.partial(_inproj_sample_kernel, aw=aw, cwid=cwid, n_heads=n_heads,
                             scale=HEAD_DIM ** -0.5, n_b=n_b, n_t=n_t)
    return pl.pallas_call(
        kern,
        out_shape=(
            jax.ShapeDtypeStruct((rows, aw), F32),
            jax.ShapeDtypeStruct((rows, aw), F32),
            jax.ShapeDtypeStruct((rows, aw), F32),
            jax.ShapeDtypeStruct((n_b, n_t, cwid), F32),
            jax.ShapeDtypeStruct((n_b, n_t, cwid), F32),
        ),
        name="inproj_sample",
        compiler_params=pltpu.CompilerParams(vmem_limit_bytes=VMEM_LIMIT),
    )(x, g, w_in, cos, sin, cache_conv, toep, conv_b, ln_g, ln_b)


def _masked_queries(q, n_heads, n_t):
    n_rows = n_heads * 2 * n_t
    aw = q.shape[-1]
    qrep = jnp.concatenate([q] * (n_heads * 2), axis=0)
    sel = (lax.broadcasted_iota(jnp.int32, (n_rows, aw), 0) // n_t
           == lax.broadcasted_iota(jnp.int32, (n_rows, aw), 1) // HEAD_DIM)
    return jnp.where(sel, qrep, 0.0).astype(BF16)


def _sample_init_kernel(q_ref, kn_ref, vn_ref, m_ref, l_ref, acc_ref, *, n_heads, n_t, group):
    aw = q_ref.shape[-1]
    n_rows = n_heads * 2 * n_t
    pad = jnp.zeros((LANES - n_t, aw), F32)
    col = lax.broadcasted_iota(jnp.int32, (n_rows, LANES), 1)
    qt = lax.broadcasted_iota(jnp.int32, (n_rows, LANES), 0) % n_t
    for b in range(group):
        rows = slice(b * n_t, (b + 1) * n_t)
        wq = _masked_queries(q_ref[rows, :], n_heads, n_t)
        kn = jnp.concatenate([kn_ref[rows, :], pad], axis=0).astype(BF16)
        vn = jnp.concatenate([vn_ref[rows, :], pad], axis=0).astype(BF16)
        sc = lax.dot_general(wq, kn, (((1,), (1,)), ((), ())), preferred_element_type=F32)
        sc = jnp.where(col <= qt, sc, -jnp.inf)
        m = jnp.max(sc, axis=-1, keepdims=True)
        p = jnp.exp(sc - m)
        m_ref[b] = m
        l_ref[b] = jnp.sum(p, axis=-1, keepdims=True)
        acc_ref[b] = jnp.dot(p.astype(BF16), vn, preferred_element_type=F32)


def _sample_init(q, k_new, v_new, *, n_b, n_t, n_heads):
    aw = q.shape[-1]
    n_rows = n_heads * 2 * n_t
    group = math.gcd(n_b, SUBLANES)
    tok = pl.BlockSpec((group * n_t, aw), lambda g: (g, 0))
    stat = pl.BlockSpec((group, n_rows, 1), lambda g: (g, 0, 0))
    return pl.pallas_call(
        functools.partial(_sample_init_kernel, n_heads=n_heads, n_t=n_t, group=group),
        grid=(n_b // group,),
        in_specs=[tok, tok, tok],
        out_specs=[stat, stat, pl.BlockSpec((group, n_rows, aw), lambda g: (g, 0, 0))],
        out_shape=(jax.ShapeDtypeStruct((n_b, n_rows, 1), F32), jax.ShapeDtypeStruct((n_b, n_rows, 1), F32),
                   jax.ShapeDtypeStruct((n_b, n_rows, aw), F32)),
        name="sample_init",
        compiler_params=_params(("arbitrary",)),
    )(q, k_new, v_new)


def _attn_kernel(pt_ref, th_ref, ti_ref, tt_ref, tl_ref, te_ref, qt_ref, kst_ref, vt_ref, lam_ref, sgcol_ref,
                 sgrow_ref, qs_ref, m0_ref, l0_ref, a0_ref, *rest,
                 n_pages, n_units, units_per_entry, tk, page, n_heads, n_t, lam_init):
    k_pages = rest[:n_pages]
    v_pages = rest[n_pages:2 * n_pages]
    op_ref, os_ref = rest[2 * n_pages:2 * n_pages + 2]
    (m_ref, acc_ref, q_scr, sa_ref, sb_ref, xa_ref, xb_ref, pend_ref,
     kbuf, vbuf, ms_ref, ls_ref, accs_ref) = rest[2 * n_pages + 2:]
    g = pl.program_id(0)
    i = ti_ref[g]
    t = tt_ref[g]
    tq = 2 * tk
    lam = _lam(lam_ref, lam_init)

    unit = jnp.minimum(g, n_units - 1)
    active = g < n_units
    first = unit % units_per_entry == 0

    def sample_scores():
        for p in range(n_pages):
            kbuf[:, p * page:(p + 1) * page] = k_pages[p][0].astype(BF16)
            for hd in range(n_heads):
                vbuf[p * page:(p + 1) * page, hd * V_DIM:(hd + 1) * V_DIM] = (
                    v_pages[p][0, pl.ds(hd, page, stride=n_heads), :].astype(BF16))
        wq = _masked_queries(qs_ref[...], n_heads, n_t)
        sc = jnp.dot(wq, kbuf[...], preferred_element_type=F32)
        return jnp.where(active, sc, -jnp.inf)

    def sample_softmax(sc):
        m_prev = jnp.where(first, m0_ref[0], ms_ref[...])
        l_prev = jnp.where(first, l0_ref[0], ls_ref[...])
        m_new = jnp.maximum(m_prev, jnp.max(sc, axis=-1, keepdims=True))
        p = jnp.exp(sc - m_new)
        alpha = jnp.exp(m_prev - m_new)
        l_new = alpha * l_prev + jnp.sum(p, axis=-1, keepdims=True)
        ms_ref[...] = m_new
        ls_ref[...] = l_new
        return p.astype(BF16), alpha, l_new

    def sample_values(p, alpha, l_new):
        acc = (alpha * jnp.where(first, a0_ref[0], accs_ref[...])
               + jnp.dot(p, vbuf[...], preferred_element_type=F32))
        accs_ref[...] = acc
        o = acc / l_new
        for hd in range(n_heads):
            r0 = hd * 2 * n_t
            sl = slice(hd * V_DIM, (hd + 1) * V_DIM)
            a = o[r0:r0 + n_t, sl] - lam * o[r0 + n_t:r0 + 2 * n_t, sl]
            os_ref[:, sl] = _rms(a, sgrow_ref[...]) * (1.0 - lam_init)

    def scores(j, s_ref, smax_ref, key_offset=None):
        st = jnp.dot(kst_ref[0, j], q_scr[...], preferred_element_type=F32)
        if key_offset is not None:
            keep = (lax.broadcasted_iota(jnp.int32, (tk, tq), 0) + key_offset
                    <= lax.broadcasted_iota(jnp.int32, (tk, tq), 1))
        for c in range(2):
            sc = st[c * tk:(c + 1) * tk, :]
            if key_offset is not None:
                sc = jnp.where(keep, sc, -jnp.inf)
            s_ref[c * tk:(c + 1) * tk, :] = sc
            smax_ref[c] = jnp.max(sc, axis=0, keepdims=True)

    def absorb(j, s_ref, smax_ref):
        vt = vt_ref[0, j]
        for c in range(2):
            sc = s_ref[c * tk:(c + 1) * tk, :]
            m_prev = m_ref[c]
            m_new = jnp.maximum(m_prev, smax_ref[c])
            p = jnp.exp2(sc - m_new).astype(BF16)
            alpha = jnp.exp2(m_prev - m_new)
            acc_ref[c] = alpha * acc_ref[c] + jnp.dot(vt, p, preferred_element_type=F32)
            m_ref[c] = m_new

    @pl.when(t < 0)
    def _():
        q_scr[:, 0:tk] = qt_ref[0, 0]
        q_scr[:, tk:tq] = qt_ref[0, 1]
        m_ref[...] = jnp.full(m_ref.shape, -jnp.inf, F32)
        acc_ref[...] = jnp.zeros(acc_ref.shape, F32)
        sc = sample_scores()
        scores(2 * i, sa_ref, xa_ref, key_offset=0)
        soft = sample_softmax(sc)
        scores(2 * i + 1, sb_ref, xb_ref, key_offset=tk)
        sample_values(*soft)
        absorb(2 * i, sa_ref, xa_ref)
        pend_ref[0] = 2 * i + 1

    @pl.when(t >= 0)
    def _():
        pending = pend_ref[0]
        sc = sample_scores()
        scores(2 * t, sa_ref, xa_ref)
        soft = sample_softmax(sc)
        absorb(pending, sb_ref, xb_ref)
        sample_values(*soft)
        scores(2 * t + 1, sb_ref, xb_ref)
        absorb(2 * t, sa_ref, xa_ref)
        pend_ref[0] = 2 * t + 1

    @pl.when(tl_ref[g] == 1)
    def _():
        absorb(pend_ref[0], sb_ref, xb_ref)
        o1 = acc_ref[0, 0:V_DIM, :] / acc_ref[0, V_DIM:V_DIM + 1, :]
        o2 = acc_ref[1, 0:V_DIM, :] / acc_ref[1, V_DIM:V_DIM + 1, :]
        a = o1 - lam * o2
        y = a * lax.rsqrt(jnp.mean(a * a, axis=0, keepdims=True) + RMS_EPS) * sgcol_ref[...] * (1.0 - lam_init)
        op_ref[...] = y.T.astype(op_ref.dtype)


def _prompt_tasks(n_heads, nq):
    tasks = [(h, i, t, int(t == i - 1)) for h in range(n_heads) for i in range(nq) for t in range(-1, i)]
    return tuple(np.asarray(col, np.int32) for col in zip(*tasks))


def _attention(page_table, qt, kst, vt, lam_vecs, subln_g, q_s, k_new, v_new, cache_kt, cache_v,
               *, n_b, n_t, lam_init):
    n_heads, nt, _, tk = qt.shape
    assert nt % 2 == 0, "a query block spans two key blocks"
    nq = nt // 2
    tq = 2 * tk
    t = nt * tk
    n_pool, aw, page = cache_kt.shape
    n_rows = n_heads * 2 * n_t
    th, ti, tt, tl = _prompt_tasks(n_heads, nq)
    n_tasks = len(th)
    n_pages_entry = page_table.shape[1]
    pages_per_unit = min(p for p in range(1, n_pages_entry + 1)
                         if n_pages_entry % p == 0 and n_b * (n_pages_entry // p) <= n_tasks)
    units_per_entry = n_pages_entry // pages_per_unit
    n_units = n_b * units_per_entry
    m0, l0, a0 = _sample_init(q_s, k_new, v_new, n_b=n_b, n_t=n_t, n_heads=n_heads)
    kern = functools.partial(_attn_kernel, n_pages=pages_per_unit, n_units=n_units,
                             units_per_entry=units_per_entry, tk=tk, page=page, n_heads=n_heads, n_t=n_t,
                             lam_init=lam_init)

    unit_of_step = np.minimum(np.arange(n_tasks), n_units - 1)
    te = (unit_of_step // units_per_entry).astype(np.int32)
    step_pages = page_table.reshape(n_units, pages_per_unit)[unit_of_step].reshape(-1)

    tok = pl.BlockSpec((n_t, aw), lambda g, pt, th, ti, tt, tl, te: (te[g], 0))
    stat = pl.BlockSpec((1, n_rows, 1), lambda g, pt, th, ti, tt, tl, te: (te[g], 0, 0))

    def page_spec(p, shape):
        return pl.BlockSpec((1,) + shape, lambda g, pt, *_: (pt[g * pages_per_unit + p], 0, 0))

    k_specs = [page_spec(p, (aw, page)) for p in range(pages_per_unit)]
    v_specs = [page_spec(p, (page * n_heads, V_DIM)) for p in range(pages_per_unit)]
    head_map = lambda g, pt, th, *_: (th[g], 0, 0, 0)
    grid_spec = pltpu.PrefetchScalarGridSpec(
        num_scalar_prefetch=6,
        grid=(n_tasks,),
        in_specs=[
            pl.BlockSpec((1, 2, LANES, tk), lambda g, pt, th, ti, *_: (th[g], ti[g], 0, 0)),
            pl.BlockSpec((1, nt, 2 * tk, LANES), head_map, pipeline_mode=pl.Buffered(1)),
            pl.BlockSpec((1, nt, V_DIM + SUM_ROWS, tk), head_map, pipeline_mode=pl.Buffered(1)),
            pl.BlockSpec(lam_vecs.shape, lambda g, *_: (0, 0)),
            pl.BlockSpec((V_DIM, 1), lambda g, *_: (0, 0)),
            pl.BlockSpec((1, V_DIM), lambda g, *_: (0, 0)),
            tok, stat, stat,
            pl.BlockSpec((1, n_rows, aw), lambda g, pt, th, ti, tt, tl, te: (te[g], 0, 0)),
        ] + k_specs + v_specs,
        out_specs=[
            pl.BlockSpec((tq, V_DIM), lambda g, pt, th, ti, *_: (ti[g], th[g])),
            tok,
        ],
        scratch_shapes=[
            pltpu.VMEM((2, 1, tq), F32),
            pltpu.VMEM((2, V_DIM + SUM_ROWS, tq), F32),
            pltpu.VMEM((LANES, tq), BF16),
            pltpu.VMEM((2 * tk, tq), F32),
            pltpu.VMEM((2 * tk, tq), F32),
            pltpu.VMEM((2, 1, tq), F32),
            pltpu.VMEM((2, 1, tq), F32),
            pltpu.SMEM((1,), jnp.int32),
            pltpu.VMEM((aw, pages_per_unit * page), BF16),
            pltpu.VMEM((pages_per_unit * page, aw), BF16),
            pltpu.VMEM((n_rows, 1), F32),
            pltpu.VMEM((n_rows, 1), F32),
            pltpu.VMEM((n_rows, aw), F32),
        ],
    )
    return pl.pallas_call(
        kern,
        grid_spec=grid_spec,
        out_shape=(jax.ShapeDtypeStruct((t, n_heads * V_DIM), BF16),
                   jax.ShapeDtypeStruct((n_b * n_t, aw), F32)),
        name="attention",
        compiler_params=_params(("arbitrary",)),
    )(step_pages, th, ti, tt, tl, te, qt, kst, vt, lam_vecs, subln_g.reshape(-1, 1),
      subln_g.reshape(1, -1), q_s, m0, l0, a0, *([cache_kt] * pages_per_unit), *([cache_v] * pages_per_unit))


def _cross_kernel(o_ref, c_ref, x_ref, wout_ref, gpost_ref, gpre_ref, wmq_ref, mk_ref, mv_ref, wmo_ref,
                  gcpost_ref, y_ref, *, mem_heads, row_splits):
    tm, d = x_ref.shape
    hd = d // mem_heads
    n_lt = hd // LANES
    n_mem = mk_ref.shape[1] // (mem_heads * n_lt)

    def head_rows(ref, h):
        return jnp.concatenate(
            [ref[0, pl.ds(lt * mem_heads + h, n_mem, stride=mem_heads * n_lt), :] for lt in range(n_lt)],
            axis=-1).astype(BF16)

    mks = [head_rows(mk_ref, h) for h in range(mem_heads)]
    mvs = [head_rows(mv_ref, h) for h in range(mem_heads)]
    rs = tm // row_splits
    groups = [slice(r * rs, (r + 1) * rs) for r in range(row_splits)]
    scale = hd ** -0.5
    mix = [jnp.dot(jnp.concatenate([o_ref[g, :].astype(BF16), c_ref[g, :].astype(BF16)], axis=-1),
                   wout_ref[...], preferred_element_type=F32) for g in groups]
    x1 = [x_ref[g, :] + _rms(m, gpost_ref[...]) for g, m in zip(groups, mix)]
    q = [(jnp.dot(_rms(x, gpre_ref[...]).astype(BF16), wmq_ref[...], preferred_element_type=F32)
          * scale).astype(BF16) for x in x1]
    outs = [[] for _ in groups]
    for h in range(mem_heads):
        sl = slice(h * hd, (h + 1) * hd)
        s = [lax.dot_general(qr[:, sl], mks[h], (((1,), (1,)), ((), ())), preferred_element_type=F32)
             for qr in q]
        for r, sr in enumerate(s):
            p = jnp.exp(sr - jnp.max(sr, axis=-1, keepdims=True))
            p = p / jnp.sum(p, axis=-1, keepdims=True)
            outs[r].append(jnp.dot(p.astype(BF16), mvs[h], preferred_element_type=F32).astype(BF16))
    att = [jnp.dot(jnp.concatenate(o, axis=-1), wmo_ref[...], preferred_element_type=F32) for o in outs]
    for g, x, a in zip(groups, x1, att):
        y_ref[g, :] = x + _rms(a, gcpost_ref[...])


def _cross(o, c, x, w_out, g_post, g_pre, w_mq, mem_k, mem_v, w_mo, g_cpost, *, tm, mem_heads):
    t, d = x.shape
    aw = o.shape[1]
    cwid = c.shape[1]
    mem_rows, mem_hd = mem_k.shape[1:]
    row = lambda i: (i, 0)
    return pl.pallas_call(
        functools.partial(_cross_kernel, mem_heads=mem_heads, row_splits=1),
        grid=(t // tm,),
        in_specs=[
            pl.BlockSpec((tm, aw), row),
            pl.BlockSpec((tm, cwid), row),
            pl.BlockSpec((tm, d), row),
            _const_spec(w_out.shape),
            _const_spec((1, d)),
            _const_spec((1, d)),
            _const_spec(w_mq.shape),
            _const_spec((1, mem_rows, mem_hd)),
            _const_spec((1, mem_rows, mem_hd)),
            _const_spec(w_mo.shape),
            _const_spec((1, d)),
        ],
        out_specs=pl.BlockSpec((tm, d), row),
        out_shape=jax.ShapeDtypeStruct((t, d), F32),
        name="cross_prompt",
        compiler_params=_params(("arbitrary",)),
    )(o, c, x, w_out, g_post, g_pre, w_mq, mem_k, mem_v, w_mo, g_cpost)


def _cross_sample_kernel(o_ref, c_ref, x_ref, wout_ref, gpost_ref, gpre_ref, wmq_ref, mk_ref, mv_ref, wmo_ref,
                         gcpost_ref, y_ref, x1_scr, q_scr, att_scr, *, mem_heads, n_t):
    b = pl.program_id(0)
    d = x_ref.shape[-1]
    hd = d // mem_heads
    n_lt = hd // LANES
    n_mem = mk_ref.shape[1] // (mem_heads * n_lt)

    @pl.when(b == 0)
    def _():
        mix_in = jnp.concatenate([o_ref[...].astype(BF16), c_ref[...].astype(BF16)], axis=-1)
        mix = jnp.dot(mix_in, wout_ref[...], preferred_element_type=F32)
        x1 = x_ref[...] + _rms(mix, gpost_ref[...])
        x1_scr[...] = x1
        hq = _rms(x1, gpre_ref[...]).astype(BF16)
        q_scr[...] = jnp.dot(hq, wmq_ref[...], preferred_element_type=F32) * (hd ** -0.5)

    def head_rows(ref, h):
        return jnp.concatenate(
            [ref[0, pl.ds(lt * mem_heads + h, n_mem, stride=mem_heads * n_lt), :] for lt in range(n_lt)],
            axis=-1).astype(BF16)

    rows = pl.ds(pl.multiple_of(b * n_t, n_t), n_t)
    q = q_scr[rows, :].astype(BF16)
    for h in range(mem_heads):
        sl = slice(h * hd, (h + 1) * hd)
        s = lax.dot_general(q[:, sl], head_rows(mk_ref, h), (((1,), (1,)), ((), ())),
                            preferred_element_type=F32)
        p = jnp.exp(s - jnp.max(s, axis=-1, keepdims=True))
        p = p / jnp.sum(p, axis=-1, keepdims=True)
        att_scr[rows, sl] = jnp.dot(p.astype(BF16), head_rows(mv_ref, h), preferred_element_type=F32)

    @pl.when(b == pl.num_programs(0) - 1)
    def _():
        att = jnp.dot(att_scr[...].astype(BF16), wmo_ref[...], preferred_element_type=F32)
        y_ref[...] = x1_scr[...] + _rms(att, gcpost_ref[...])


def _cross_sample(o, c, x, w_out, g_post, g_pre, w_mq, mem_k, mem_v, w_mo, g_cpost, *, n_t, mem_heads):
    t, d = x.shape
    n_b, mem_rows, mem_hd = mem_k.shape
    assert n_t == SUBLANES and t == n_b * n_t
    mem_map = lambda b: (b, 0, 0)
    return pl.pallas_call(
        functools.partial(_cross_sample_kernel, mem_heads=mem_heads, n_t=n_t),
        grid=(n_b,),
        in_specs=[
            _const_spec(o.shape),
            _const_spec(c.shape),
            _const_spec(x.shape),
            _const_spec(w_out.shape),
            _const_spec((1, d)),
            _const_spec((1, d)),
            _const_spec(w_mq.shape),
            pl.BlockSpec((1, mem_rows, mem_hd), mem_map),
            pl.BlockSpec((1, mem_rows, mem_hd), mem_map),
            _const_spec(w_mo.shape),
            _const_spec((1, d)),
        ],
        out_specs=pl.BlockSpec((t, d), lambda b: (0, 0)),
        out_shape=jax.ShapeDtypeStruct((t, d), F32),
        scratch_shapes=[pltpu.VMEM((t, d), F32), pltpu.VMEM((t, d), F32), pltpu.VMEM((t, d), F32)],
        name="cross_sample",
        compiler_params=_params(("arbitrary",)),
    )(o, c, x, w_out, g_post, g_pre, w_mq, mem_k, mem_v, w_mo, g_cpost)


def _mlp_kernel(x_ref, gpre_ref, w1_ref, w2_ref, gpost_ref, y_ref):
    x = x_ref[...]
    h = _rms(x, gpre_ref[...]).astype(BF16)
    hf = jnp.dot(h, w1_ref[...], preferred_element_type=F32)
    r = jnp.maximum(hf, 0.0)
    f = jnp.dot((r * r).astype(BF16), w2_ref[...], preferred_element_type=F32)
    y_ref[...] = x + _rms(f, gpost_ref[...])


def _mlp(x, g_pre, w1, w2, g_post, *, tm):
    t, d = x.shape
    row = lambda i: (i, 0)
    return pl.pallas_call(
        _mlp_kernel,
        grid=(t // tm,),
        in_specs=[pl.BlockSpec((tm, d), row), _const_spec((1, d)), _const_spec(w1.shape),
                  _const_spec(w2.shape), _const_spec((1, d))],
        out_specs=pl.BlockSpec((tm, d), row),
        out_shape=jax.ShapeDtypeStruct((t, d), F32),
        name="mlp_tile%d" % tm,
        compiler_params=_params(("arbitrary",)),
    )(x, g_pre, w1, w2, g_post)


def _rope_tables(pos):
    half = HEAD_DIM // 2
    inv = ROPE_THETA ** (-jnp.arange(0, half, dtype=F32) * 2.0 / HEAD_DIM)
    ang = pos.astype(F32)[:, None] * inv[None, :]
    cos = jnp.cos(ang)
    sin = jnp.sin(ang)
    reps = LANES // HEAD_DIM
    return (jnp.tile(jnp.concatenate([cos, cos], axis=-1), (1, reps)),
            jnp.tile(jnp.concatenate([-sin, sin], axis=-1), (1, reps)))


def _conv_toeplitz(conv_w, n_t):
    n_r = CONV_K - 1 + n_t
    r = jnp.arange(n_r)[:, None]
    t = jnp.arange(n_t)[None, :]
    j = r - t
    ok = (j >= 0) & (j < CONV_K)
    return jnp.where(ok[:, :, None], conv_w[jnp.clip(j, 0, CONV_K - 1)], 0.0)


def _layer(i, depth_inputs, x_prompt, x_sample, mem_prompt, cache_k, cache_v, cache_conv, cache_mem_k,
           cache_mem_v, page_table, *, tile):
    (w_in, lq1, lk1, lq2, lk2, subln_g, conv_w, conv_b, ln_g, ln_b, w_out, mix_pre_g, mix_post_g, mem_norm_g,
     w_mq, w_mk, w_mv, w_mo, cross_pre_g, cross_post_g, w_ff1, w_ff2, mlp_pre_g, mlp_post_g) = depth_inputs
    lam_init = 0.8 - 0.6 * math.exp(-0.3 * i)
    _, seq, d = x_prompt.shape
    n_b, n_t, _ = x_sample.shape
    n_heads, v_dim = cache_v.shape[-2:]
    aw = n_heads * v_dim
    cwid = cache_conv.shape[-1]
    mem_heads = cache_mem_k.shape[-2]
    n_mem = mem_prompt.shape[1]
    past = page_table.shape[1] * cache_k.shape[1]

    row2 = lambda a: a.reshape(1, -1)
    w_in_b, w_out_b, w_mq_b, w_mk_b, w_mv_b, w_mo_b, w_ff1_b, w_ff2_b = (
        w.astype(BF16) for w in (w_in, w_out, w_mq, w_mk, w_mv, w_mo, w_ff1, w_ff2))
    lam_vecs = jnp.stack([lq1, lk1, lq2, lk2])

    mk_p, mv_p = _memory_kv(mem_prompt[0], row2(mem_norm_g), w_mk_b, w_mv_b, mem_heads=mem_heads)
    rope_tile = jnp.stack(_rope_tables(jnp.arange(0, seq, tile, dtype=jnp.int32)))
    rope_row = jnp.stack(_rope_tables(jnp.arange(tile, dtype=jnp.int32)))
    k_p, v_p, qt, kst, vt, c_p, tail_p = _inproj_prompt(
        x_prompt[0], row2(mix_pre_g), w_in_b, rope_tile, rope_row, conv_w, row2(conv_b), row2(ln_g), row2(ln_b),
        tm=tile, aw=aw, cwid=cwid, n_heads=n_heads)

    pos_s = jnp.tile(past + jnp.arange(n_t, dtype=jnp.int32), n_b)
    cos_s, sin_s = _rope_tables(pos_s)
    xs = x_sample.reshape(n_b * n_t, d)
    k_s, v_s, q_s, glu_s, c_s = _inproj_sample(
        xs, row2(mix_pre_g), w_in_b, cos_s, sin_s, cache_conv, _conv_toeplitz(conv_w, n_t), row2(conv_b),
        row2(ln_g), row2(ln_b), aw=aw, cwid=cwid, n_heads=n_heads, n_b=n_b, n_t=n_t)
    n_pool, page = cache_k.shape[:2]
    cache_kt = jnp.transpose(cache_k, (0, 2, 3, 4, 1)).reshape(n_pool, aw, page)
    cache_vr = cache_v.reshape(n_pool, page * n_heads, v_dim)
    o_p, o_s = _attention(page_table, qt, kst, vt, lam_vecs, subln_g, q_s, k_s, v_s, cache_kt, cache_vr,
                          n_b=n_b, n_t=n_t, lam_init=lam_init)
    x2_p = _cross(o_p, c_p, x_prompt[0], w_out_b, row2(mix_post_g), row2(cross_pre_g), w_mq_b,
                  mk_p[None], mv_p[None], w_mo_b, row2(cross_post_g), tm=tile, mem_heads=mem_heads)
    y_p = _mlp(x2_p, row2(mlp_pre_g), w_ff1_b, w_ff2_b, row2(mlp_post_g), tm=tile)
    n_lt = d // mem_heads // LANES

    def mem_rows(a):
        a = a.reshape(a.shape[0], n_mem, mem_heads, n_lt, LANES)
        return jnp.transpose(a, (0, 1, 3, 2, 4)).reshape(a.shape[0], n_mem * n_lt * mem_heads, LANES)

    x2_s = _cross_sample(o_s, c_s.reshape(n_b * n_t, cwid), xs, w_out_b, row2(mix_post_g), row2(cross_pre_g),
                         w_mq_b, mem_rows(cache_mem_k), mem_rows(cache_mem_v), w_mo_b,
                         row2(cross_post_g), n_t=n_t, mem_heads=mem_heads)

    def mem_out(a):
        a = a.reshape(n_mem, n_lt, mem_heads, LANES)
        return jnp.transpose(a, (0, 2, 1, 3)).reshape(1, n_mem, mem_heads, n_lt * LANES)
    y_s = _mlp(x2_s, row2(mlp_pre_g), w_ff1_b, w_ff2_b, row2(mlp_post_g), tm=n_b * n_t)

    hd2 = (n_heads, 2, HEAD_DIM)
    outs = dict(
        y_p=y_p[None], y_s=y_s.reshape(n_b, n_t, d),
        k_p=jnp.transpose(k_p.reshape(*hd2, seq), (3, 0, 1, 2))[None],
        v_p=v_p.reshape(1, seq, n_heads, v_dim),
        conv_p=tail_p[CONV_HALO - (CONV_K - 1):][None],
        mk_p=mem_out(mk_p), mv_p=mem_out(mv_p),
        k_s=k_s.reshape(n_b, n_t, *hd2), v_s=v_s.reshape(n_b, n_t, n_heads, v_dim),
        conv_s=jnp.concatenate([cache_conv[:, n_t:], glu_s], axis=1),
    )
    return outs


def kernel(x_prompt, x_sample, mem_prompt, cache_k, cache_v, cache_conv, cache_mem_k, cache_mem_v, page_table, w_in, lambda_q1, lambda_k1, lambda_q2, lambda_k2, subln_g, conv_w, conv_b, conv_ln_g, conv_ln_b, w_out, mix_pre_g, mix_post_g, mem_norm_g, w_mq, w_mk, w_mv, w_mo, cross_pre_g, cross_post_g, w_ff1, w_ff2, mlp_pre_g, mlp_post_g):
    depth = w_in.shape[0]
    stacked = (w_in, lambda_q1, lambda_k1, lambda_q2, lambda_k2, subln_g, conv_w, conv_b, conv_ln_g, conv_ln_b,
               w_out, mix_pre_g, mix_post_g, mem_norm_g, w_mq, w_mk, w_mv, w_mo, cross_pre_g, cross_post_g,
               w_ff1, w_ff2, mlp_pre_g, mlp_post_g)
    tile = min(512, x_prompt.shape[1])
    y_p, y_s = x_prompt, x_sample
    per_layer = []
    for i in range(depth):
        o = _layer(i, tuple(a[i] for a in stacked), y_p, y_s, mem_prompt, cache_k[i], cache_v[i], cache_conv[i],
                   cache_mem_k[i], cache_mem_v[i], page_table, tile=tile)
        y_p, y_s = o["y_p"], o["y_s"]
        per_layer.append(o)
    stack = lambda name: jnp.stack([o[name] for o in per_layer], axis=0)
    return (y_p, y_s, stack("k_p"), stack("v_p"), stack("conv_p"), stack("mk_p"), stack("mv_p"),
            stack("k_s"), stack("v_s"), stack("conv_s"))
```

```python
import functools
import math

import jax
import numpy as np
import jax.numpy as jnp
from jax import lax
from jax.experimental import pallas as pl
from jax.experimental.pallas import tpu as pltpu

F32 = jnp.float32
BF16 = jnp.bfloat16

RMS_EPS = 1e-6
LN_EPS = 1e-5
ROPE_THETA = 10000.0
LANES = 128
SUBLANES = 8
HEAD_DIM = 64
V_DIM = 2 * HEAD_DIM
CONV_K = 31
CONV_HALO = 32
SUM_ROWS = 16
SAFE_BOUND = 60.0
VMEM_LIMIT = 56 * 1024 * 1024


def _rms(x, g):
    return x * lax.rsqrt(jnp.mean(x * x, axis=-1, keepdims=True) + RMS_EPS) * g


def _lam(lam_ref, lam_init):
    a = jnp.sum(lam_ref[0:1, :] * lam_ref[1:2, :], axis=-1, keepdims=True)
    b = jnp.sum(lam_ref[2:3, :] * lam_ref[3:4, :], axis=-1, keepdims=True)
    return jnp.exp(a) - jnp.exp(b) + lam_init


def _rope_lanes(x, cos, sin_signed, first_half):
    rot = jnp.where(first_half, pltpu.roll(x, LANES - HEAD_DIM // 2, 1), pltpu.roll(x, HEAD_DIM // 2, 1))
    return x * cos + rot * sin_signed


def _const_spec(shape):
    nd = len(shape)
    return pl.BlockSpec(shape, lambda *_: (0,) * nd, pipeline_mode=pl.Buffered(1))


def _params(sem):
    return pltpu.CompilerParams(dimension_semantics=sem, vmem_limit_bytes=VMEM_LIMIT)


def _memory_kv_kernel(mem_ref, g_ref, wk_ref, wv_ref, mk_ref, mv_ref, *, mem_heads):
    m = _rms(mem_ref[...], g_ref[...]).astype(BF16)
    n, d = mem_ref.shape
    hd = d // mem_heads
    mk = jnp.dot(m, wk_ref[...], preferred_element_type=F32)
    mv = jnp.dot(m, wv_ref[...], preferred_element_type=F32)
    n_lt = hd // LANES
    for h in range(mem_heads):
        for lt in range(n_lt):
            rows = pl.ds(lt * mem_heads + h, n, stride=mem_heads * n_lt)
            col = h * hd + lt * LANES
            mk_ref[rows, :] = mk[:, col:col + LANES]
            mv_ref[rows, :] = mv[:, col:col + LANES]


def _memory_kv(mem, g, wk, wv, *, mem_heads):
    n, d = mem.shape
    out = jax.ShapeDtypeStruct((n * d // LANES, LANES), F32)
    return pl.pallas_call(
        functools.partial(_memory_kv_kernel, mem_heads=mem_heads),
        out_shape=(out, out),
        name="memory_kv",
        compiler_params=pltpu.CompilerParams(vmem_limit_bytes=VMEM_LIMIT),
    )(mem, g, wk, wv)


def _inproj_prompt_kernel(x_ref, g_ref, w_ref, rope_tile_ref, rope_row_ref, cw_ref, cb_ref, lng_ref, lnb_ref,
                          kout_ref, vout_ref, qt_ref, kst_ref, vt_ref, c_ref, tail_ref, qn2_ref, kn2_ref,
                          gpad_ref, gsh_ref, *, tm, aw, cwid, n_heads, scale, row_chunk):
    i = pl.program_id(0)

    @pl.when(i == 0)
    def _():
        gpad_ref[0:CONV_HALO, :] = jnp.zeros((CONV_HALO, cwid), F32)

    h = _rms(x_ref[...], g_ref[...]).astype(BF16)
    cos_a = rope_tile_ref[0, pl.ds(i, 1), :]
    sin_a = rope_tile_ref[1, pl.ds(i, 1), :]
    cos_b = rope_row_ref[0]
    sin_b = rope_row_ref[1]
    cos = cos_a * cos_b - sin_a * sin_b
    sin = sin_a * cos_b + cos_a * sin_b
    lane = lax.broadcasted_iota(jnp.int32, (tm, LANES), 1)
    first_half = (lane % HEAD_DIM) < (HEAD_DIM // 2)
    lo = lane < HEAD_DIM
    sum_rows = (lax.broadcasted_iota(jnp.int32, (SUM_ROWS, tm), 0) == 0).astype(BF16)

    q = jnp.dot(h, w_ref[:, 0:aw], preferred_element_type=F32)
    k = jnp.dot(h, w_ref[:, aw:2 * aw], preferred_element_type=F32)
    v = jnp.dot(h, w_ref[:, 2 * aw:3 * aw], preferred_element_type=F32)
    for hd in range(n_heads):
        sl = slice(hd * LANES, (hd + 1) * LANES)
        qh = _rope_lanes(q[:, sl], cos, sin, first_half) * scale
        kh = _rope_lanes(k[:, sl], cos, sin, first_half)
        kht = kh.T
        qht = qh.T
        kout_ref[sl, :] = kht
        vout_ref[pl.ds(hd, tm, stride=n_heads), :] = v[:, sl]
        qt_ref[hd, 0] = qht.astype(BF16)
        for c in range(2):
            half = slice(c * HEAD_DIM, (c + 1) * HEAD_DIM)
            qn2_ref[hd, 0, c:c + 1, :] = jnp.sum(qht[half, :] * qht[half, :], axis=0, keepdims=True)
            kn2 = jnp.max(jnp.sum(kht[half, :] * kht[half, :], axis=0, keepdims=True), axis=1, keepdims=True)
            kn2_ref[0, 2 * hd + c:2 * hd + c + 1, :] = jnp.broadcast_to(kn2, (1, LANES))
        kst_ref[hd, 0, 0:tm, :] = jnp.where(lo, kh, 0.0).astype(BF16)
        kst_ref[hd, 0, tm:2 * tm, :] = jnp.where(lo, 0.0, kh).astype(BF16)
        vt_ref[hd, 0, 0:V_DIM, :] = v[:, sl].T.astype(BF16)
        vt_ref[hd, 0, V_DIM:V_DIM + SUM_ROWS, :] = sum_rows

    u1 = jnp.dot(h, w_ref[:, 3 * aw:3 * aw + cwid], preferred_element_type=F32)
    u2 = jnp.dot(h, w_ref[:, 3 * aw + cwid:3 * aw + 2 * cwid], preferred_element_type=F32)
    gpad_ref[CONV_HALO:CONV_HALO + tm, :] = u1 * jax.nn.sigmoid(u2)

    base = CONV_HALO - (CONV_K - 1)
    cb = cb_ref[...]
    lng = lng_ref[...]
    lnb = lnb_ref[...]
    n_sh = gsh_ref.shape[1]
    for b in range(1, SUBLANES):
        gsh_ref[b - 1] = gpad_ref[pl.ds(b, n_sh), :]
    for r0 in range(0, tm, row_chunk):
        acc = jnp.zeros((row_chunk, cwid), F32)
        for j in range(CONV_K):
            a, b = divmod(base + j, SUBLANES)
            rows = pl.ds(r0 + a * SUBLANES, row_chunk)
            tap = gpad_ref[rows, :] if b == 0 else gsh_ref[b - 1, rows, :]
            acc = acc + cw_ref[j:j + 1, :] * tap
        acc = acc + cb
        mu = jnp.mean(acc, axis=-1, keepdims=True)
        d = acc - mu
        var = jnp.mean(d * d, axis=-1, keepdims=True)
        y = d * lax.rsqrt(var + LN_EPS) * lng + lnb
        c_ref[r0:r0 + row_chunk, :] = (y * jax.nn.sigmoid(y)).astype(c_ref.dtype)

    tail_ref[...] = gpad_ref[tm:tm + CONV_HALO, :]
    gpad_ref[0:CONV_HALO, :] = gpad_ref[tm:tm + CONV_HALO, :]


def _inproj_prompt(x, g, w_in, rope_tile, rope_row, conv_w, conv_b, ln_g, ln_b, *, tm, aw, cwid, n_heads):
    t, d = x.shape
    nt = t // tm
    kern = functools.partial(_inproj_prompt_kernel, tm=tm, aw=aw, cwid=cwid, n_heads=n_heads,
                             scale=HEAD_DIM ** -0.5 * math.log2(math.e), row_chunk=min(tm, 32))
    row = lambda i: (i, 0)
    per_head = lambda i: (0, i, 0, 0)
    return pl.pallas_call(
        kern,
        grid=(nt,),
        in_specs=[
            pl.BlockSpec((tm, d), row),
            _const_spec((1, d)),
            _const_spec(w_in.shape),
            _const_spec(rope_tile.shape),
            _const_spec(rope_row.shape),
            _const_spec(conv_w.shape),
            _const_spec((1, cwid)),
            _const_spec((1, cwid)),
            _const_spec((1, cwid)),
        ],
        out_specs=[
            pl.BlockSpec((aw, tm), lambda i: (0, i)),
            pl.BlockSpec((tm * n_heads, V_DIM), row),
            pl.BlockSpec((n_heads, 1, LANES, tm), per_head),
            pl.BlockSpec((n_heads, 1, 2 * tm, LANES), per_head),
            pl.BlockSpec((n_heads, 1, V_DIM + SUM_ROWS, tm), per_head),
            pl.BlockSpec((tm, cwid), row),
            pl.BlockSpec((CONV_HALO, cwid), lambda i: (0, 0)),
            pl.BlockSpec((n_heads, 1, 2, tm), per_head),
            pl.BlockSpec((1, 2 * n_heads, LANES), lambda i: (i, 0, 0)),
        ],
        out_shape=(
            jax.ShapeDtypeStruct((aw, t), F32),
            jax.ShapeDtypeStruct((t * n_heads, V_DIM), F32),
            jax.ShapeDtypeStruct((n_heads, nt, LANES, tm), BF16),
            jax.ShapeDtypeStruct((n_heads, nt, 2 * tm, LANES), BF16),
            jax.ShapeDtypeStruct((n_heads, nt, V_DIM + SUM_ROWS, tm), BF16),
            jax.ShapeDtypeStruct((t, cwid), BF16),
            jax.ShapeDtypeStruct((CONV_HALO, cwid), F32),
            jax.ShapeDtypeStruct((n_heads, nt, 2, tm), F32),
            jax.ShapeDtypeStruct((nt, 2 * n_heads, LANES), F32),
        ),
        scratch_shapes=[pltpu.VMEM((tm + CONV_HALO, cwid), F32),
                        pltpu.VMEM((SUBLANES - 1, tm + CONV_HALO - SUBLANES, cwid), F32)],
        name="inproj_prompt",
        compiler_params=_params(("arbitrary",)),
    )(x, g, w_in, rope_tile, rope_row, conv_w, conv_b, ln_g, ln_b)


def _inproj_sample_kernel(x_ref, g_ref, w_ref, cos_ref, sin_ref, cc_ref, toep_ref, cb_ref, lng_ref, lnb_ref,
                          kout_ref, vout_ref, q_ref, glu_ref, c_ref, *, aw, cwid, n_heads, scale, n_b, n_t):
    rows = n_b * n_t
    h = _rms(x_ref[...], g_ref[...]).astype(BF16)
    cos = cos_ref[...]
    sin = sin_ref[...]
    lane = lax.broadcasted_iota(jnp.int32, (rows, LANES), 1)
    first_half = (lane % HEAD_DIM) < (HEAD_DIM // 2)
    q = jnp.dot(h, w_ref[:, 0:aw], preferred_element_type=F32)
    k = jnp.dot(h, w_ref[:, aw:2 * aw], preferred_element_type=F32)
    vout_ref[...] = jnp.dot(h, w_ref[:, 2 * aw:3 * aw], preferred_element_type=F32)
    for hd in range(n_heads):
        sl = slice(hd * LANES, (hd + 1) * LANES)
        q_ref[:, sl] = _rope_lanes(q[:, sl], cos, sin, first_half) * scale
        kout_ref[:, sl] = _rope_lanes(k[:, sl], cos, sin, first_half)
    u1 = jnp.dot(h, w_ref[:, 3 * aw:3 * aw + cwid], preferred_element_type=F32)
    u2 = jnp.dot(h, w_ref[:, 3 * aw + cwid:3 * aw + 2 * cwid], preferred_element_type=F32)
    glu_ref[...] = (u1 * jax.nn.sigmoid(u2)).reshape(n_b, n_t, cwid)

    acc = jnp.zeros((n_b, n_t, cwid), F32)
    for r in range(CONV_K - 1):
        acc = acc + cc_ref[:, r:r + 1, :] * toep_ref[r]
    for r in range(n_t):
        acc = acc + glu_ref[:, r:r + 1, :] * toep_ref[CONV_K - 1 + r]
    acc = acc + cb_ref[...]
    mu = jnp.mean(acc, axis=-1, keepdims=True)
    d = acc - mu
    var = jnp.mean(d * d, axis=-1, keepdims=True)
    y = d * lax.rsqrt(var + LN_EPS) * lng_ref[...] + lnb_ref[...]
    c_ref[...] = y * jax.nn.sigmoid(y)


def _inproj_sample(x, g, w_in, cos, sin, cache_conv, toep, conv_b, ln_g, ln_b, *, aw, cwid, n_heads, n_b, n_t):
    rows = n_b * n_t
    kern = functools.partial(_inproj_sample_kernel, aw=aw, cwid=cwid, n_heads=n_heads,
                             scale=HEAD_DIM ** -0.5, n_b=n_b, n_t=n_t)
    return pl.pallas_call(
        kern,
        out_shape=(
            jax.ShapeDtypeStruct((rows, aw), F32),
            jax.ShapeDtypeStruct((rows, aw), F32),
            jax.ShapeDtypeStruct((rows, aw), F32),
            jax.ShapeDtypeStruct((n_b, n_t, cwid), F32),
            jax.ShapeDtypeStruct((n_b, n_t, cwid), F32),
        ),
        name="inproj_sample",
        compiler_params=pltpu.CompilerParams(vmem_limit_bytes=VMEM_LIMIT),
    )(x, g, w_in, cos, sin, cache_conv, toep, conv_b, ln_g, ln_b)


def _masked_queries(q, n_heads, n_t):
    n_rows = n_heads * 2 * n_t
    aw = q.shape[-1]
    qrep = jnp.concatenate([q] * (n_heads * 2), axis=0)
    sel = (lax.broadcasted_iota(jnp.int32, (n_rows, aw), 0) // n_t
           == lax.broadcasted_iota(jnp.int32, (n_rows, aw), 1) // HEAD_DIM)
    return jnp.where(sel, qrep, 0.0).astype(BF16)


def _sample_init_kernel(q_ref, kn_ref, vn_ref, m_ref, l_ref, acc_ref, *, n_heads, n_t, group):
    aw = q_ref.shape[-1]
    n_rows = n_heads * 2 * n_t
    pad = jnp.zeros((LANES - n_t, aw), F32)
    col = lax.broadcasted_iota(jnp.int32, (n_rows, LANES), 1)
    qt = lax.broadcasted_iota(jnp.int32, (n_rows, LANES), 0) % n_t
    for b in range(group):
        rows = slice(b * n_t, (b + 1) * n_t)
        wq = _masked_queries(q_ref[rows, :], n_heads, n_t)
        kn = jnp.concatenate([kn_ref[rows, :], pad], axis=0).astype(BF16)
        vn = jnp.concatenate([vn_ref[rows, :], pad], axis=0).astype(BF16)
        sc = lax.dot_general(wq, kn, (((1,), (1,)), ((), ())), preferred_element_type=F32)
        sc = jnp.where(col <= qt, sc, -jnp.inf)
        m = jnp.max(sc, axis=-1, keepdims=True)
        p = jnp.exp(sc - m)
        m_ref[b] = m
        l_ref[b] = jnp.sum(p, axis=-1, keepdims=True)
        acc_ref[b] = jnp.dot(p.astype(BF16), vn, preferred_element_type=F32)


def _sample_init(q, k_new, v_new, *, n_b, n_t, n_heads):
    aw = q.shape[-1]
    n_rows = n_heads * 2 * n_t
    group = math.gcd(n_b, SUBLANES)
    tok = pl.BlockSpec((group * n_t, aw), lambda g: (g, 0))
    stat = pl.BlockSpec((group, n_rows, 1), lambda g: (g, 0, 0))
    return pl.pallas_call(
        functools.partial(_sample_init_kernel, n_heads=n_heads, n_t=n_t, group=group),
        grid=(n_b // group,),
        in_specs=[tok, tok, tok],
        out_specs=[stat, stat, pl.BlockSpec((group, n_rows, aw), lambda g: (g, 0, 0))],
        out_shape=(jax.ShapeDtypeStruct((n_b, n_rows, 1), F32), jax.ShapeDtypeStruct((n_b, n_rows, 1), F32),
                   jax.ShapeDtypeStruct((n_b, n_rows, aw), F32)),
        name="sample_init",
        compiler_params=_params(("arbitrary",)),
    )(q, k_new, v_new)


def _attn_kernel(pt_ref, th_ref, ti_ref, tt_ref, tl_ref, te_ref, tb_ref, qt_ref, kst_ref, vt_ref, bound_ref,
                 lam_ref, sgcol_ref, sgrow_ref, qs_ref, m0_ref, l0_ref, a0_ref, *rest,
                 n_pages, n_units, units_per_entry, tk, page, n_heads, n_t, lam_init):
    k_pages = rest[:n_pages]
    v_pages = rest[n_pages:2 * n_pages]
    op_ref, os_ref = rest[2 * n_pages:2 * n_pages + 2]
    (m_ref, acc_ref, q_scr, ref_scr, sa_ref, sb_ref, xa_ref, xb_ref, pend_ref,
     kbuf, vbuf, ms_ref, ls_ref, accs_ref) = rest[2 * n_pages + 2:]
    g = pl.program_id(0)
    i = ti_ref[g]
    t = tt_ref[g]
    tq = 2 * tk
    lam = _lam(lam_ref, lam_init)

    unit = jnp.minimum(g, n_units - 1)
    active = g < n_units
    first = unit % units_per_entry == 0

    def sample_scores():
        for p in range(n_pages):
            kbuf[:, p * page:(p + 1) * page] = k_pages[p][0].astype(BF16)
            for hd in range(n_heads):
                vbuf[p * page:(p + 1) * page, hd * V_DIM:(hd + 1) * V_DIM] = (
                    v_pages[p][0, pl.ds(hd, page, stride=n_heads), :].astype(BF16))
        wq = _masked_queries(qs_ref[...], n_heads, n_t)
        sc = jnp.dot(wq, kbuf[...], preferred_element_type=F32)
        return jnp.where(active, sc, -jnp.inf)

    def sample_softmax(sc):
        m_prev = jnp.where(first, m0_ref[0], ms_ref[...])
        l_prev = jnp.where(first, l0_ref[0], ls_ref[...])
        m_new = jnp.maximum(m_prev, jnp.max(sc, axis=-1, keepdims=True))
        p = jnp.exp(sc - m_new)
        alpha = jnp.exp(m_prev - m_new)
        l_new = alpha * l_prev + jnp.sum(p, axis=-1, keepdims=True)
        ms_ref[...] = m_new
        ls_ref[...] = l_new
        return p.astype(BF16), alpha, l_new

    def sample_values(p, alpha, l_new):
        acc = (alpha * jnp.where(first, a0_ref[0], accs_ref[...])
               + jnp.dot(p, vbuf[...], preferred_element_type=F32))
        accs_ref[...] = acc
        o = acc / l_new
        for hd in range(n_heads):
            r0 = hd * 2 * n_t
            sl = slice(hd * V_DIM, (hd + 1) * V_DIM)
            a = o[r0:r0 + n_t, sl] - lam * o[r0 + n_t:r0 + 2 * n_t, sl]
            os_ref[:, sl] = _rms(a, sgrow_ref[...]) * (1.0 - lam_init)

    def scores(j, s_ref, smax_ref, key_offset=None):
        st = jnp.dot(kst_ref[0, j], q_scr[...], preferred_element_type=F32)
        if key_offset is not None:
            keep = (lax.broadcasted_iota(jnp.int32, (tk, tq), 0) + key_offset
                    <= lax.broadcasted_iota(jnp.int32, (tk, tq), 1))
        for c in range(2):
            sc = st[c * tk:(c + 1) * tk, :]
            if key_offset is not None:
                sc = jnp.where(keep, sc, -jnp.inf)
            s_ref[c * tk:(c + 1) * tk, :] = sc
            smax_ref[c] = jnp.max(sc, axis=0, keepdims=True)

    def absorb(j, s_ref, smax_ref):
        vt = vt_ref[0, j]
        for c in range(2):
            sc = s_ref[c * tk:(c + 1) * tk, :]
            m_prev = m_ref[c]
            m_new = jnp.maximum(m_prev, smax_ref[c])
            p = jnp.exp2(sc - m_new).astype(BF16)
            alpha = jnp.exp2(m_prev - m_new)
            acc_ref[c] = alpha * acc_ref[c] + jnp.dot(vt, p, preferred_element_type=F32)
            m_ref[c] = m_new

    def bounded_block(j, key_offset=None):
        st = jnp.dot(kst_ref[0, j], q_scr[...], preferred_element_type=F32)
        vt = vt_ref[0, j]
        if key_offset is not None:
            keep = (lax.broadcasted_iota(jnp.int32, (tk, tq), 0) + key_offset
                    <= lax.broadcasted_iota(jnp.int32, (tk, tq), 1))
        for c in range(2):
            p = jnp.exp2(st[c * tk:(c + 1) * tk, :] - ref_scr[c])
            if key_offset is not None:
                p = jnp.where(keep, p, 0.0)
            acc_ref[c] += jnp.dot(vt, p.astype(BF16), preferred_element_type=F32)

    def start_query_block():
        q_scr[:, 0:tk] = qt_ref[0, 0]
        q_scr[:, tk:tq] = qt_ref[0, 1]
        acc_ref[...] = jnp.zeros(acc_ref.shape, F32)

    bounded = tb_ref[g] == 1
    diag = t < 0

    @pl.when(jnp.logical_and(diag, bounded))
    def _():
        start_query_block()
        for c in range(2):
            ref_scr[c, :, 0:tk] = bound_ref[0, 0, c:c + 1, :]
            ref_scr[c, :, tk:tq] = bound_ref[0, 1, c:c + 1, :]
        sc = sample_scores()
        bounded_block(2 * i, key_offset=0)
        soft = sample_softmax(sc)
        bounded_block(2 * i + 1, key_offset=tk)
        sample_values(*soft)

    @pl.when(jnp.logical_and(jnp.logical_not(diag), bounded))
    def _():
        sc = sample_scores()
        bounded_block(2 * t)
        soft = sample_softmax(sc)
        bounded_block(2 * t + 1)
        sample_values(*soft)

    @pl.when(jnp.logical_and(diag, jnp.logical_not(bounded)))
    def _():
        start_query_block()
        m_ref[...] = jnp.full(m_ref.shape, -jnp.inf, F32)
        sc = sample_scores()
        scores(2 * i, sa_ref, xa_ref, key_offset=0)
        soft = sample_softmax(sc)
        scores(2 * i + 1, sb_ref, xb_ref, key_offset=tk)
        sample_values(*soft)
        absorb(2 * i, sa_ref, xa_ref)
        pend_ref[0] = 2 * i + 1

    @pl.when(jnp.logical_and(jnp.logical_not(diag), jnp.logical_not(bounded)))
    def _():
        pending = pend_ref[0]
        sc = sample_scores()
        scores(2 * t, sa_ref, xa_ref)
        soft = sample_softmax(sc)
        absorb(pending, sb_ref, xb_ref)
        sample_values(*soft)
        scores(2 * t + 1, sb_ref, xb_ref)
        absorb(2 * t, sa_ref, xa_ref)
        pend_ref[0] = 2 * t + 1

    last = tl_ref[g] == 1

    @pl.when(jnp.logical_and(last, jnp.logical_not(bounded)))
    def _():
        absorb(pend_ref[0], sb_ref, xb_ref)

    @pl.when(last)
    def _():
        o1 =acc_ref[0, 0:V_DIM, :] / acc_ref[0, V_DIM:V_DIM + 1, :]
        o2 = acc_ref[1, 0:V_DIM, :] / acc_ref[1, V_DIM:V_DIM + 1, :]
        a = o1 - lam * o2
        y = a * lax.rsqrt(jnp.mean(a * a, axis=0, keepdims=True) + RMS_EPS) * sgcol_ref[...] * (1.0 - lam_init)
        op_ref[...] = y.T.astype(op_ref.dtype)


def _prompt_tasks(n_heads, nq):
    tasks = [(h, i, t, int(t == i - 1)) for h in range(n_heads) for i in range(nq) for t in range(-1, i)]
    return tuple(np.asarray(col, np.int32) for col in zip(*tasks))


def _attention(page_table, qt, kst, vt, qn2, kn2, lam_vecs, subln_g, q_s, k_new, v_new, cache_kt, cache_v,
               *, n_b, n_t, lam_init):
    n_heads, nt, _, tk = qt.shape
    assert nt % 2 == 0, "a query block spans two key blocks"
    nq = nt // 2
    tq = 2 * tk
    t = nt * tk
    n_pool, aw, page = cache_kt.shape
    n_rows = n_heads * 2 * n_t
    th, ti, tt, tl = _prompt_tasks(n_heads, nq)
    n_tasks = len(th)
    kmax2 = jnp.max(kn2[:, :, 0], axis=0).reshape(n_heads, 1, 2, 1)
    bound = jnp.sqrt(qn2 * kmax2)
    block_bound = jnp.max(bound.reshape(n_heads, nq, -1), axis=-1)
    tb = (block_bound <= SAFE_BOUND).astype(jnp.int32)[th, ti]
    n_pages_entry = page_table.shape[1]
    pages_per_unit = min(p for p in range(1, n_pages_entry + 1)
                         if n_pages_entry % p == 0 and n_b * (n_pages_entry // p) <= n_tasks)
    units_per_entry = n_pages_entry // pages_per_unit
    n_units = n_b * units_per_entry
    m0, l0, a0 = _sample_init(q_s, k_new, v_new, n_b=n_b, n_t=n_t, n_heads=n_heads)
    kern = functools.partial(_attn_kernel, n_pages=pages_per_unit, n_units=n_units,
                             units_per_entry=units_per_entry, tk=tk, page=page, n_heads=n_heads, n_t=n_t,
                             lam_init=lam_init)

    unit_of_step = np.minimum(np.arange(n_tasks), n_units - 1)
    te = (unit_of_step // units_per_entry).astype(np.int32)
    step_pages = page_table.reshape(n_units, pages_per_unit)[unit_of_step].reshape(-1)

    tok = pl.BlockSpec((n_t, aw), lambda g, pt, th, ti, tt, tl, te, tb: (te[g], 0))
    stat = pl.BlockSpec((1, n_rows, 1), lambda g, pt, th, ti, tt, tl, te, tb: (te[g], 0, 0))

    def page_spec(p, shape):
        return pl.BlockSpec((1,) + shape, lambda g, pt, *_: (pt[g * pages_per_unit + p], 0, 0))

    k_specs = [page_spec(p, (aw, page)) for p in range(pages_per_unit)]
    v_specs = [page_spec(p, (page * n_heads, V_DIM)) for p in range(pages_per_unit)]
    head_map = lambda g, pt, th, *_: (th[g], 0, 0, 0)
    grid_spec = pltpu.PrefetchScalarGridSpec(
        num_scalar_prefetch=7,
        grid=(n_tasks,),
        in_specs=[
            pl.BlockSpec((1, 2, LANES, tk), lambda g, pt, th, ti, *_: (th[g], ti[g], 0, 0)),
            pl.BlockSpec((1, nt, 2 * tk, LANES), head_map, pipeline_mode=pl.Buffered(1)),
            pl.BlockSpec((1, nt, V_DIM + SUM_ROWS, tk), head_map, pipeline_mode=pl.Buffered(1)),
            pl.BlockSpec((1, 2, 2, tk), lambda g, pt, th, ti, *_: (th[g], ti[g], 0, 0)),
            pl.BlockSpec(lam_vecs.shape, lambda g, *_: (0, 0)),
            pl.BlockSpec((V_DIM, 1), lambda g, *_: (0, 0)),
            pl.BlockSpec((1, V_DIM), lambda g, *_: (0, 0)),
            tok, stat, stat,
            pl.BlockSpec((1, n_rows, aw), lambda g, pt, th, ti, tt, tl, te, tb: (te[g], 0, 0)),
        ] + k_specs + v_specs,
        out_specs=[
            pl.BlockSpec((tq, V_DIM), lambda g, pt, th, ti, *_: (ti[g], th[g])),
            tok,
        ],
        scratch_shapes=[
            pltpu.VMEM((2, 1, tq), F32),
            pltpu.VMEM((2, V_DIM + SUM_ROWS, tq), F32),
            pltpu.VMEM((LANES, tq), BF16),
            pltpu.VMEM((2, 1, tq), F32),
            pltpu.VMEM((2 * tk, tq), F32),
            pltpu.VMEM((2 * tk, tq), F32),
            pltpu.VMEM((2, 1, tq), F32),
            pltpu.VMEM((2, 1, tq), F32),
            pltpu.SMEM((1,), jnp.int32),
            pltpu.VMEM((aw, pages_per_unit * page), BF16),
            pltpu.VMEM((pages_per_unit * page, aw), BF16),
            pltpu.VMEM((n_rows, 1), F32),
            pltpu.VMEM((n_rows, 1), F32),
            pltpu.VMEM((n_rows, aw), F32),
        ],
    )
    return pl.pallas_call(
        kern,
        grid_spec=grid_spec,
        out_shape=(jax.ShapeDtypeStruct((t, n_heads * V_DIM), BF16),
                   jax.ShapeDtypeStruct((n_b * n_t, aw), F32)),
        name="attention",
        compiler_params=_params(("arbitrary",)),
    )(step_pages, th, ti, tt, tl, te, tb, qt, kst, vt, bound, lam_vecs, subln_g.reshape(-1, 1),
      subln_g.reshape(1, -1), q_s, m0, l0, a0, *([cache_kt] * pages_per_unit), *([cache_v] * pages_per_unit))


def _cross_kernel(o_ref, c_ref, x_ref, wout_ref, gpost_ref, gpre_ref, wmq_ref, mk_ref, mv_ref, wmo_ref,
                  gcpost_ref, y_ref, *, mem_heads, row_splits):
    tm, d = x_ref.shape
    hd = d // mem_heads
    n_lt = hd // LANES
    n_mem = mk_ref.shape[1] // (mem_heads * n_lt)

    def head_rows(ref, h):
        return jnp.concatenate(
            [ref[0, pl.ds(lt * mem_heads + h, n_mem, stride=mem_heads * n_lt), :] for lt in range(n_lt)],
            axis=-1).astype(BF16)

    mks = [head_rows(mk_ref, h) for h in range(mem_heads)]
    mvs = [head_rows(mv_ref, h) for h in range(mem_heads)]
    rs = tm // row_splits
    groups = [slice(r * rs, (r + 1) * rs) for r in range(row_splits)]
    scale = hd ** -0.5
    mix = [jnp.dot(jnp.concatenate([o_ref[g, :].astype(BF16), c_ref[g, :].astype(BF16)], axis=-1),
                   wout_ref[...], preferred_element_type=F32) for g in groups]
    x1 = [x_ref[g, :] + _rms(m, gpost_ref[...]) for g, m in zip(groups, mix)]
    q = [(jnp.dot(_rms(x, gpre_ref[...]).astype(BF16), wmq_ref[...], preferred_element_type=F32)
          * scale).astype(BF16) for x in x1]
    outs = [[] for _ in groups]
    for h in range(mem_heads):
        sl = slice(h * hd, (h + 1) * hd)
        s = [lax.dot_general(qr[:, sl], mks[h], (((1,), (1,)), ((), ())), preferred_element_type=F32)
             for qr in q]
        for r, sr in enumerate(s):
            p = jnp.exp(sr - jnp.max(sr, axis=-1, keepdims=True))
            p = p / jnp.sum(p, axis=-1, keepdims=True)
            outs[r].append(jnp.dot(p.astype(BF16), mvs[h], preferred_element_type=F32).astype(BF16))
    att = [jnp.dot(jnp.concatenate(o, axis=-1), wmo_ref[...], preferred_element_type=F32) for o in outs]
    for g, x, a in zip(groups, x1, att):
        y_ref[g, :] = x + _rms(a, gcpost_ref[...])


def _cross(o, c, x, w_out, g_post, g_pre, w_mq, mem_k, mem_v, w_mo, g_cpost, *, tm, mem_heads):
    t, d = x.shape
    aw = o.shape[1]
    cwid = c.shape[1]
    mem_rows, mem_hd = mem_k.shape[1:]
    row = lambda i: (i, 0)
    return pl.pallas_call(
        functools.partial(_cross_kernel, mem_heads=mem_heads, row_splits=1),
        grid=(t // tm,),
        in_specs=[
            pl.BlockSpec((tm, aw), row),
            pl.BlockSpec((tm, cwid), row),
            pl.BlockSpec((tm, d), row),
            _const_spec(w_out.shape),
            _const_spec((1, d)),
            _const_spec((1, d)),
            _const_spec(w_mq.shape),
            _const_spec((1, mem_rows, mem_hd)),
            _const_spec((1, mem_rows, mem_hd)),
            _const_spec(w_mo.shape),
            _const_spec((1, d)),
        ],
        out_specs=pl.BlockSpec((tm, d), row),
        out_shape=jax.ShapeDtypeStruct((t, d), F32),
        name="cross_prompt",
        compiler_params=_params(("arbitrary",)),
    )(o, c, x, w_out, g_post, g_pre, w_mq, mem_k, mem_v, w_mo, g_cpost)


def _cross_sample_kernel(o_ref, c_ref, x_ref, wout_ref, gpost_ref, gpre_ref, wmq_ref, mk_ref, mv_ref, wmo_ref,
                         gcpost_ref, y_ref, x1_scr, q_scr, att_scr, *, mem_heads, n_t):
    b = pl.program_id(0)
    d = x_ref.shape[-1]
    hd = d // mem_heads
    n_lt = hd // LANES
    n_mem = mk_ref.shape[1] // (mem_heads * n_lt)

    @pl.when(b == 0)
    def _():
        mix_in = jnp.concatenate([o_ref[...].astype(BF16), c_ref[...].astype(BF16)], axis=-1)
        mix = jnp.dot(mix_in, wout_ref[...], preferred_element_type=F32)
        x1 = x_ref[...] + _rms(mix, gpost_ref[...])
        x1_scr[...] = x1
        hq = _rms(x1, gpre_ref[...]).astype(BF16)
        q_scr[...] = jnp.dot(hq, wmq_ref[...], preferred_element_type=F32) * (hd ** -0.5)

    def head_rows(ref, h):
        return jnp.concatenate(
            [ref[0, pl.ds(lt * mem_heads + h, n_mem, stride=mem_heads * n_lt), :] for lt in range(n_lt)],
            axis=-1).astype(BF16)

    rows = pl.ds(pl.multiple_of(b * n_t, n_t), n_t)
    q = q_scr[rows, :].astype(BF16)
    for h in range(mem_heads):
        sl = slice(h * hd, (h + 1) * hd)
        s = lax.dot_general(q[:, sl], head_rows(mk_ref, h), (((1,), (1,)), ((), ())),
                            preferred_element_type=F32)
        p = jnp.exp(s - jnp.max(s, axis=-1, keepdims=True))
        p = p / jnp.sum(p, axis=-1, keepdims=True)
        att_scr[rows, sl] = jnp.dot(p.astype(BF16), head_rows(mv_ref, h), preferred_element_type=F32)

    @pl.when(b == pl.num_programs(0) - 1)
    def _():
        att = jnp.dot(att_scr[...].astype(BF16), wmo_ref[...], preferred_element_type=F32)
        y_ref[...] = x1_scr[...] + _rms(att, gcpost_ref[...])


def _cross_sample(o, c, x, w_out, g_post, g_pre, w_mq, mem_k, mem_v, w_mo, g_cpost, *, n_t, mem_heads):
    t, d = x.shape
    n_b, mem_rows, mem_hd = mem_k.shape
    assert n_t == SUBLANES and t == n_b * n_t
    mem_map = lambda b: (b, 0, 0)
    return pl.pallas_call(
        functools.partial(_cross_sample_kernel, mem_heads=mem_heads, n_t=n_t),
        grid=(n_b,),
        in_specs=[
            _const_spec(o.shape),
            _const_spec(c.shape),
            _const_spec(x.shape),
            _const_spec(w_out.shape),
            _const_spec((1, d)),
            _const_spec((1, d)),
            _const_spec(w_mq.shape),
            pl.BlockSpec((1, mem_rows, mem_hd), mem_map),
            pl.BlockSpec((1, mem_rows, mem_hd), mem_map),
            _const_spec(w_mo.shape),
            _const_spec((1, d)),
        ],
        out_specs=pl.BlockSpec((t, d), lambda b: (0, 0)),
        out_shape=jax.ShapeDtypeStruct((t, d), F32),
        scratch_shapes=[pltpu.VMEM((t, d), F32), pltpu.VMEM((t, d), F32), pltpu.VMEM((t, d), F32)],
        name="cross_sample",
        compiler_params=_params(("arbitrary",)),
    )(o, c, x, w_out, g_post, g_pre, w_mq, mem_k, mem_v, w_mo, g_cpost)


def _mlp_kernel(x_ref, gpre_ref, w1_ref, w2_ref, gpost_ref, y_ref):
    x = x_ref[...]
    h = _rms(x, gpre_ref[...]).astype(BF16)
    hf = jnp.dot(h, w1_ref[...], preferred_element_type=F32)
    r = jnp.maximum(hf, 0.0)
    f = jnp.dot((r * r).astype(BF16), w2_ref[...], preferred_element_type=F32)
    y_ref[...] = x + _rms(f, gpost_ref[...])


def _mlp(x, g_pre, w1, w2, g_post, *, tm):
    t, d = x.shape
    row = lambda i: (i, 0)
    return pl.pallas_call(
        _mlp_kernel,
        grid=(t // tm,),
        in_specs=[pl.BlockSpec((tm, d), row), _const_spec((1, d)), _const_spec(w1.shape),
                  _const_spec(w2.shape), _const_spec((1, d))],
        out_specs=pl.BlockSpec((tm, d), row),
        out_shape=jax.ShapeDtypeStruct((t, d), F32),
        name="mlp_tile%d" % tm,
        compiler_params=_params(("arbitrary",)),
    )(x, g_pre, w1, w2, g_post)


def _rope_tables(pos):
    half = HEAD_DIM // 2
    inv = ROPE_THETA ** (-jnp.arange(0, half, dtype=F32) * 2.0 / HEAD_DIM)
    ang = pos.astype(F32)[:, None] * inv[None, :]
    cos = jnp.cos(ang)
    sin = jnp.sin(ang)
    reps = LANES // HEAD_DIM
    return (jnp.tile(jnp.concatenate([cos, cos], axis=-1), (1, reps)),
            jnp.tile(jnp.concatenate([-sin, sin], axis=-1), (1, reps)))


def _conv_toeplitz(conv_w, n_t):
    n_r = CONV_K - 1 + n_t
    r = jnp.arange(n_r)[:, None]
    t = jnp.arange(n_t)[None, :]
    j = r - t
    ok = (j >= 0) & (j < CONV_K)
    return jnp.where(ok[:, :, None], conv_w[jnp.clip(j, 0, CONV_K - 1)], 0.0)


def _layer(i, depth_inputs, x_prompt, x_sample, mem_prompt, cache_k, cache_v, cache_conv, cache_mem_k,
           cache_mem_v, page_table, *, tile):
    (w_in, lq1, lk1, lq2, lk2, subln_g, conv_w, conv_b, ln_g, ln_b, w_out, mix_pre_g, mix_post_g, mem_norm_g,
     w_mq, w_mk, w_mv, w_mo, cross_pre_g, cross_post_g, w_ff1, w_ff2, mlp_pre_g, mlp_post_g) = depth_inputs
    lam_init = 0.8 - 0.6 * math.exp(-0.3 * i)
    _, seq, d = x_prompt.shape
    n_b, n_t, _ = x_sample.shape
    n_heads, v_dim = cache_v.shape[-2:]
    aw = n_heads * v_dim
    cwid = cache_conv.shape[-1]
    mem_heads = cache_mem_k.shape[-2]
    n_mem = mem_prompt.shape[1]
    past = page_table.shape[1] * cache_k.shape[1]

    row2 = lambda a: a.reshape(1, -1)
    w_in_b, w_out_b, w_mq_b, w_mk_b, w_mv_b, w_mo_b, w_ff1_b, w_ff2_b = (
        w.astype(BF16) for w in (w_in, w_out, w_mq, w_mk, w_mv, w_mo, w_ff1, w_ff2))
    lam_vecs = jnp.stack([lq1, lk1, lq2, lk2])

    mk_p, mv_p = _memory_kv(mem_prompt[0], row2(mem_norm_g), w_mk_b, w_mv_b, mem_heads=mem_heads)
    rope_tile = jnp.stack(_rope_tables(jnp.arange(0, seq, tile, dtype=jnp.int32)))
    rope_row = jnp.stack(_rope_tables(jnp.arange(tile, dtype=jnp.int32)))
    k_p, v_p, qt, kst, vt, c_p, tail_p, qn2, kn2 = _inproj_prompt(
        x_prompt[0], row2(mix_pre_g), w_in_b, rope_tile, rope_row, conv_w, row2(conv_b), row2(ln_g), row2(ln_b),
        tm=tile, aw=aw, cwid=cwid, n_heads=n_heads)

    pos_s = jnp.tile(past + jnp.arange(n_t, dtype=jnp.int32), n_b)
    cos_s, sin_s = _rope_tables(pos_s)
    xs = x_sample.reshape(n_b * n_t, d)
    k_s, v_s, q_s, glu_s, c_s = _inproj_sample(
        xs, row2(mix_pre_g), w_in_b, cos_s, sin_s, cache_conv, _conv_toeplitz(conv_w, n_t), row2(conv_b),
        row2(ln_g), row2(ln_b), aw=aw, cwid=cwid, n_heads=n_heads, n_b=n_b, n_t=n_t)
    n_pool, page = cache_k.shape[:2]
    cache_kt = jnp.transpose(cache_k, (0, 2, 3, 4, 1)).reshape(n_pool, aw, page)
    cache_vr = cache_v.reshape(n_pool, page * n_heads, v_dim)
    o_p, o_s = _attention(page_table, qt, kst, vt, qn2, kn2, lam_vecs, subln_g, q_s, k_s, v_s, cache_kt, cache_vr,
                          n_b=n_b, n_t=n_t, lam_init=lam_init)
    x2_p = _cross(o_p, c_p, x_prompt[0], w_out_b, row2(mix_post_g), row2(cross_pre_g), w_mq_b,
                  mk_p[None], mv_p[None], w_mo_b, row2(cross_post_g), tm=tile, mem_heads=mem_heads)
    y_p = _mlp(x2_p, row2(mlp_pre_g), w_ff1_b, w_ff2_b, row2(mlp_post_g), tm=tile)
    n_lt = d // mem_heads // LANES

    def mem_rows(a):
        a = a.reshape(a.shape[0], n_mem, mem_heads, n_lt, LANES)
        return jnp.transpose(a, (0, 1, 3, 2, 4)).reshape(a.shape[0], n_mem * n_lt * mem_heads, LANES)

    x2_s = _cross_sample(o_s, c_s.reshape(n_b * n_t, cwid), xs, w_out_b, row2(mix_post_g), row2(cross_pre_g),
                         w_mq_b, mem_rows(cache_mem_k), mem_rows(cache_mem_v), w_mo_b,
                         row2(cross_post_g), n_t=n_t, mem_heads=mem_heads)

    def mem_out(a):
        a = a.reshape(n_mem, n_lt, mem_heads, LANES)
        return jnp.transpose(a, (0, 2, 1, 3)).reshape(1, n_mem, mem_heads, n_lt * LANES)
    y_s = _mlp(x2_s, row2(mlp_pre_g), w_ff1_b, w_ff2_b, row2(mlp_post_g), tm=n_b * n_t)

    hd2 = (n_heads, 2, HEAD_DIM)
    outs = dict(
        y_p=y_p[None], y_s=y_s.reshape(n_b, n_t, d),
        k_p=jnp.transpose(k_p.reshape(*hd2, seq), (3, 0, 1, 2))[None],
        v_p=v_p.reshape(1, seq, n_heads, v_dim),
        conv_p=tail_p[CONV_HALO - (CONV_K - 1):][None],
        mk_p=mem_out(mk_p), mv_p=mem_out(mv_p),
        k_s=k_s.reshape(n_b, n_t, *hd2), v_s=v_s.reshape(n_b, n_t, n_heads, v_dim),
        conv_s=jnp.concatenate([cache_conv[:, n_t:], glu_s], axis=1),
    )
    return outs


def kernel(x_prompt, x_sample, mem_prompt, cache_k, cache_v, cache_conv, cache_mem_k, cache_mem_v, page_table, w_in, lambda_q1, lambda_k1, lambda_q2, lambda_k2, subln_g, conv_w, conv_b, conv_ln_g, conv_ln_b, w_out, mix_pre_g, mix_post_g, mem_norm_g, w_mq, w_mk, w_mv, w_mo, cross_pre_g, cross_post_g, w_ff1, w_ff2, mlp_pre_g, mlp_post_g):
    depth = w_in.shape[0]
    stacked = (w_in, lambda_q1, lambda_k1, lambda_q2, lambda_k2, subln_g, conv_w, conv_b, conv_ln_g, conv_ln_b,
               w_out, mix_pre_g, mix_post_g, mem_norm_g, w_mq, w_mk, w_mv, w_mo, cross_pre_g, cross_post_g,
               w_ff1, w_ff2, mlp_pre_g, mlp_post_g)
    tile = min(512, x_prompt.shape[1])
    y_p, y_s = x_prompt, x_sample
    per_layer = []
    for i in range(depth):
        o = _layer(i, tuple(a[i] for a in stacked), y_p, y_s, mem_prompt, cache_k[i], cache_v[i], cache_conv[i],
                   cache_mem_k[i], cache_mem_v[i], page_table, tile=tile)
        y_p, y_s = o["y_p"], o["y_s"]
        per_layer.append(o)
    stack = lambda name: jnp.stack([o[name] for o in per_layer], axis=0)
    return (y_p, y_s, stack("k_p"), stack("v_p"), stack("conv_p"), stack("mk_p"), stack("mv_p"),
            stack("k_s"), stack("v_s"), stack("conv_s"))
```

```python
import functools
import math

import jax
import numpy as np
import jax.numpy as jnp
from jax import lax
from jax.experimental import pallas as pl
from jax.experimental.pallas import tpu as pltpu

F32 = jnp.float32
BF16 = jnp.bfloat16

RMS_EPS = 1e-6
LN_EPS = 1e-5
ROPE_THETA = 10000.0
LANES = 128
SUBLANES = 8
HEAD_DIM = 64
V_DIM = 2 * HEAD_DIM
CONV_K = 31
CONV_HALO = 32
SUM_ROWS = 16
SAFE_BOUND = 60.0
VMEM_LIMIT = 56 * 1024 * 1024


def _rms(x, g):
    return x * lax.rsqrt(jnp.mean(x * x, axis=-1, keepdims=True) + RMS_EPS) * g


def _lam(lam_ref, lam_init):
    a = jnp.sum(lam_ref[0:1, :] * lam_ref[1:2, :], axis=-1, keepdims=True)
    b = jnp.sum(lam_ref[2:3, :] * lam_ref[3:4, :], axis=-1, keepdims=True)
    return jnp.exp(a) - jnp.exp(b) + lam_init


def _rope_lanes(x, cos, sin_signed, first_half):
    rot = jnp.where(first_half, pltpu.roll(x, LANES - HEAD_DIM // 2, 1), pltpu.roll(x, HEAD_DIM // 2, 1))
    return x * cos + rot * sin_signed


def _const_spec(shape):
    nd = len(shape)
    return pl.BlockSpec(shape, lambda *_: (0,) * nd, pipeline_mode=pl.Buffered(1))


def _params(sem):
    return pltpu.CompilerParams(dimension_semantics=sem, vmem_limit_bytes=VMEM_LIMIT)


def _memory_kv_kernel(mem_ref, g_ref, wk_ref, wv_ref, mk_ref, mv_ref, *, mem_heads):
    m = _rms(mem_ref[...], g_ref[...]).astype(BF16)
    n, d = mem_ref.shape
    hd = d // mem_heads
    mk = jnp.dot(m, wk_ref[...], preferred_element_type=F32)
    mv = jnp.dot(m, wv_ref[...], preferred_element_type=F32)
    n_lt = hd // LANES
    for h in range(mem_heads):
        for lt in range(n_lt):
            rows = pl.ds(lt * mem_heads + h, n, stride=mem_heads * n_lt)
            col = h * hd + lt * LANES
            mk_ref[rows, :] = mk[:, col:col + LANES]
            mv_ref[rows, :] = mv[:, col:col + LANES]


def _memory_kv(mem, g, wk, wv, *, mem_heads):
    n, d = mem.shape
    out = jax.ShapeDtypeStruct((n * d // LANES, LANES), F32)
    return pl.pallas_call(
        functools.partial(_memory_kv_kernel, mem_heads=mem_heads),
        out_shape=(out, out),
        name="memory_kv",
        compiler_params=pltpu.CompilerParams(vmem_limit_bytes=VMEM_LIMIT),
    )(mem, g, wk, wv)


def _inproj_prompt_kernel(x_ref, g_ref, w_ref, rope_tile_ref, rope_row_ref, cw_ref, cb_ref, lng_ref, lnb_ref,
                          kout_ref, vout_ref, qt_ref, kst_ref, vt_ref, c_ref, tail_ref, qn2_ref, kn2_ref,
                          gpad_ref, gsh_ref, *, tm, aw, cwid, n_heads, scale, row_chunk):
    i = pl.program_id(0)

    @pl.when(i == 0)
    def _():
        gpad_ref[0:CONV_HALO, :] = jnp.zeros((CONV_HALO, cwid), F32)

    h = _rms(x_ref[...], g_ref[...]).astype(BF16)
    cos_a = rope_tile_ref[0, pl.ds(i, 1), :]
    sin_a = rope_tile_ref[1, pl.ds(i, 1), :]
    cos_b = rope_row_ref[0]
    sin_b = rope_row_ref[1]
    cos = cos_a * cos_b - sin_a * sin_b
    sin = sin_a * cos_b + cos_a * sin_b
    lane = lax.broadcasted_iota(jnp.int32, (tm, LANES), 1)
    first_half = (lane % HEAD_DIM) < (HEAD_DIM // 2)
    lo = lane < HEAD_DIM
    sum_rows = (lax.broadcasted_iota(jnp.int32, (SUM_ROWS, tm), 0) == 0).astype(BF16)

    q = jnp.dot(h, w_ref[:, 0:aw], preferred_element_type=F32)
    k = jnp.dot(h, w_ref[:, aw:2 * aw], preferred_element_type=F32)
    v = jnp.dot(h, w_ref[:, 2 * aw:3 * aw], preferred_element_type=F32)
    for hd in range(n_heads):
        sl = slice(hd * LANES, (hd + 1) * LANES)
        qh = _rope_lanes(q[:, sl], cos, sin, first_half) * scale
        kh = _rope_lanes(k[:, sl], cos, sin, first_half)
        kht = kh.T
        qht = qh.T
        kout_ref[sl, :] = kht
        vout_ref[pl.ds(hd, tm, stride=n_heads), :] = v[:, sl]
        qt_ref[hd, 0] = qht.astype(BF16)
        for c in range(2):
            half = slice(c * HEAD_DIM, (c + 1) * HEAD_DIM)
            qn2_ref[hd, 0, c:c + 1, :] = jnp.sum(qht[half, :] * qht[half, :], axis=0, keepdims=True)
            kn2 = jnp.max(jnp.sum(kht[half, :] * kht[half, :], axis=0, keepdims=True), axis=1, keepdims=True)
            kn2_ref[0, 2 * hd + c:2 * hd + c + 1, :] = jnp.broadcast_to(kn2, (1, LANES))
        kst_ref[hd, 0, 0:tm, :] = jnp.where(lo, kh, 0.0).astype(BF16)
        kst_ref[hd, 0, tm:2 * tm, :] = jnp.where(lo, 0.0, kh).astype(BF16)
        vt_ref[hd, 0, 0:V_DIM, :] = v[:, sl].T.astype(BF16)
        vt_ref[hd, 0, V_DIM:V_DIM + SUM_ROWS, :] = sum_rows

    u1 = jnp.dot(h, w_ref[:, 3 * aw:3 * aw + cwid], preferred_element_type=F32)
    u2 = jnp.dot(h, w_ref[:, 3 * aw + cwid:3 * aw + 2 * cwid], preferred_element_type=F32)
    gpad_ref[CONV_HALO:CONV_HALO + tm, :] = u1 * jax.nn.sigmoid(u2)

    base = CONV_HALO - (CONV_K - 1)
    cb = cb_ref[...]
    lng = lng_ref[...]
    lnb = lnb_ref[...]
    n_sh = gsh_ref.shape[1]
    for b in range(1, SUBLANES):
        gsh_ref[b - 1] = gpad_ref[pl.ds(b, n_sh), :]
    for r0 in range(0, tm, row_chunk):
        acc = jnp.zeros((row_chunk, cwid), F32)
        for j in range(CONV_K):
            a, b = divmod(base + j, SUBLANES)
            rows = pl.ds(r0 + a * SUBLANES, row_chunk)
            tap = gpad_ref[rows, :] if b == 0 else gsh_ref[b - 1, rows, :]
            acc = acc + cw_ref[j:j + 1, :] * tap
        acc = acc + cb
        mu = jnp.mean(acc, axis=-1, keepdims=True)
        d = acc - mu
        var = jnp.mean(d * d, axis=-1, keepdims=True)
        y = d * lax.rsqrt(var + LN_EPS) * lng + lnb
        c_ref[r0:r0 + row_chunk, :] = (y * jax.nn.sigmoid(y)).astype(c_ref.dtype)

    tail_ref[...] = gpad_ref[tm:tm + CONV_HALO, :]
    gpad_ref[0:CONV_HALO, :] = gpad_ref[tm:tm + CONV_HALO, :]


def _inproj_prompt(x, g, w_in, rope_tile, rope_row, conv_w, conv_b, ln_g, ln_b, *, tm, aw, cwid, n_heads):
    t, d = x.shape
    nt = t // tm
    kern = functools.partial(_inproj_prompt_kernel, tm=tm, aw=aw, cwid=cwid, n_heads=n_heads,
                             scale=HEAD_DIM ** -0.5 * math.log2(math.e), row_chunk=min(tm, 32))
    row = lambda i: (i, 0)
    per_head = lambda i: (0, i, 0, 0)
    return pl.pallas_call(
        kern,
        grid=(nt,),
        in_specs=[
            pl.BlockSpec((tm, d), row),
            _const_spec((1, d)),
            _const_spec(w_in.shape),
            _const_spec(rope_tile.shape),
            _const_spec(rope_row.shape),
            _const_spec(conv_w.shape),
            _const_spec((1, cwid)),
            _const_spec((1, cwid)),
            _const_spec((1, cwid)),
        ],
        out_specs=[
            pl.BlockSpec((aw, tm), lambda i: (0, i)),
            pl.BlockSpec((tm * n_heads, V_DIM), row),
            pl.BlockSpec((n_heads, 1, LANES, tm), per_head),
            pl.BlockSpec((n_heads, 1, 2 * tm, LANES), per_head),
            pl.BlockSpec((n_heads, 1, V_DIM + SUM_ROWS, tm), per_head),
            pl.BlockSpec((tm, cwid), row),
            pl.BlockSpec((CONV_HALO, cwid), lambda i: (0, 0)),
            pl.BlockSpec((n_heads, 1, 2, tm), per_head),
            pl.BlockSpec((1, 2 * n_heads, LANES), lambda i: (i, 0, 0)),
        ],
        out_shape=(
            jax.ShapeDtypeStruct((aw, t), F32),
            jax.ShapeDtypeStruct((t * n_heads, V_DIM), F32),
            jax.ShapeDtypeStruct((n_heads, nt, LANES, tm), BF16),
            jax.ShapeDtypeStruct((n_heads, nt, 2 * tm, LANES), BF16),
            jax.ShapeDtypeStruct((n_heads, nt, V_DIM + SUM_ROWS, tm), BF16),
            jax.ShapeDtypeStruct((t, cwid), BF16),
            jax.ShapeDtypeStruct((CONV_HALO, cwid), F32),
            jax.ShapeDtypeStruct((n_heads, nt, 2, tm), F32),
            jax.ShapeDtypeStruct((nt, 2 * n_heads, LANES), F32),
        ),
        scratch_shapes=[pltpu.VMEM((tm + CONV_HALO, cwid), F32),
                        pltpu.VMEM((SUBLANES - 1, tm + CONV_HALO - SUBLANES, cwid), F32)],
        name="inproj_prompt",
        compiler_params=_params(("arbitrary",)),
    )(x, g, w_in, rope_tile, rope_row, conv_w, conv_b, ln_g, ln_b)


def _inproj_sample_kernel(x_ref, g_ref, w_ref, cos_ref, sin_ref, cc_ref, toep_ref, cb_ref, lng_ref, lnb_ref,
                          kout_ref, vout_ref, q_ref, glu_ref, c_ref, *, aw, cwid, n_heads, scale, n_b, n_t):
    rows = n_b * n_t
    h = _rms(x_ref[...], g_ref[...]).astype(BF16)
    cos = cos_ref[...]
    sin = sin_ref[...]
    lane = lax.broadcasted_iota(jnp.int32, (rows, LANES), 1)
    first_half = (lane % HEAD_DIM) < (HEAD_DIM // 2)
    q = jnp.dot(h, w_ref[:, 0:aw], preferred_element_type=F32)
    k = jnp.dot(h, w_ref[:, aw:2 * aw], preferred_element_type=F32)
    vout_ref[...] = jnp.dot(h, w_ref[:, 2 * aw:3 * aw], preferred_element_type=F32)
    for hd in range(n_heads):
        sl = slice(hd * LANES, (hd + 1) * LANES)
        q_ref[:, sl] = _rope_lanes(q[:, sl], cos, sin, first_half) * scale
        kout_ref[:, sl] = _rope_lanes(k[:, sl], cos, sin, first_half)
    u1 = jnp.dot(h, w_ref[:, 3 * aw:3 * aw + cwid], preferred_element_type=F32)
    u2 = jnp.dot(h, w_ref[:, 3 * aw + cwid:3 * aw + 2 * cwid], preferred_element_type=F32)
    glu_ref[...] = (u1 * jax.nn.sigmoid(u2)).reshape(n_b, n_t, cwid)

    acc = jnp.zeros((n_b, n_t, cwid), F32)
    for r in range(CONV_K - 1):
        acc = acc + cc_ref[:, r:r + 1, :] * toep_ref[r]
    for r in range(n_t):
        acc = acc + glu_ref[:, r:r + 1, :] * toep_ref[CONV_K - 1 + r]
    acc = acc + cb_ref[...]
    mu = jnp.mean(acc, axis=-1, keepdims=True)
    d = acc - mu
    var = jnp.mean(d * d, axis=-1, keepdims=True)
    y = d * lax.rsqrt(var + LN_EPS) * lng_ref[...] + lnb_ref[...]
    c_ref[...] = y * jax.nn.sigmoid(y)


def _inproj_sample(x, g, w_in, cos, sin, cache_conv, toep, conv_b, ln_g, ln_b, *, aw, cwid, n_heads, n_b, n_t):
    rows = n_b * n_t
    kern = functools.partial(_inproj_sample_kernel, aw=aw, cwid=cwid, n_heads=n_heads,
                             scale=HEAD_DIM ** -0.5, n_b=n_b, n_t=n_t)
    return pl.pallas_call(
        kern,
        out_shape=(
            jax.ShapeDtypeStruct((rows, aw), F32),
            jax.ShapeDtypeStruct((rows, aw), F32),
            jax.ShapeDtypeStruct((rows, aw), F32),
            jax.ShapeDtypeStruct((n_b, n_t, cwid), F32),
            jax.ShapeDtypeStruct((n_b, n_t, cwid), F32),
        ),
        name="inproj_sample",
        compiler_params=pltpu.CompilerParams(vmem_limit_bytes=VMEM_LIMIT),
    )(x, g, w_in, cos, sin, cache_conv, toep, conv_b, ln_g, ln_b)


def _masked_queries(q, n_heads, n_t):
    n_rows = n_heads * 2 * n_t
    aw = q.shape[-1]
    qrep = jnp.concatenate([q] * (n_heads * 2), axis=0)
    sel = (lax.broadcasted_iota(jnp.int32, (n_rows, aw), 0) // n_t
           == lax.broadcasted_iota(jnp.int32, (n_rows, aw), 1) // HEAD_DIM)
    return jnp.where(sel, qrep, 0.0).astype(BF16)


def _sample_init_kernel(q_ref, kn_ref, vn_ref, m_ref, l_ref, acc_ref, *, n_heads, n_t, group):
    aw = q_ref.shape[-1]
    n_rows = n_heads * 2 * n_t
    pad = jnp.zeros((LANES - n_t, aw), F32)
    col = lax.broadcasted_iota(jnp.int32, (n_rows, LANES), 1)
    qt = lax.broadcasted_iota(jnp.int32, (n_rows, LANES), 0) % n_t
    for b in range(group):
        rows = slice(b * n_t, (b + 1) * n_t)
        wq = _masked_queries(q_ref[rows, :], n_heads, n_t)
        kn = jnp.concatenate([kn_ref[rows, :], pad], axis=0).astype(BF16)
        vn = jnp.concatenate([vn_ref[rows, :], pad], axis=0).astype(BF16)
        sc = lax.dot_general(wq, kn, (((1,), (1,)), ((), ())), preferred_element_type=F32)
        sc = jnp.where(col <= qt, sc, -jnp.inf)
        m = jnp.max(sc, axis=-1, keepdims=True)
        p = jnp.exp(sc - m)
        m_ref[b] = m
        l_ref[b] = jnp.sum(p, axis=-1, keepdims=True)
        acc_ref[b] = jnp.dot(p.astype(BF16), vn, preferred_element_type=F32)


def _sample_init(q, k_new, v_new, *, n_b, n_t, n_heads):
    aw = q.shape[-1]
    n_rows = n_heads * 2 * n_t
    group = math.gcd(n_b, SUBLANES)
    tok = pl.BlockSpec((group * n_t, aw), lambda g: (g, 0))
    stat = pl.BlockSpec((group, n_rows, 1), lambda g: (g, 0, 0))
    return pl.pallas_call(
        functools.partial(_sample_init_kernel, n_heads=n_heads, n_t=n_t, group=group),
        grid=(n_b // group,),
        in_specs=[tok, tok, tok],
        out_specs=[stat, stat, pl.BlockSpec((group, n_rows, aw), lambda g: (g, 0, 0))],
        out_shape=(jax.ShapeDtypeStruct((n_b, n_rows, 1), F32), jax.ShapeDtypeStruct((n_b, n_rows, 1), F32),
                   jax.ShapeDtypeStruct((n_b, n_rows, aw), F32)),
        name="sample_init",
        compiler_params=_params(("arbitrary",)),
    )(q, k_new, v_new)


def _attn_kernel(pt_ref, th_ref, ti_ref, tt_ref, tl_ref, te_ref, tb_ref, tn_ref, qt_ref, kst_ref, vt_ref, bound_ref,
                 lam_ref, sgcol_ref, sgrow_ref, qs_ref, m0_ref, l0_ref, a0_ref, *rest,
                 n_pages, n_units, units_per_entry, tk, page, n_heads, n_t, lam_init):
    k_pages = rest[:n_pages]
    v_pages = rest[n_pages:2 * n_pages]
    op_ref, os_ref = rest[2 * n_pages:2 * n_pages + 2]
    (m_ref, acc_ref, q_scr, ref_scr, sa_ref, sb_ref, xa_ref, xb_ref, pend_ref,
     kbuf, vbuf, ms_ref, ls_ref, accs_ref) = rest[2 * n_pages + 2:]
    g = pl.program_id(0)
    i = ti_ref[g]
    t = tt_ref[g]
    tq = 2 * tk
    lam = _lam(lam_ref, lam_init)

    unit = jnp.minimum(g, n_units - 1)
    active = g < n_units
    first = unit % units_per_entry == 0

    def sample_scores():
        for p in range(n_pages):
            kbuf[:, p * page:(p + 1) * page] = k_pages[p][0].astype(BF16)
            for hd in range(n_heads):
                vbuf[p * page:(p + 1) * page, hd * V_DIM:(hd + 1) * V_DIM] = (
                    v_pages[p][0, pl.ds(hd, page, stride=n_heads), :].astype(BF16))
        wq = _masked_queries(qs_ref[...], n_heads, n_t)
        sc = jnp.dot(wq, kbuf[...], preferred_element_type=F32)
        return jnp.where(active, sc, -jnp.inf)

    def sample_softmax(sc):
        m_prev = jnp.where(first, m0_ref[0], ms_ref[...])
        l_prev = jnp.where(first, l0_ref[0], ls_ref[...])
        m_new = jnp.maximum(m_prev, jnp.max(sc, axis=-1, keepdims=True))
        p = jnp.exp(sc - m_new)
        alpha = jnp.exp(m_prev - m_new)
        l_new = alpha * l_prev + jnp.sum(p, axis=-1, keepdims=True)
        ms_ref[...] = m_new
        ls_ref[...] = l_new
        return p.astype(BF16), alpha, l_new

    def sample_values(p, alpha, l_new):
        acc = (alpha * jnp.where(first, a0_ref[0], accs_ref[...])
               + jnp.dot(p, vbuf[...], preferred_element_type=F32))
        accs_ref[...] = acc
        o = acc / l_new
        for hd in range(n_heads):
            r0 = hd * 2 * n_t
            sl = slice(hd * V_DIM, (hd + 1) * V_DIM)
            a = o[r0:r0 + n_t, sl] - lam * o[r0 + n_t:r0 + 2 * n_t, sl]
            os_ref[:, sl] = _rms(a, sgrow_ref[...]) * (1.0 - lam_init)

    def scores(j, s_ref, smax_ref, key_offset=None):
        st = jnp.dot(kst_ref[0, j], q_scr[...], preferred_element_type=F32)
        if key_offset is not None:
            keep = (lax.broadcasted_iota(jnp.int32, (tk, tq), 0) + key_offset
                    <= lax.broadcasted_iota(jnp.int32, (tk, tq), 1))
        for c in range(2):
            sc = st[c * tk:(c + 1) * tk, :]
            if key_offset is not None:
                sc = jnp.where(keep, sc, -jnp.inf)
            s_ref[c * tk:(c + 1) * tk, :] = sc
            smax_ref[c] = jnp.max(sc, axis=0, keepdims=True)

    def absorb(j, s_ref, smax_ref):
        vt = vt_ref[0, j]
        for c in range(2):
            sc = s_ref[c * tk:(c + 1) * tk, :]
            m_prev = m_ref[c]
            m_new = jnp.maximum(m_prev, smax_ref[c])
            p = jnp.exp2(sc - m_new).astype(BF16)
            alpha = jnp.exp2(m_prev - m_new)
            acc_ref[c] = alpha * acc_ref[c] + jnp.dot(vt, p, preferred_element_type=F32)
            m_ref[c] = m_new

    def bounded_block(j, key_offset=None):
        st = jnp.dot(kst_ref[0, j], q_scr[...], preferred_element_type=F32)
        vt = vt_ref[0, j]
        if key_offset is not None:
            keep = (lax.broadcasted_iota(jnp.int32, (tk, tq), 0) + key_offset
                    <= lax.broadcasted_iota(jnp.int32, (tk, tq), 1))
        for c in range(2):
            p = jnp.exp2(st[c * tk:(c + 1) * tk, :] - ref_scr[c])
            if key_offset is not None:
                p = jnp.where(keep, p, 0.0)
            acc_ref[c] += jnp.dot(vt, p.astype(BF16), preferred_element_type=F32)

    def start_query_block():
        q_scr[:, 0:tk] = qt_ref[0, 0]
        q_scr[:, tk:tq] = qt_ref[0, 1]
        acc_ref[...] = jnp.zeros(acc_ref.shape, F32)

    bounded = tb_ref[g] == 1
    diag = t < 0

    @pl.when(jnp.logical_and(diag, bounded))
    def _():
        start_query_block()
        for c in range(2):
            ref_scr[c, :, 0:tk] = bound_ref[0, 0, c:c + 1, :]
            ref_scr[c, :, tk:tq] = bound_ref[0, 1, c:c + 1, :]
        sc = sample_scores()
        bounded_block(2 * i, key_offset=0)
        soft = sample_softmax(sc)
        bounded_block(2 * i + 1, key_offset=tk)
        sample_values(*soft)

    n_pairs = tn_ref[g]

    @pl.when(jnp.logical_and(n_pairs == 1, bounded))
    def _():
        sc = sample_scores()
        bounded_block(2 * t)
        soft = sample_softmax(sc)
        bounded_block(2 * t + 1)
        sample_values(*soft)

    @pl.when(jnp.logical_and(n_pairs == 2, bounded))
    def _():
        sc = sample_scores()
        bounded_block(2 * t)
        soft = sample_softmax(sc)
        bounded_block(2 * t + 1)
        sample_values(*soft)
        bounded_block(2 * t + 2)
        bounded_block(2 * t + 3)

    @pl.when(jnp.logical_and(diag, jnp.logical_not(bounded)))
    def _():
        start_query_block()
        m_ref[...] = jnp.full(m_ref.shape, -jnp.inf, F32)
        sc = sample_scores()
        scores(2 * i, sa_ref, xa_ref, key_offset=0)
        soft = sample_softmax(sc)
        scores(2 * i + 1, sb_ref, xb_ref, key_offset=tk)
        sample_values(*soft)
        absorb(2 * i, sa_ref, xa_ref)
        pend_ref[0] = 2 * i + 1

    @pl.when(jnp.logical_and(n_pairs >= 1, jnp.logical_not(bounded)))
    def _():
        pending = pend_ref[0]
        sc = sample_scores()
        scores(2 * t, sa_ref, xa_ref)
        soft = sample_softmax(sc)
        absorb(pending, sb_ref, xb_ref)
        sample_values(*soft)
        scores(2 * t + 1, sb_ref, xb_ref)
        absorb(2 * t, sa_ref, xa_ref)
        pend_ref[0] = 2 * t + 1

    @pl.when(jnp.logical_and(n_pairs == 2, jnp.logical_not(bounded)))
    def _():
        scores(2 * t + 2, sa_ref, xa_ref)
        absorb(2 * t + 1, sb_ref, xb_ref)
        scores(2 * t + 3, sb_ref, xb_ref)
        absorb(2 * t + 2, sa_ref, xa_ref)
        pend_ref[0] = 2 * t + 3

    last = tl_ref[g] == 1

    @pl.when(jnp.logical_and(last, jnp.logical_not(bounded)))
    def _():
        absorb(pend_ref[0], sb_ref, xb_ref)

    @pl.when(last)
    def _():
        o1 =acc_ref[0, 0:V_DIM, :] / acc_ref[0, V_DIM:V_DIM + 1, :]
        o2 = acc_ref[1, 0:V_DIM, :] / acc_ref[1, V_DIM:V_DIM + 1, :]
        a = o1 - lam * o2
        y = a * lax.rsqrt(jnp.mean(a * a, axis=0, keepdims=True) + RMS_EPS) * sgcol_ref[...] * (1.0 - lam_init)
        op_ref[...] = y.T.astype(op_ref.dtype)


def _prompt_tasks(n_heads, nq):
    tasks = []
    for h in range(n_heads):
        for i in range(nq):
            tasks.append((h, i, -1, int(i == 0), 0))
            for t in range(0, i, 2):
                n = min(2, i - t)
                tasks.append((h, i, t, int(t + n == i), n))
    return tuple(np.asarray(col, np.int32) for col in zip(*tasks))


def _attention(page_table, qt, kst, vt, qn2, kn2, lam_vecs, subln_g, q_s, k_new, v_new, cache_kt, cache_v,
               *, n_b, n_t, lam_init):
    n_heads, nt, _, tk = qt.shape
    assert nt % 2 == 0, "a query block spans two key blocks"
    nq = nt // 2
    tq = 2 * tk
    t = nt * tk
    n_pool, aw, page = cache_kt.shape
    n_rows = n_heads * 2 * n_t
    th, ti, tt, tl, tn = _prompt_tasks(n_heads, nq)
    n_tasks = len(th)
    kmax2 = jnp.max(kn2[:, :, 0], axis=0).reshape(n_heads, 1, 2, 1)
    bound = jnp.sqrt(qn2 * kmax2)
    block_bound = jnp.max(bound.reshape(n_heads, nq, -1), axis=-1)
    tb = (block_bound <= SAFE_BOUND).astype(jnp.int32)[th, ti]
    n_pages_entry = page_table.shape[1]
    pages_per_unit = min(p for p in range(1, n_pages_entry + 1)
                         if n_pages_entry % p == 0 and n_b * (n_pages_entry // p) <= n_tasks)
    units_per_entry = n_pages_entry // pages_per_unit
    n_units = n_b * units_per_entry
    m0, l0, a0 = _sample_init(q_s, k_new, v_new, n_b=n_b, n_t=n_t, n_heads=n_heads)
    kern = functools.partial(_attn_kernel, n_pages=pages_per_unit, n_units=n_units,
                             units_per_entry=units_per_entry, tk=tk, page=page, n_heads=n_heads, n_t=n_t,
                             lam_init=lam_init)

    unit_of_step = np.minimum(np.arange(n_tasks), n_units - 1)
    te = (unit_of_step // units_per_entry).astype(np.int32)
    step_pages = page_table.reshape(n_units, pages_per_unit)[unit_of_step].reshape(-1)

    tok = pl.BlockSpec((n_t, aw), lambda g, pt, th, ti, tt, tl, te, *_: (te[g], 0))
    stat = pl.BlockSpec((1, n_rows, 1), lambda g, pt, th, ti, tt, tl, te, *_: (te[g], 0, 0))

    def page_spec(p, shape):
        return pl.BlockSpec((1,) + shape, lambda g, pt, *_: (pt[g * pages_per_unit + p], 0, 0))

    k_specs = [page_spec(p, (aw, page)) for p in range(pages_per_unit)]
    v_specs = [page_spec(p, (page * n_heads, V_DIM)) for p in range(pages_per_unit)]
    head_map = lambda g, pt, th, *_: (th[g], 0, 0, 0)
    grid_spec = pltpu.PrefetchScalarGridSpec(
        num_scalar_prefetch=8,
        grid=(n_tasks,),
        in_specs=[
            pl.BlockSpec((1, 2, LANES, tk), lambda g, pt, th, ti, *_: (th[g], ti[g], 0, 0)),
            pl.BlockSpec((1, nt, 2 * tk, LANES), head_map, pipeline_mode=pl.Buffered(1)),
            pl.BlockSpec((1, nt, V_DIM + SUM_ROWS, tk), head_map, pipeline_mode=pl.Buffered(1)),
            pl.BlockSpec((1, 2, 2, tk), lambda g, pt, th, ti, *_: (th[g], ti[g], 0, 0)),
            pl.BlockSpec(lam_vecs.shape, lambda g, *_: (0, 0)),
            pl.BlockSpec((V_DIM, 1), lambda g, *_: (0, 0)),
            pl.BlockSpec((1, V_DIM), lambda g, *_: (0, 0)),
            tok, stat, stat,
            pl.BlockSpec((1, n_rows, aw), lambda g, pt, th, ti, tt, tl, te, *_: (te[g], 0, 0)),
        ] + k_specs + v_specs,
        out_specs=[
            pl.BlockSpec((tq, V_DIM), lambda g, pt, th, ti, *_: (ti[g], th[g])),
            tok,
        ],
        scratch_shapes=[
            pltpu.VMEM((2, 1, tq), F32),
            pltpu.VMEM((2, V_DIM + SUM_ROWS, tq), F32),
            pltpu.VMEM((LANES, tq), BF16),
            pltpu.VMEM((2, 1, tq), F32),
            pltpu.VMEM((2 * tk, tq), F32),
            pltpu.VMEM((2 * tk, tq), F32),
            pltpu.VMEM((2, 1, tq), F32),
            pltpu.VMEM((2, 1, tq), F32),
            pltpu.SMEM((1,), jnp.int32),
            pltpu.VMEM((aw, pages_per_unit * page), BF16),
            pltpu.VMEM((pages_per_unit * page, aw), BF16),
            pltpu.VMEM((n_rows, 1), F32),
            pltpu.VMEM((n_rows, 1), F32),
            pltpu.VMEM((n_rows, aw), F32),
        ],
    )
    return pl.pallas_call(
        kern,
        grid_spec=grid_spec,
        out_shape=(jax.ShapeDtypeStruct((t, n_heads * V_DIM), BF16),
                   jax.ShapeDtypeStruct((n_b * n_t, aw), F32)),
        name="attention",
        compiler_params=_params(("arbitrary",)),
    )(step_pages, th, ti, tt, tl, te, tb, tn, qt, kst, vt, bound, lam_vecs, subln_g.reshape(-1, 1),
      subln_g.reshape(1, -1), q_s, m0, l0, a0, *([cache_kt] * pages_per_unit), *([cache_v] * pages_per_unit))


def _cross_kernel(o_ref, c_ref, x_ref, wout_ref, gpost_ref, gpre_ref, wmq_ref, mk_ref, mv_ref, wmo_ref,
                  gcpost_ref, y_ref, *, mem_heads, row_splits):
    tm, d = x_ref.shape
    hd = d // mem_heads
    n_lt = hd // LANES
    n_mem = mk_ref.shape[1] // (mem_heads * n_lt)

    def head_rows(ref, h):
        return jnp.concatenate(
            [ref[0, pl.ds(lt * mem_heads + h, n_mem, stride=mem_heads * n_lt), :] for lt in range(n_lt)],
            axis=-1).astype(BF16)

    mks = [head_rows(mk_ref, h) for h in range(mem_heads)]
    mvs = [head_rows(mv_ref, h) for h in range(mem_heads)]
    rs = tm // row_splits
    groups = [slice(r * rs, (r + 1) * rs) for r in range(row_splits)]
    scale = hd ** -0.5
    mix = [jnp.dot(jnp.concatenate([o_ref[g, :].astype(BF16), c_ref[g, :].astype(BF16)], axis=-1),
                   wout_ref[...], preferred_element_type=F32) for g in groups]
    x1 = [x_ref[g, :] + _rms(m, gpost_ref[...]) for g, m in zip(groups, mix)]
    q = [(jnp.dot(_rms(x, gpre_ref[...]).astype(BF16), wmq_ref[...], preferred_element_type=F32)
          * scale).astype(BF16) for x in x1]
    outs = [[] for _ in groups]
    for h in range(mem_heads):
        sl = slice(h * hd, (h + 1) * hd)
        s = [lax.dot_general(qr[:, sl], mks[h], (((1,), (1,)), ((), ())), preferred_element_type=F32)
             for qr in q]
        for r, sr in enumerate(s):
            p = jnp.exp(sr - jnp.max(sr, axis=-1, keepdims=True))
            p = p / jnp.sum(p, axis=-1, keepdims=True)
            outs[r].append(jnp.dot(p.astype(BF16), mvs[h], preferred_element_type=F32).astype(BF16))
    att = [jnp.dot(jnp.concatenate(o, axis=-1), wmo_ref[...], preferred_element_type=F32) for o in outs]
    for g, x, a in zip(groups, x1, att):
        y_ref[g, :] = x + _rms(a, gcpost_ref[...])


def _cross(o, c, x, w_out, g_post, g_pre, w_mq, mem_k, mem_v, w_mo, g_cpost, *, tm, mem_heads):
    t, d = x.shape
    aw = o.shape[1]
    cwid = c.shape[1]
    mem_rows, mem_hd = mem_k.shape[1:]
    row = lambda i: (i, 0)
    return pl.pallas_call(
        functools.partial(_cross_kernel, mem_heads=mem_heads, row_splits=1),
        grid=(t // tm,),
        in_specs=[
            pl.BlockSpec((tm, aw), row),
            pl.BlockSpec((tm, cwid), row),
            pl.BlockSpec((tm, d), row),
            _const_spec(w_out.shape),
            _const_spec((1, d)),
            _const_spec((1, d)),
            _const_spec(w_mq.shape),
            _const_spec((1, mem_rows, mem_hd)),
            _const_spec((1, mem_rows, mem_hd)),
            _const_spec(w_mo.shape),
            _const_spec((1, d)),
        ],
        out_specs=pl.BlockSpec((tm, d), row),
        out_shape=jax.ShapeDtypeStruct((t, d), F32),
        name="cross_prompt",
        compiler_params=_params(("arbitrary",)),
    )(o, c, x, w_out, g_post, g_pre, w_mq, mem_k, mem_v, w_mo, g_cpost)


def _cross_sample_kernel(o_ref, c_ref, x_ref, wout_ref, gpost_ref, gpre_ref, wmq_ref, mk_ref, mv_ref, wmo_ref,
                         gcpost_ref, y_ref, x1_scr, q_scr, att_scr, *, mem_heads, n_t):
    b = pl.program_id(0)
    d = x_ref.shape[-1]
    hd = d // mem_heads
    n_lt = hd // LANES
    n_mem = mk_ref.shape[1] // (mem_heads * n_lt)

    @pl.when(b == 0)
    def _():
        mix_in = jnp.concatenate([o_ref[...].astype(BF16), c_ref[...].astype(BF16)], axis=-1)
        mix = jnp.dot(mix_in, wout_ref[...], preferred_element_type=F32)
        x1 = x_ref[...] + _rms(mix, gpost_ref[...])
        x1_scr[...] = x1
        hq = _rms(x1, gpre_ref[...]).astype(BF16)
        q_scr[...] = jnp.dot(hq, wmq_ref[...], preferred_element_type=F32) * (hd ** -0.5)

    def head_rows(ref, h):
        return jnp.concatenate(
            [ref[0, pl.ds(lt * mem_heads + h, n_mem, stride=mem_heads * n_lt), :] for lt in range(n_lt)],
            axis=-1).astype(BF16)

    rows = pl.ds(pl.multiple_of(b * n_t, n_t), n_t)
    q = q_scr[rows, :].astype(BF16)
    for h in range(mem_heads):
        sl = slice(h * hd, (h + 1) * hd)
        s = lax.dot_general(q[:, sl], head_rows(mk_ref, h), (((1,), (1,)), ((), ())),
                            preferred_element_type=F32)
        p = jnp.exp(s - jnp.max(s, axis=-1, keepdims=True))
        p = p / jnp.sum(p, axis=-1, keepdims=True)
        att_scr[rows, sl] = jnp.dot(p.astype(BF16), head_rows(mv_ref, h), preferred_element_type=F32)

    @pl.when(b == pl.num_programs(0) - 1)
    def _():
        att = jnp.dot(att_scr[...].astype(BF16), wmo_ref[...], preferred_element_type=F32)
        y_ref[...] = x1_scr[...] + _rms(att, gcpost_ref[...])


def _cross_sample(o, c, x, w_out, g_post, g_pre, w_mq, mem_k, mem_v, w_mo, g_cpost, *, n_t, mem_heads):
    t, d = x.shape
    n_b, mem_rows, mem_hd = mem_k.shape
    assert n_t == SUBLANES and t == n_b * n_t
    mem_map = lambda b: (b, 0, 0)
    return pl.pallas_call(
        functools.partial(_cross_sample_kernel, mem_heads=mem_heads, n_t=n_t),
        grid=(n_b,),
        in_specs=[
            _const_spec(o.shape),
            _const_spec(c.shape),
            _const_spec(x.shape),
            _const_spec(w_out.shape),
            _const_spec((1, d)),
            _const_spec((1, d)),
            _const_spec(w_mq.shape),
            pl.BlockSpec((1, mem_rows, mem_hd), mem_map),
            pl.BlockSpec((1, mem_rows, mem_hd), mem_map),
            _const_spec(w_mo.shape),
            _const_spec((1, d)),
        ],
        out_specs=pl.BlockSpec((t, d), lambda b: (0, 0)),
        out_shape=jax.ShapeDtypeStruct((t, d), F32),
        scratch_shapes=[pltpu.VMEM((t, d), F32), pltpu.VMEM((t, d), F32), pltpu.VMEM((t, d), F32)],
        name="cross_sample",
        compiler_params=_params(("arbitrary",)),
    )(o, c, x, w_out, g_post, g_pre, w_mq, mem_k, mem_v, w_mo, g_cpost)


def _mlp_kernel(x_ref, gpre_ref, w1_ref, w2_ref, gpost_ref, y_ref):
    x = x_ref[...]
    h = _rms(x, gpre_ref[...]).astype(BF16)
    hf = jnp.dot(h, w1_ref[...], preferred_element_type=F32)
    r = jnp.maximum(hf, 0.0)
    f = jnp.dot((r * r).astype(BF16), w2_ref[...], preferred_element_type=F32)
    y_ref[...] = x + _rms(f, gpost_ref[...])


def _mlp(x, g_pre, w1, w2, g_post, *, tm):
    t, d = x.shape
    row = lambda i: (i, 0)
    return pl.pallas_call(
        _mlp_kernel,
        grid=(t // tm,),
        in_specs=[pl.BlockSpec((tm, d), row), _const_spec((1, d)), _const_spec(w1.shape),
                  _const_spec(w2.shape), _const_spec((1, d))],
        out_specs=pl.BlockSpec((tm, d), row),
        out_shape=jax.ShapeDtypeStruct((t, d), F32),
        name="mlp_tile%d" % tm,
        compiler_params=_params(("arbitrary",)),
    )(x, g_pre, w1, w2, g_post)


def _rope_tables(pos):
    half = HEAD_DIM // 2
    inv = ROPE_THETA ** (-jnp.arange(0, half, dtype=F32) * 2.0 / HEAD_DIM)
    ang = pos.astype(F32)[:, None] * inv[None, :]
    cos = jnp.cos(ang)
    sin = jnp.sin(ang)
    reps = LANES // HEAD_DIM
    return (jnp.tile(jnp.concatenate([cos, cos], axis=-1), (1, reps)),
            jnp.tile(jnp.concatenate([-sin, sin], axis=-1), (1, reps)))


def _conv_toeplitz(conv_w, n_t):
    n_r = CONV_K - 1 + n_t
    r = jnp.arange(n_r)[:, None]
    t = jnp.arange(n_t)[None, :]
    j = r - t
    ok = (j >= 0) & (j < CONV_K)
    return jnp.where(ok[:, :, None], conv_w[jnp.clip(j, 0, CONV_K - 1)], 0.0)


def _layer(i, depth_inputs, x_prompt, x_sample, mem_prompt, cache_k, cache_v, cache_conv, cache_mem_k,
           cache_mem_v, page_table, *, tile):
    (w_in, lq1, lk1, lq2, lk2, subln_g, conv_w, conv_b, ln_g, ln_b, w_out, mix_pre_g, mix_post_g, mem_norm_g,
     w_mq, w_mk, w_mv, w_mo, cross_pre_g, cross_post_g, w_ff1, w_ff2, mlp_pre_g, mlp_post_g) = depth_inputs
    lam_init = 0.8 - 0.6 * math.exp(-0.3 * i)
    _, seq, d = x_prompt.shape
    n_b, n_t, _ = x_sample.shape
    n_heads, v_dim = cache_v.shape[-2:]
    aw = n_heads * v_dim
    cwid = cache_conv.shape[-1]
    mem_heads = cache_mem_k.shape[-2]
    n_mem = mem_prompt.shape[1]
    past = page_table.shape[1] * cache_k.shape[1]

    row2 = lambda a: a.reshape(1, -1)
    w_in_b, w_out_b, w_mq_b, w_mk_b, w_mv_b, w_mo_b, w_ff1_b, w_ff2_b = (
        w.astype(BF16) for w in (w_in, w_out, w_mq, w_mk, w_mv, w_mo, w_ff1, w_ff2))
    lam_vecs = jnp.stack([lq1, lk1, lq2, lk2])

    mk_p, mv_p = _memory_kv(mem_prompt[0], row2(mem_norm_g), w_mk_b, w_mv_b, mem_heads=mem_heads)
    rope_tile = jnp.stack(_rope_tables(jnp.arange(0, seq, tile, dtype=jnp.int32)))
    rope_row = jnp.stack(_rope_tables(jnp.arange(tile, dtype=jnp.int32)))
    k_p, v_p, qt, kst, vt, c_p, tail_p, qn2, kn2 = _inproj_prompt(
        x_prompt[0], row2(mix_pre_g), w_in_b, rope_tile, rope_row, conv_w, row2(conv_b), row2(ln_g), row2(ln_b),
        tm=tile, aw=aw, cwid=cwid, n_heads=n_heads)

    pos_s = jnp.tile(past + jnp.arange(n_t, dtype=jnp.int32), n_b)
    cos_s, sin_s = _rope_tables(pos_s)
    xs = x_sample.reshape(n_b * n_t, d)
    k_s, v_s, q_s, glu_s, c_s = _inproj_sample(
        xs, row2(mix_pre_g), w_in_b, cos_s, sin_s, cache_conv, _conv_toeplitz(conv_w, n_t), row2(conv_b),
        row2(ln_g), row2(ln_b), aw=aw, cwid=cwid, n_heads=n_heads, n_b=n_b, n_t=n_t)
    n_pool, page = cache_k.shape[:2]
    cache_kt = jnp.transpose(cache_k, (0, 2, 3, 4, 1)).reshape(n_pool, aw, page)
    cache_vr = cache_v.reshape(n_pool, page * n_heads, v_dim)
    o_p, o_s = _attention(page_table, qt, kst, vt, qn2, kn2, lam_vecs, subln_g, q_s, k_s, v_s, cache_kt, cache_vr,
                          n_b=n_b, n_t=n_t, lam_init=lam_init)
    x2_p = _cross(o_p, c_p, x_prompt[0], w_out_b, row2(mix_post_g), row2(cross_pre_g), w_mq_b,
                  mk_p[None], mv_p[None], w_mo_b, row2(cross_post_g), tm=tile, mem_heads=mem_heads)
    y_p = _mlp(x2_p, row2(mlp_pre_g), w_ff1_b, w_ff2_b, row2(mlp_post_g), tm=tile)
    n_lt = d // mem_heads // LANES

    def mem_rows(a):
        a = a.reshape(a.shape[0], n_mem, mem_heads, n_lt, LANES)
        return jnp.transpose(a, (0, 1, 3, 2, 4)).reshape(a.shape[0], n_mem * n_lt * mem_heads, LANES)

    x2_s = _cross_sample(o_s, c_s.reshape(n_b * n_t, cwid), xs, w_out_b, row2(mix_post_g), row2(cross_pre_g),
                         w_mq_b, mem_rows(cache_mem_k), mem_rows(cache_mem_v), w_mo_b,
                         row2(cross_post_g), n_t=n_t, mem_heads=mem_heads)

    def mem_out(a):
        a = a.reshape(n_mem, n_lt, mem_heads, LANES)
        return jnp.transpose(a, (0, 2, 1, 3)).reshape(1, n_mem, mem_heads, n_lt * LANES)
    y_s = _mlp(x2_s, row2(mlp_pre_g), w_ff1_b, w_ff2_b, row2(mlp_post_g), tm=n_b * n_t)

    hd2 = (n_heads, 2, HEAD_DIM)
    outs = dict(
        y_p=y_p[None], y_s=y_s.reshape(n_b, n_t, d),
        k_p=jnp.transpose(k_p.reshape(*hd2, seq), (3, 0, 1, 2))[None],
        v_p=v_p.reshape(1, seq, n_heads, v_dim),
        conv_p=tail_p[CONV_HALO - (CONV_K - 1):][None],
        mk_p=mem_out(mk_p), mv_p=mem_out(mv_p),
        k_s=k_s.reshape(n_b, n_t, *hd2), v_s=v_s.reshape(n_b, n_t, n_heads, v_dim),
        conv_s=jnp.concatenate([cache_conv[:, n_t:], glu_s], axis=1),
    )
    return outs


def kernel(x_prompt, x_sample, mem_prompt, cache_k, cache_v, cache_conv, cache_mem_k, cache_mem_v, page_table, w_in, lambda_q1, lambda_k1, lambda_q2, lambda_k2, subln_g, conv_w, conv_b, conv_ln_g, conv_ln_b, w_out, mix_pre_g, mix_post_g, mem_norm_g, w_mq, w_mk, w_mv, w_mo, cross_pre_g, cross_post_g, w_ff1, w_ff2, mlp_pre_g, mlp_post_g):
    depth = w_in.shape[0]
    stacked = (w_in, lambda_q1, lambda_k1, lambda_q2, lambda_k2, subln_g, conv_w, conv_b, conv_ln_g, conv_ln_b,
               w_out, mix_pre_g, mix_post_g, mem_norm_g, w_mq, w_mk, w_mv, w_mo, cross_pre_g, cross_post_g,
               w_ff1, w_ff2, mlp_pre_g, mlp_post_g)
    tile = min(512, x_prompt.shape[1])
    y_p, y_s = x_prompt, x_sample
    per_layer = []
    for i in range(depth):
        o = _layer(i, tuple(a[i] for a in stacked), y_p, y_s, mem_prompt, cache_k[i], cache_v[i], cache_conv[i],
                   cache_mem_k[i], cache_mem_v[i], page_table, tile=tile)
        y_p, y_s = o["y_p"], o["y_s"]
        per_layer.append(o)
    stack = lambda name: jnp.stack([o[name] for o in per_layer], axis=0)
    return (y_p, y_s, stack("k_p"), stack("v_p"), stack("conv_p"), stack("mk_p"), stack("mv_p"),
            stack("k_s"), stack("v_s"), stack("conv_s"))
```

```python
import functools
import math

import jax
import numpy as np
import jax.numpy as jnp
from jax import lax
from jax.experimental import pallas as pl
from jax.experimental.pallas import tpu as pltpu

F32 = jnp.float32
BF16 = jnp.bfloat16

RMS_EPS = 1e-6
LN_EPS = 1e-5
ROPE_THETA = 10000.0
LANES = 128
SUBLANES = 8
HEAD_DIM = 64
V_DIM = 2 * HEAD_DIM
CONV_K = 31
CONV_HALO = 32
SUM_ROWS = 16
SAFE_BOUND = 60.0
VMEM_LIMIT = 56 * 1024 * 1024


def _rms(x, g):
    return x * lax.rsqrt(jnp.mean(x * x, axis=-1, keepdims=True) + RMS_EPS) * g


def _lam(lam_ref, lam_init):
    a = jnp.sum(lam_ref[0:1, :] * lam_ref[1:2, :], axis=-1, keepdims=True)
    b = jnp.sum(lam_ref[2:3, :] * lam_ref[3:4, :], axis=-1, keepdims=True)
    return jnp.exp(a) - jnp.exp(b) + lam_init


def _rope_lanes(x, cos, sin_signed, first_half):
    rot = jnp.where(first_half, pltpu.roll(x, LANES - HEAD_DIM // 2, 1), pltpu.roll(x, HEAD_DIM // 2, 1))
    return x * cos + rot * sin_signed


def _const_spec(shape):
    nd = len(shape)
    return pl.BlockSpec(shape, lambda *_: (0,) * nd, pipeline_mode=pl.Buffered(1))


def _params(sem):
    return pltpu.CompilerParams(dimension_semantics=sem, vmem_limit_bytes=VMEM_LIMIT)


def _memory_kv_kernel(mem_ref, g_ref, wk_ref, wv_ref, mk_ref, mv_ref, *, mem_heads):
    m = _rms(mem_ref[...], g_ref[...]).astype(BF16)
    n, d = mem_ref.shape
    hd = d // mem_heads
    mk = jnp.dot(m, wk_ref[...], preferred_element_type=F32)
    mv = jnp.dot(m, wv_ref[...], preferred_element_type=F32)
    n_lt = hd // LANES
    for h in range(mem_heads):
        for lt in range(n_lt):
            rows = pl.ds(lt * mem_heads + h, n, stride=mem_heads * n_lt)
            col = h * hd + lt * LANES
            mk_ref[rows, :] = mk[:, col:col + LANES]
            mv_ref[rows, :] = mv[:, col:col + LANES]


def _memory_kv(mem, g, wk, wv, *, mem_heads):
    n, d = mem.shape
    out = jax.ShapeDtypeStruct((n * d // LANES, LANES), F32)
    return pl.pallas_call(
        functools.partial(_memory_kv_kernel, mem_heads=mem_heads),
        out_shape=(out, out),
        name="memory_kv",
        compiler_params=pltpu.CompilerParams(vmem_limit_bytes=VMEM_LIMIT),
    )(mem, g, wk, wv)


def _inproj_prompt_kernel(x_ref, g_ref, w_ref, rope_tile_ref, rope_row_ref, cw_ref, cb_ref, lng_ref, lnb_ref,
                          kout_ref, vout_ref, qt_ref, kst_ref, vt_ref, c_ref, tail_ref, qn2_ref, kn2_ref,
                          gpad_ref, gsh_ref, *, tm, aw, cwid, n_heads, scale, row_chunk):
    i = pl.program_id(0)

    @pl.when(i == 0)
    def _():
        gpad_ref[0:CONV_HALO, :] = jnp.zeros((CONV_HALO, cwid), F32)

    h = _rms(x_ref[...], g_ref[...]).astype(BF16)
    cos_a = rope_tile_ref[0, pl.ds(i, 1), :]
    sin_a = rope_tile_ref[1, pl.ds(i, 1), :]
    cos_b = rope_row_ref[0]
    sin_b = rope_row_ref[1]
    cos = cos_a * cos_b - sin_a * sin_b
    sin = sin_a * cos_b + cos_a * sin_b
    lane = lax.broadcasted_iota(jnp.int32, (tm, LANES), 1)
    first_half = (lane % HEAD_DIM) < (HEAD_DIM // 2)
    lo = lane < HEAD_DIM
    sum_rows = (lax.broadcasted_iota(jnp.int32, (SUM_ROWS, tm), 0) == 0).astype(BF16)

    q = jnp.dot(h, w_ref[:, 0:aw], preferred_element_type=F32)
    k = jnp.dot(h, w_ref[:, aw:2 * aw], preferred_element_type=F32)
    v = jnp.dot(h, w_ref[:, 2 * aw:3 * aw], preferred_element_type=F32)
    for hd in range(n_heads):
        sl = slice(hd * LANES, (hd + 1) * LANES)
        qh = _rope_lanes(q[:, sl], cos, sin, first_half) * scale
        kh = _rope_lanes(k[:, sl], cos, sin, first_half)
        kht = kh.T
        qht = qh.T
        kout_ref[sl, :] = kht
        vout_ref[pl.ds(hd, tm, stride=n_heads), :] = v[:, sl]
        qt_ref[hd, 0] = qht.astype(BF16)
        for c in range(2):
            half = slice(c * HEAD_DIM, (c + 1) * HEAD_DIM)
            qn2_ref[hd, 0, c:c + 1, :] = jnp.sum(qht[half, :] * qht[half, :], axis=0, keepdims=True)
            kn2 = jnp.max(jnp.sum(kht[half, :] * kht[half, :], axis=0, keepdims=True), axis=1, keepdims=True)
            kn2_ref[0, 2 * hd + c:2 * hd + c + 1, :] = jnp.broadcast_to(kn2, (1, LANES))
        kst_ref[hd, 0, 0:tm, :] = jnp.where(lo, kh, 0.0).astype(BF16)
        kst_ref[hd, 0, tm:2 * tm, :] = jnp.where(lo, 0.0, kh).astype(BF16)
        vt_ref[hd, 0, 0:V_DIM, :] = v[:, sl].T.astype(BF16)
        vt_ref[hd, 0, V_DIM:V_DIM + SUM_ROWS, :] = sum_rows

    u1 = jnp.dot(h, w_ref[:, 3 * aw:3 * aw + cwid], preferred_element_type=F32)
    u2 = jnp.dot(h, w_ref[:, 3 * aw + cwid:3 * aw + 2 * cwid], preferred_element_type=F32)
    gpad_ref[CONV_HALO:CONV_HALO + tm, :] = u1 * jax.nn.sigmoid(u2)

    base = CONV_HALO - (CONV_K - 1)
    cb = cb_ref[...]
    lng = lng_ref[...]
    lnb = lnb_ref[...]
    n_sh = gsh_ref.shape[1]
    for b in range(1, SUBLANES):
        gsh_ref[b - 1] = gpad_ref[pl.ds(b, n_sh), :]
    for r0 in range(0, tm, row_chunk):
        acc = jnp.zeros((row_chunk, cwid), F32)
        for j in range(CONV_K):
            a, b = divmod(base + j, SUBLANES)
            rows = pl.ds(r0 + a * SUBLANES, row_chunk)
            tap = gpad_ref[rows, :] if b == 0 else gsh_ref[b - 1, rows, :]
            acc = acc + cw_ref[j:j + 1, :] * tap
        acc = acc + cb
        mu = jnp.mean(acc, axis=-1, keepdims=True)
        d = acc - mu
        var = jnp.mean(d * d, axis=-1, keepdims=True)
        y = d * lax.rsqrt(var + LN_EPS) * lng + lnb
        c_ref[r0:r0 + row_chunk, :] = (y * jax.nn.sigmoid(y)).astype(c_ref.dtype)

    tail_ref[...] = gpad_ref[tm:tm + CONV_HALO, :]
    gpad_ref[0:CONV_HALO, :] = gpad_ref[tm:tm + CONV_HALO, :]


def _inproj_prompt(x, g, w_in, rope_tile, rope_row, conv_w, conv_b, ln_g, ln_b, *, tm, aw, cwid, n_heads):
    t, d = x.shape
    nt = t // tm
    kern = functools.partial(_inproj_prompt_kernel, tm=tm, aw=aw, cwid=cwid, n_heads=n_heads,
                             scale=HEAD_DIM ** -0.5 * math.log2(math.e), row_chunk=min(tm, 32))
    row = lambda i: (i, 0)
    per_head = lambda i: (0, i, 0, 0)
    return pl.pallas_call(
        kern,
        grid=(nt,),
        in_specs=[
            pl.BlockSpec((tm, d), row),
            _const_spec((1, d)),
            _const_spec(w_in.shape),
            _const_spec(rope_tile.shape),
            _const_spec(rope_row.shape),
            _const_spec(conv_w.shape),
            _const_spec((1, cwid)),
            _const_spec((1, cwid)),
            _const_spec((1, cwid)),
        ],
        out_specs=[
            pl.BlockSpec((aw, tm), lambda i: (0, i)),
            pl.BlockSpec((tm * n_heads, V_DIM), row),
            pl.BlockSpec((n_heads, 1, LANES, tm), per_head),
            pl.BlockSpec((n_heads, 1, 2 * tm, LANES), per_head),
            pl.BlockSpec((n_heads, 1, V_DIM + SUM_ROWS, tm), per_head),
            pl.BlockSpec((tm, cwid), row),
            pl.BlockSpec((CONV_HALO, cwid), lambda i: (0, 0)),
            pl.BlockSpec((n_heads, 1, 2, tm), per_head),
            pl.BlockSpec((1, 2 * n_heads, LANES), lambda i: (i, 0, 0)),
        ],
        out_shape=(
            jax.ShapeDtypeStruct((aw, t), F32),
            jax.ShapeDtypeStruct((t * n_heads, V_DIM), F32),
            jax.ShapeDtypeStruct((n_heads, nt, LANES, tm), BF16),
            jax.ShapeDtypeStruct((n_heads, nt, 2 * tm, LANES), BF16),
            jax.ShapeDtypeStruct((n_heads, nt, V_DIM + SUM_ROWS, tm), BF16),
            jax.ShapeDtypeStruct((t, cwid), BF16),
            jax.ShapeDtypeStruct((CONV_HALO, cwid), F32),
            jax.ShapeDtypeStruct((n_heads, nt, 2, tm), F32),
            jax.ShapeDtypeStruct((nt, 2 * n_heads, LANES), F32),
        ),
        scratch_shapes=[pltpu.VMEM((tm + CONV_HALO, cwid), F32),
                        pltpu.VMEM((SUBLANES - 1, tm + CONV_HALO - SUBLANES, cwid), F32)],
        name="inproj_prompt",
        compiler_params=_params(("arbitrary",)),
    )(x, g, w_in, rope_tile, rope_row, conv_w, conv_b, ln_g, ln_b)


def _inproj_sample_kernel(x_ref, g_ref, w_ref, cos_ref, sin_ref, cc_ref, toep_ref, cb_ref, lng_ref, lnb_ref,
                          kout_ref, vout_ref, q_ref, glu_ref, c_ref, *, aw, cwid, n_heads, scale, n_b, n_t):
    rows = n_b * n_t
    h = _rms(x_ref[...], g_ref[...]).astype(BF16)
    cos = cos_ref[...]
    sin = sin_ref[...]
    lane = lax.broadcasted_iota(jnp.int32, (rows, LANES), 1)
    first_half = (lane % HEAD_DIM) < (HEAD_DIM // 2)
    q = jnp.dot(h, w_ref[:, 0:aw], preferred_element_type=F32)
    k = jnp.dot(h, w_ref[:, aw:2 * aw], preferred_element_type=F32)
    vout_ref[...] = jnp.dot(h, w_ref[:, 2 * aw:3 * aw], preferred_element_type=F32)
    for hd in range(n_heads):
        sl = slice(hd * LANES, (hd + 1) * LANES)
        q_ref[:, sl] = _rope_lanes(q[:, sl], cos, sin, first_half) * scale
        kout_ref[:, sl] = _rope_lanes(k[:, sl], cos, sin, first_half)
    u1 = jnp.dot(h, w_ref[:, 3 * aw:3 * aw + cwid], preferred_element_type=F32)
    u2 = jnp.dot(h, w_ref[:, 3 * aw + cwid:3 * aw + 2 * cwid], preferred_element_type=F32)
    glu_ref[...] = (u1 * jax.nn.sigmoid(u2)).reshape(n_b, n_t, cwid)

    acc = jnp.zeros((n_b, n_t, cwid), F32)
    for r in range(CONV_K - 1):
        acc = acc + cc_ref[:, r:r + 1, :] * toep_ref[r]
    for r in range(n_t):
        acc = acc + glu_ref[:, r:r + 1, :] * toep_ref[CONV_K - 1 + r]
    acc = acc + cb_ref[...]
    mu = jnp.mean(acc, axis=-1, keepdims=True)
    d = acc - mu
    var = jnp.mean(d * d, axis=-1, keepdims=True)
    y = d * lax.rsqrt(var + LN_EPS) * lng_ref[...] + lnb_ref[...]
    c_ref[...] = y * jax.nn.sigmoid(y)


def _inproj_sample(x, g, w_in, cos, sin, cache_conv, toep, conv_b, ln_g, ln_b, *, aw, cwid, n_heads, n_b, n_t):
    rows = n_b * n_t
    kern = functools.partial(_inproj_sample_kernel, aw=aw, cwid=cwid, n_heads=n_heads,
                             scale=HEAD_DIM ** -0.5, n_b=n_b, n_t=n_t)
    return pl.pallas_call(
        kern,
        out_shape=(
            jax.ShapeDtypeStruct((rows, aw), F32),
            jax.ShapeDtypeStruct((rows, aw), F32),
            jax.ShapeDtypeStruct((rows, aw), F32),
            jax.ShapeDtypeStruct((n_b, n_t, cwid), F32),
            jax.ShapeDtypeStruct((n_b, n_t, cwid), F32),
        ),
        name="inproj_sample",
        compiler_params=pltpu.CompilerParams(vmem_limit_bytes=VMEM_LIMIT),
    )(x, g, w_in, cos, sin, cache_conv, toep, conv_b, ln_g, ln_b)


def _masked_queries(q, n_heads, n_t):
    n_rows = n_heads * 2 * n_t
    aw = q.shape[-1]
    qrep = jnp.concatenate([q] * (n_heads * 2), axis=0)
    sel = (lax.broadcasted_iota(jnp.int32, (n_rows, aw), 0) // n_t
           == lax.broadcasted_iota(jnp.int32, (n_rows, aw), 1) // HEAD_DIM)
    return jnp.where(sel, qrep, 0.0).astype(BF16)


def _sample_init_kernel(q_ref, kn_ref, vn_ref, m_ref, l_ref, acc_ref, *, n_heads, n_t, group):
    aw = q_ref.shape[-1]
    n_rows = n_heads * 2 * n_t
    pad = jnp.zeros((LANES - n_t, aw), F32)
    col = lax.broadcasted_iota(jnp.int32, (n_rows, LANES), 1)
    qt = lax.broadcasted_iota(jnp.int32, (n_rows, LANES), 0) % n_t
    for b in range(group):
        rows = slice(b * n_t, (b + 1) * n_t)
        wq = _masked_queries(q_ref[rows, :], n_heads, n_t)
        kn = jnp.concatenate([kn_ref[rows, :], pad], axis=0).astype(BF16)
        vn = jnp.concatenate([vn_ref[rows, :], pad], axis=0).astype(BF16)
        sc = lax.dot_general(wq, kn, (((1,), (1,)), ((), ())), preferred_element_type=F32)
        sc = jnp.where(col <= qt, sc, -jnp.inf)
        m = jnp.max(sc, axis=-1, keepdims=True)
        p = jnp.exp(sc - m)
        m_ref[b] = m
        l_ref[b] = jnp.sum(p, axis=-1, keepdims=True)
        acc_ref[b] = jnp.dot(p.astype(BF16), vn, preferred_element_type=F32)


def _sample_init(q, k_new, v_new, *, n_b, n_t, n_heads):
    aw = q.shape[-1]
    n_rows = n_heads * 2 * n_t
    group = math.gcd(n_b, SUBLANES)
    tok = pl.BlockSpec((group * n_t, aw), lambda g: (g, 0))
    stat = pl.BlockSpec((group, n_rows, 1), lambda g: (g, 0, 0))
    return pl.pallas_call(
        functools.partial(_sample_init_kernel, n_heads=n_heads, n_t=n_t, group=group),
        grid=(n_b // group,),
        in_specs=[tok, tok, tok],
        out_specs=[stat, stat, pl.BlockSpec((group, n_rows, aw), lambda g: (g, 0, 0))],
        out_shape=(jax.ShapeDtypeStruct((n_b, n_rows, 1), F32), jax.ShapeDtypeStruct((n_b, n_rows, 1), F32),
                   jax.ShapeDtypeStruct((n_b, n_rows, aw), F32)),
        name="sample_init",
        compiler_params=_params(("arbitrary",)),
    )(q, k_new, v_new)


def _attn_kernel(pt_ref, th_ref, ti_ref, tt_ref, tl_ref, te_ref, tb_ref, qt_ref, kst_ref, vt_ref, bound_ref,
                 lam_ref, sgcol_ref, sgrow_ref, qs_ref, m0_ref, l0_ref, a0_ref, *rest,
                 n_pages, n_units, units_per_entry, tk, page, n_heads, n_t, lam_init):
    k_pages = rest[:n_pages]
    v_pages = rest[n_pages:2 * n_pages]
    op_ref, os_ref = rest[2 * n_pages:2 * n_pages + 2]
    (m_ref, acc_ref, q_scr, ref_scr, sa_ref, sb_ref, xa_ref, xb_ref, pend_ref,
     kbuf, vbuf, ms_ref, ls_ref, accs_ref) = rest[2 * n_pages + 2:]
    g = pl.program_id(0)
    i = ti_ref[g]
    t = tt_ref[g]
    tq = 2 * tk
    lam = _lam(lam_ref, lam_init)

    unit = jnp.minimum(g, n_units - 1)
    active = g < n_units
    first = unit % units_per_entry == 0

    def sample_scores():
        for p in range(n_pages):
            kbuf[:, p * page:(p + 1) * page] = k_pages[p][0].astype(BF16)
            for hd in range(n_heads):
                vbuf[p * page:(p + 1) * page, hd * V_DIM:(hd + 1) * V_DIM] = (
                    v_pages[p][0, pl.ds(hd, page, stride=n_heads), :].astype(BF16))
        wq = _masked_queries(qs_ref[...], n_heads, n_t)
        sc = jnp.dot(wq, kbuf[...], preferred_element_type=F32)
        return jnp.where(active, sc, -jnp.inf)

    def sample_softmax(sc):
        m_prev = jnp.where(first, m0_ref[0], ms_ref[...])
        l_prev = jnp.where(first, l0_ref[0], ls_ref[...])
        m_new = jnp.maximum(m_prev, jnp.max(sc, axis=-1, keepdims=True))
        p = jnp.exp(sc - m_new)
        alpha = jnp.exp(m_prev - m_new)
        l_new = alpha * l_prev + jnp.sum(p, axis=-1, keepdims=True)
        ms_ref[...] = m_new
        ls_ref[...] = l_new
        return p.astype(BF16), alpha, l_new

    def sample_values(p, alpha, l_new):
        acc = (alpha * jnp.where(first, a0_ref[0], accs_ref[...])
               + jnp.dot(p, vbuf[...], preferred_element_type=F32))
        accs_ref[...] = acc
        o = acc / l_new
        for hd in range(n_heads):
            r0 = hd * 2 * n_t
            sl = slice(hd * V_DIM, (hd + 1) * V_DIM)
            a = o[r0:r0 + n_t, sl] - lam * o[r0 + n_t:r0 + 2 * n_t, sl]
            os_ref[:, sl] = _rms(a, sgrow_ref[...]) * (1.0 - lam_init)

    def scores(j, s_ref, smax_ref, key_offset=None):
        st = jnp.dot(kst_ref[0, j], q_scr[...], preferred_element_type=F32)
        if key_offset is not None:
            keep = (lax.broadcasted_iota(jnp.int32, (tk, tq), 0) + key_offset
                    <= lax.broadcasted_iota(jnp.int32, (tk, tq), 1))
        for c in range(2):
            sc = st[c * tk:(c + 1) * tk, :]
            if key_offset is not None:
                sc = jnp.where(keep, sc, -jnp.inf)
            s_ref[c * tk:(c + 1) * tk, :] = sc
            smax_ref[c] = jnp.max(sc, axis=0, keepdims=True)

    def absorb(j, s_ref, smax_ref):
        vt = vt_ref[0, j]
        for c in range(2):
            sc = s_ref[c * tk:(c + 1) * tk, :]
            m_prev = m_ref[c]
            m_new = jnp.maximum(m_prev, smax_ref[c])
            p = jnp.exp2(sc - m_new).astype(BF16)
            alpha = jnp.exp2(m_prev - m_new)
            acc_ref[c] = alpha * acc_ref[c] + jnp.dot(vt, p, preferred_element_type=F32)
            m_ref[c] = m_new

    def bounded_probs(j, key_offset=None):
        st = jnp.dot(kst_ref[0, j], q_scr[...], preferred_element_type=F32)
        if key_offset is not None:
            keep = (lax.broadcasted_iota(jnp.int32, (tk, tq), 0) + key_offset
                    <= lax.broadcasted_iota(jnp.int32, (tk, tq), 1))
        probs = []
        for c in range(2):
            p = jnp.exp2(st[c * tk:(c + 1) * tk, :] - ref_scr[c])
            if key_offset is not None:
                p = jnp.where(keep, p, 0.0)
            probs.append(p.astype(BF16))
        return probs

    def bounded_values(j, pa, pb):
        vt = jnp.concatenate([vt_ref[0, j], vt_ref[0, j + 1]], axis=1)
        for c in range(2):
            acc_ref[c] += jnp.dot(vt, jnp.concatenate([pa[c], pb[c]], axis=0), preferred_element_type=F32)

    def start_query_block():
        q_scr[:, 0:tk] = qt_ref[0, 0]
        q_scr[:, tk:tq] = qt_ref[0, 1]
        acc_ref[...] = jnp.zeros(acc_ref.shape, F32)

    bounded = tb_ref[g] == 1
    diag = t < 0

    @pl.when(jnp.logical_and(diag, bounded))
    def _():
        start_query_block()
        for c in range(2):
            ref_scr[c, :, 0:tk] = bound_ref[0, 0, c:c + 1, :]
            ref_scr[c, :, tk:tq] = bound_ref[0, 1, c:c + 1, :]
        sc = sample_scores()
        pa = bounded_probs(2 * i, key_offset=0)
        soft = sample_softmax(sc)
        pb = bounded_probs(2 * i + 1, key_offset=tk)
        sample_values(*soft)
        bounded_values(2 * i, pa, pb)

    @pl.when(jnp.logical_and(jnp.logical_not(diag), bounded))
    def _():
        sc = sample_scores()
        pa = bounded_probs(2 * t)
        soft = sample_softmax(sc)
        pb = bounded_probs(2 * t + 1)
        sample_values(*soft)
        bounded_values(2 * t, pa, pb)

    @pl.when(jnp.logical_and(diag, jnp.logical_not(bounded)))
    def _():
        start_query_block()
        m_ref[...] = jnp.full(m_ref.shape, -jnp.inf, F32)
        sc = sample_scores()
        scores(2 * i, sa_ref, xa_ref, key_offset=0)
        soft = sample_softmax(sc)
        scores(2 * i + 1, sb_ref, xb_ref, key_offset=tk)
        sample_values(*soft)
        absorb(2 * i, sa_ref, xa_ref)
        pend_ref[0] = 2 * i + 1

    @pl.when(jnp.logical_and(jnp.logical_not(diag), jnp.logical_not(bounded)))
    def _():
        pending = pend_ref[0]
        sc = sample_scores()
        scores(2 * t, sa_ref, xa_ref)
        soft = sample_softmax(sc)
        absorb(pending, sb_ref, xb_ref)
        sample_values(*soft)
        scores(2 * t + 1, sb_ref, xb_ref)
        absorb(2 * t, sa_ref, xa_ref)
        pend_ref[0] = 2 * t + 1

    last = tl_ref[g] == 1

    @pl.when(jnp.logical_and(last, jnp.logical_not(bounded)))
    def _():
        absorb(pend_ref[0], sb_ref, xb_ref)

    @pl.when(last)
    def _():
        o1 =acc_ref[0, 0:V_DIM, :] / acc_ref[0, V_DIM:V_DIM + 1, :]
        o2 = acc_ref[1, 0:V_DIM, :] / acc_ref[1, V_DIM:V_DIM + 1, :]
        a = o1 - lam * o2
        y = a * lax.rsqrt(jnp.mean(a * a, axis=0, keepdims=True) + RMS_EPS) * sgcol_ref[...] * (1.0 - lam_init)
        op_ref[...] = y.T.astype(op_ref.dtype)


def _prompt_tasks(n_heads, nq):
    tasks = [(h, i, t, int(t == i - 1)) for h in range(n_heads) for i in range(nq) for t in range(-1, i)]
    return tuple(np.asarray(col, np.int32) for col in zip(*tasks))


def _attention(page_table, qt, kst, vt, qn2, kn2, lam_vecs, subln_g, q_s, k_new, v_new, cache_kt, cache_v,
               *, n_b, n_t, lam_init):
    n_heads, nt, _, tk = qt.shape
    assert nt % 2 == 0, "a query block spans two key blocks"
    nq = nt // 2
    tq = 2 * tk
    t = nt * tk
    n_pool, aw, page = cache_kt.shape
    n_rows = n_heads * 2 * n_t
    th, ti, tt, tl = _prompt_tasks(n_heads, nq)
    n_tasks = len(th)
    kmax2 = jnp.max(kn2[:, :, 0], axis=0).reshape(n_heads, 1, 2, 1)
    bound = jnp.sqrt(qn2 * kmax2)
    block_bound = jnp.max(bound.reshape(n_heads, nq, -1), axis=-1)
    tb = (block_bound <= SAFE_BOUND).astype(jnp.int32)[th, ti]
    n_pages_entry = page_table.shape[1]
    pages_per_unit = min(p for p in range(1, n_pages_entry + 1)
                         if n_pages_entry % p == 0 and n_b * (n_pages_entry // p) <= n_tasks)
    units_per_entry = n_pages_entry // pages_per_unit
    n_units = n_b * units_per_entry
    m0, l0, a0 = _sample_init(q_s, k_new, v_new, n_b=n_b, n_t=n_t, n_heads=n_heads)
    kern = functools.partial(_attn_kernel, n_pages=pages_per_unit, n_units=n_units,
                             units_per_entry=units_per_entry, tk=tk, page=page, n_heads=n_heads, n_t=n_t,
                             lam_init=lam_init)

    unit_of_step = np.minimum(np.arange(n_tasks), n_units - 1)
    te = (unit_of_step // units_per_entry).astype(np.int32)
    step_pages = page_table.reshape(n_units, pages_per_unit)[unit_of_step].reshape(-1)

    tok = pl.BlockSpec((n_t, aw), lambda g, pt, th, ti, tt, tl, te, tb: (te[g], 0))
    stat = pl.BlockSpec((1, n_rows, 1), lambda g, pt, th, ti, tt, tl, te, tb: (te[g], 0, 0))

    def page_spec(p, shape):
        return pl.BlockSpec((1,) + shape, lambda g, pt, *_: (pt[g * pages_per_unit + p], 0, 0))

    k_specs = [page_spec(p, (aw, page)) for p in range(pages_per_unit)]
    v_specs = [page_spec(p, (page * n_heads, V_DIM)) for p in range(pages_per_unit)]
    head_map = lambda g, pt, th, *_: (th[g], 0, 0, 0)
    grid_spec = pltpu.PrefetchScalarGridSpec(
        num_scalar_prefetch=7,
        grid=(n_tasks,),
        in_specs=[
            pl.BlockSpec((1, 2, LANES, tk), lambda g, pt, th, ti, *_: (th[g], ti[g], 0, 0)),
            pl.BlockSpec((1, nt, 2 * tk, LANES), head_map, pipeline_mode=pl.Buffered(1)),
            pl.BlockSpec((1, nt, V_DIM + SUM_ROWS, tk), head_map, pipeline_mode=pl.Buffered(1)),
            pl.BlockSpec((1, 2, 2, tk), lambda g, pt, th, ti, *_: (th[g], ti[g], 0, 0)),
            pl.BlockSpec(lam_vecs.shape, lambda g, *_: (0, 0)),
            pl.BlockSpec((V_DIM, 1), lambda g, *_: (0, 0)),
            pl.BlockSpec((1, V_DIM), lambda g, *_: (0, 0)),
            tok, stat, stat,
            pl.BlockSpec((1, n_rows, aw), lambda g, pt, th, ti, tt, tl, te, tb: (te[g], 0, 0)),
        ] + k_specs + v_specs,
        out_specs=[
            pl.BlockSpec((tq, V_DIM), lambda g, pt, th, ti, *_: (ti[g], th[g])),
            tok,
        ],
        scratch_shapes=[
            pltpu.VMEM((2, 1, tq), F32),
            pltpu.VMEM((2, V_DIM + SUM_ROWS, tq), F32),
            pltpu.VMEM((LANES, tq), BF16),
            pltpu.VMEM((2, 1, tq), F32),
            pltpu.VMEM((2 * tk, tq), F32),
            pltpu.VMEM((2 * tk, tq), F32),
            pltpu.VMEM((2, 1, tq), F32),
            pltpu.VMEM((2, 1, tq), F32),
            pltpu.SMEM((1,), jnp.int32),
            pltpu.VMEM((aw, pages_per_unit * page), BF16),
            pltpu.VMEM((pages_per_unit * page, aw), BF16),
            pltpu.VMEM((n_rows, 1), F32),
            pltpu.VMEM((n_rows, 1), F32),
            pltpu.VMEM((n_rows, aw), F32),
        ],
    )
    return pl.pallas_call(
        kern,
        grid_spec=grid_spec,
        out_shape=(jax.ShapeDtypeStruct((t, n_heads * V_DIM), BF16),
                   jax.ShapeDtypeStruct((n_b * n_t, aw), F32)),
        name="attention",
        compiler_params=_params(("arbitrary",)),
    )(step_pages, th, ti, tt, tl, te, tb, qt, kst, vt, bound, lam_vecs, subln_g.reshape(-1, 1),
      subln_g.reshape(1, -1), q_s, m0, l0, a0, *([cache_kt] * pages_per_unit), *([cache_v] * pages_per_unit))


def _cross_kernel(o_ref, c_ref, x_ref, wout_ref, gpost_ref, gpre_ref, wmq_ref, mk_ref, mv_ref, wmo_ref,
                  gcpost_ref, y_ref, *, mem_heads, row_splits):
    tm, d = x_ref.shape
    hd = d // mem_heads
    n_lt = hd // LANES
    n_mem = mk_ref.shape[1] // (mem_heads * n_lt)

    def head_rows(ref, h):
        return jnp.concatenate(
            [ref[0, pl.ds(lt * mem_heads + h, n_mem, stride=mem_heads * n_lt), :] for lt in range(n_lt)],
            axis=-1).astype(BF16)

    mks = [head_rows(mk_ref, h) for h in range(mem_heads)]
    mvs = [head_rows(mv_ref, h) for h in range(mem_heads)]
    rs = tm // row_splits
    groups = [slice(r * rs, (r + 1) * rs) for r in range(row_splits)]
    scale = hd ** -0.5
    mix = [jnp.dot(jnp.concatenate([o_ref[g, :].astype(BF16), c_ref[g, :].astype(BF16)], axis=-1),
                   wout_ref[...], preferred_element_type=F32) for g in groups]
    x1 = [x_ref[g, :] + _rms(m, gpost_ref[...]) for g, m in zip(groups, mix)]
    q = [(jnp.dot(_rms(x, gpre_ref[...]).astype(BF16), wmq_ref[...], preferred_element_type=F32)
          * scale).astype(BF16) for x in x1]
    outs = [[] for _ in groups]
    for h in range(mem_heads):
        sl = slice(h * hd, (h + 1) * hd)
        s = [lax.dot_general(qr[:, sl], mks[h], (((1,), (1,)), ((), ())), preferred_element_type=F32)
             for qr in q]
        for r, sr in enumerate(s):
            p = jnp.exp(sr - jnp.max(sr, axis=-1, keepdims=True))
            p = p / jnp.sum(p, axis=-1, keepdims=True)
            outs[r].append(jnp.dot(p.astype(BF16), mvs[h], preferred_element_type=F32).astype(BF16))
    att = [jnp.dot(jnp.concatenate(o, axis=-1), wmo_ref[...], preferred_element_type=F32) for o in outs]
    for g, x, a in zip(groups, x1, att):
        y_ref[g, :] = x + _rms(a, gcpost_ref[...])


def _cross(o, c, x, w_out, g_post, g_pre, w_mq, mem_k, mem_v, w_mo, g_cpost, *, tm, mem_heads):
    t, d = x.shape
    aw = o.shape[1]
    cwid = c.shape[1]
    mem_rows, mem_hd = mem_k.shape[1:]
    row = lambda i: (i, 0)
    return pl.pallas_call(
        functools.partial(_cross_kernel, mem_heads=mem_heads, row_splits=1),
        grid=(t // tm,),
        in_specs=[
            pl.BlockSpec((tm, aw), row),
            pl.BlockSpec((tm, cwid), row),
            pl.BlockSpec((tm, d), row),
            _const_spec(w_out.shape),
            _const_spec((1, d)),
            _const_spec((1, d)),
            _const_spec(w_mq.shape),
            _const_spec((1, mem_rows, mem_hd)),
            _const_spec((1, mem_rows, mem_hd)),
            _const_spec(w_mo.shape),
            _const_spec((1, d)),
        ],
        out_specs=pl.BlockSpec((tm, d), row),
        out_shape=jax.ShapeDtypeStruct((t, d), F32),
        name="cross_prompt",
        compiler_params=_params(("arbitrary",)),
    )(o, c, x, w_out, g_post, g_pre, w_mq, mem_k, mem_v, w_mo, g_cpost)


def _cross_sample_kernel(o_ref, c_ref, x_ref, wout_ref, gpost_ref, gpre_ref, wmq_ref, mk_ref, mv_ref, wmo_ref,
                         gcpost_ref, y_ref, x1_scr, q_scr, att_scr, *, mem_heads, n_t):
    b = pl.program_id(0)
    d = x_ref.shape[-1]
    hd = d // mem_heads
    n_lt = hd // LANES
    n_mem = mk_ref.shape[1] // (mem_heads * n_lt)

    @pl.when(b == 0)
    def _():
        mix_in = jnp.concatenate([o_ref[...].astype(BF16), c_ref[...].astype(BF16)], axis=-1)
        mix = jnp.dot(mix_in, wout_ref[...], preferred_element_type=F32)
        x1 = x_ref[...] + _rms(mix, gpost_ref[...])
        x1_scr[...] = x1
        hq = _rms(x1, gpre_ref[...]).astype(BF16)
        q_scr[...] = jnp.dot(hq, wmq_ref[...], preferred_element_type=F32) * (hd ** -0.5)

    def head_rows(ref, h):
        return jnp.concatenate(
            [ref[0, pl.ds(lt * mem_heads + h, n_mem, stride=mem_heads * n_lt), :] for lt in range(n_lt)],
            axis=-1).astype(BF16)

    rows = pl.ds(pl.multiple_of(b * n_t, n_t), n_t)
    q = q_scr[rows, :].astype(BF16)
    for h in range(mem_heads):
        sl = slice(h * hd, (h + 1) * hd)
        s = lax.dot_general(q[:, sl], head_rows(mk_ref, h), (((1,), (1,)), ((), ())),
                            preferred_element_type=F32)
        p = jnp.exp(s - jnp.max(s, axis=-1, keepdims=True))
        p = p / jnp.sum(p, axis=-1, keepdims=True)
        att_scr[rows, sl] = jnp.dot(p.astype(BF16), head_rows(mv_ref, h), preferred_element_type=F32)

    @pl.when(b == pl.num_programs(0) - 1)
    def _():
        att = jnp.dot(att_scr[...].astype(BF16), wmo_ref[...], preferred_element_type=F32)
        y_ref[...] = x1_scr[...] + _rms(att, gcpost_ref[...])


def _cross_sample(o, c, x, w_out, g_post, g_pre, w_mq, mem_k, mem_v, w_mo, g_cpost, *, n_t, mem_heads):
    t, d = x.shape
    n_b, mem_rows, mem_hd = mem_k.shape
    assert n_t == SUBLANES and t == n_b * n_t
    mem_map = lambda b: (b, 0, 0)
    return pl.pallas_call(
        functools.partial(_cross_sample_kernel, mem_heads=mem_heads, n_t=n_t),
        grid=(n_b,),
        in_specs=[
            _const_spec(o.shape),
            _const_spec(c.shape),
            _const_spec(x.shape),
            _const_spec(w_out.shape),
            _const_spec((1, d)),
            _const_spec((1, d)),
            _const_spec(w_mq.shape),
            pl.BlockSpec((1, mem_rows, mem_hd), mem_map),
            pl.BlockSpec((1, mem_rows, mem_hd), mem_map),
            _const_spec(w_mo.shape),
            _const_spec((1, d)),
        ],
        out_specs=pl.BlockSpec((t, d), lambda b: (0, 0)),
        out_shape=jax.ShapeDtypeStruct((t, d), F32),
        scratch_shapes=[pltpu.VMEM((t, d), F32), pltpu.VMEM((t, d), F32), pltpu.VMEM((t, d), F32)],
        name="cross_sample",
        compiler_params=_params(("arbitrary",)),
    )(o, c, x, w_out, g_post, g_pre, w_mq, mem_k, mem_v, w_mo, g_cpost)


def _mlp_kernel(x_ref, gpre_ref, w1_ref, w2_ref, gpost_ref, y_ref):
    x = x_ref[...]
    h = _rms(x, gpre_ref[...]).astype(BF16)
    hf = jnp.dot(h, w1_ref[...], preferred_element_type=F32)
    r = jnp.maximum(hf, 0.0)
    f = jnp.dot((r * r).astype(BF16), w2_ref[...], preferred_element_type=F32)
    y_ref[...] = x + _rms(f, gpost_ref[...])


def _mlp(x, g_pre, w1, w2, g_post, *, tm):
    t, d = x.shape
    row = lambda i: (i, 0)
    return pl.pallas_call(
        _mlp_kernel,
        grid=(t // tm,),
        in_specs=[pl.BlockSpec((tm, d), row), _const_spec((1, d)), _const_spec(w1.shape),
                  _const_spec(w2.shape), _const_spec((1, d))],
        out_specs=pl.BlockSpec((tm, d), row),
        out_shape=jax.ShapeDtypeStruct((t, d), F32),
        name="mlp_tile%d" % tm,
        compiler_params=_params(("arbitrary",)),
    )(x, g_pre, w1, w2, g_post)


def _rope_tables(pos):
    half = HEAD_DIM // 2
    inv = ROPE_THETA ** (-jnp.arange(0, half, dtype=F32) * 2.0 / HEAD_DIM)
    ang = pos.astype(F32)[:, None] * inv[None, :]
    cos = jnp.cos(ang)
    sin = jnp.sin(ang)
    reps = LANES // HEAD_DIM
    return (jnp.tile(jnp.concatenate([cos, cos], axis=-1), (1, reps)),
            jnp.tile(jnp.concatenate([-sin, sin], axis=-1), (1, reps)))


def _conv_toeplitz(conv_w, n_t):
    n_r = CONV_K - 1 + n_t
    r = jnp.arange(n_r)[:, None]
    t = jnp.arange(n_t)[None, :]
    j = r - t
    ok = (j >= 0) & (j < CONV_K)
    return jnp.where(ok[:, :, None], conv_w[jnp.clip(j, 0, CONV_K - 1)], 0.0)


def _layer(i, depth_inputs, x_prompt, x_sample, mem_prompt, cache_k, cache_v, cache_conv, cache_mem_k,
           cache_mem_v, page_table, *, tile):
    (w_in, lq1, lk1, lq2, lk2, subln_g, conv_w, conv_b, ln_g, ln_b, w_out, mix_pre_g, mix_post_g, mem_norm_g,
     w_mq, w_mk, w_mv, w_mo, cross_pre_g, cross_post_g, w_ff1, w_ff2, mlp_pre_g, mlp_post_g) = depth_inputs
    lam_init = 0.8 - 0.6 * math.exp(-0.3 * i)
    _, seq, d = x_prompt.shape
    n_b, n_t, _ = x_sample.shape
    n_heads, v_dim = cache_v.shape[-2:]
    aw = n_heads * v_dim
    cwid = cache_conv.shape[-1]
    mem_heads = cache_mem_k.shape[-2]
    n_mem = mem_prompt.shape[1]
    past = page_table.shape[1] * cache_k.shape[1]

    row2 = lambda a: a.reshape(1, -1)
    w_in_b, w_out_b, w_mq_b, w_mk_b, w_mv_b, w_mo_b, w_ff1_b, w_ff2_b = (
        w.astype(BF16) for w in (w_in, w_out, w_mq, w_mk, w_mv, w_mo, w_ff1, w_ff2))
    lam_vecs = jnp.stack([lq1, lk1, lq2, lk2])

    mk_p, mv_p = _memory_kv(mem_prompt[0], row2(mem_norm_g), w_mk_b, w_mv_b, mem_heads=mem_heads)
    rope_tile = jnp.stack(_rope_tables(jnp.arange(0, seq, tile, dtype=jnp.int32)))
    rope_row = jnp.stack(_rope_tables(jnp.arange(tile, dtype=jnp.int32)))
    k_p, v_p, qt, kst, vt, c_p, tail_p, qn2, kn2 = _inproj_prompt(
        x_prompt[0], row2(mix_pre_g), w_in_b, rope_tile, rope_row, conv_w, row2(conv_b), row2(ln_g), row2(ln_b),
        tm=tile, aw=aw, cwid=cwid, n_heads=n_heads)

    pos_s = jnp.tile(past + jnp.arange(n_t, dtype=jnp.int32), n_b)
    cos_s, sin_s = _rope_tables(pos_s)
    xs = x_sample.reshape(n_b * n_t, d)
    k_s, v_s, q_s, glu_s, c_s = _inproj_sample(
        xs, row2(mix_pre_g), w_in_b, cos_s, sin_s, cache_conv, _conv_toeplitz(conv_w, n_t), row2(conv_b),
        row2(ln_g), row2(ln_b), aw=aw, cwid=cwid, n_heads=n_heads, n_b=n_b, n_t=n_t)
    n_pool, page = cache_k.shape[:2]
    cache_kt = jnp.transpose(cache_k, (0, 2, 3, 4, 1)).reshape(n_pool, aw, page)
    cache_vr = cache_v.reshape(n_pool, page * n_heads, v_dim)
    o_p, o_s = _attention(page_table, qt, kst, vt, qn2, kn2, lam_vecs, subln_g, q_s, k_s, v_s, cache_kt, cache_vr,
                          n_b=n_b, n_t=n_t, lam_init=lam_init)
    x2_p = _cross(o_p, c_p, x_prompt[0], w_out_b, row2(mix_post_g), row2(cross_pre_g), w_mq_b,
                  mk_p[None], mv_p[None], w_mo_b, row2(cross_post_g), tm=tile, mem_heads=mem_heads)
    y_p = _mlp(x2_p, row2(mlp_pre_g), w_ff1_b, w_ff2_b, row2(mlp_post_g), tm=tile)
    n_lt = d // mem_heads // LANES

    def mem_rows(a):
        a = a.reshape(a.shape[0], n_mem, mem_heads, n_lt, LANES)
        return jnp.transpose(a, (0, 1, 3, 2, 4)).reshape(a.shape[0], n_mem * n_lt * mem_heads, LANES)

    x2_s = _cross_sample(o_s, c_s.reshape(n_b * n_t, cwid), xs, w_out_b, row2(mix_post_g), row2(cross_pre_g),
                         w_mq_b, mem_rows(cache_mem_k), mem_rows(cache_mem_v), w_mo_b,
                         row2(cross_post_g), n_t=n_t, mem_heads=mem_heads)

    def mem_out(a):
        a = a.reshape(n_mem, n_lt, mem_heads, LANES)
        return jnp.transpose(a, (0, 2, 1, 3)).reshape(1, n_mem, mem_heads, n_lt * LANES)
    y_s = _mlp(x2_s, row2(mlp_pre_g), w_ff1_b, w_ff2_b, row2(mlp_post_g), tm=n_b * n_t)

    hd2 = (n_heads, 2, HEAD_DIM)
    outs = dict(
        y_p=y_p[None], y_s=y_s.reshape(n_b, n_t, d),
        k_p=jnp.transpose(k_p.reshape(*hd2, seq), (3, 0, 1, 2))[None],
        v_p=v_p.reshape(1, seq, n_heads, v_dim),
        conv_p=tail_p[CONV_HALO - (CONV_K - 1):][None],
        mk_p=mem_out(mk_p), mv_p=mem_out(mv_p),
        k_s=k_s.reshape(n_b, n_t, *hd2), v_s=v_s.reshape(n_b, n_t, n_heads, v_dim),
        conv_s=jnp.concatenate([cache_conv[:, n_t:], glu_s], axis=1),
    )
    return outs


def kernel(x_prompt, x_sample, mem_prompt, cache_k, cache_v, cache_conv, cache_mem_k, cache_mem_v, page_table, w_in, lambda_q1, lambda_k1, lambda_q2, lambda_k2, subln_g, conv_w, conv_b, conv_ln_g, conv_ln_b, w_out, mix_pre_g, mix_post_g, mem_norm_g, w_mq, w_mk, w_mv, w_mo, cross_pre_g, cross_post_g, w_ff1, w_ff2, mlp_pre_g, mlp_post_g):
    depth = w_in.shape[0]
    stacked = (w_in, lambda_q1, lambda_k1, lambda_q2, lambda_k2, subln_g, conv_w, conv_b, conv_ln_g, conv_ln_b,
               w_out, mix_pre_g, mix_post_g, mem_norm_g, w_mq, w_mk, w_mv, w_mo, cross_pre_g, cross_post_g,
               w_ff1, w_ff2, mlp_pre_g, mlp_post_g)
    tile = min(512, x_prompt.shape[1])
    y_p, y_s = x_prompt, x_sample
    per_layer = []
    for i in range(depth):
        o = _layer(i, tuple(a[i] for a in stacked), y_p, y_s, mem_prompt, cache_k[i], cache_v[i], cache_conv[i],
                   cache_mem_k[i], cache_mem_v[i], page_table, tile=tile)
        y_p, y_s = o["y_p"], o["y_s"]
        per_layer.append(o)
    stack = lambda name: jnp.stack([o[name] for o in per_layer], axis=0)
    return (y_p, y_s, stack("k_p"), stack("v_p"), stack("conv_p"), stack("mk_p"), stack("mv_p"),
            stack("k_s"), stack("v_s"), stack("conv_s"))
```

```python
import functools
import math

import jax
import numpy as np
import jax.numpy as jnp
from jax import lax
from jax.experimental import pallas as pl
from jax.experimental.pallas import tpu as pltpu

F32 = jnp.float32
BF16 = jnp.bfloat16

RMS_EPS = 1e-6
LN_EPS = 1e-5
ROPE_THETA = 10000.0
LANES = 128
SUBLANES = 8
HEAD_DIM = 64
V_DIM = 2 * HEAD_DIM
CONV_K = 31
CONV_HALO = 32
SUM_ROWS = 16
SAFE_BOUND = 60.0
VMEM_LIMIT = 56 * 1024 * 1024


def _rms(x, g):
    return x * lax.rsqrt(jnp.mean(x * x, axis=-1, keepdims=True) + RMS_EPS) * g


def _lam(lam_ref, lam_init):
    a = jnp.sum(lam_ref[0:1, :] * lam_ref[1:2, :], axis=-1, keepdims=True)
    b = jnp.sum(lam_ref[2:3, :] * lam_ref[3:4, :], axis=-1, keepdims=True)
    return jnp.exp(a) - jnp.exp(b) + lam_init


def _rope_lanes(x, cos, sin_signed, first_half):
    rot = jnp.where(first_half, pltpu.roll(x, LANES - HEAD_DIM // 2, 1), pltpu.roll(x, HEAD_DIM // 2, 1))
    return x * cos + rot * sin_signed


def _const_spec(shape):
    nd = len(shape)
    return pl.BlockSpec(shape, lambda *_: (0,) * nd, pipeline_mode=pl.Buffered(1))


def _params(sem):
    return pltpu.CompilerParams(dimension_semantics=sem, vmem_limit_bytes=VMEM_LIMIT)


def _memory_kv_kernel(mem_ref, g_ref, wk_ref, wv_ref, mk_ref, mv_ref, *, mem_heads):
    m = _rms(mem_ref[...], g_ref[...]).astype(BF16)
    n, d = mem_ref.shape
    hd = d // mem_heads
    mk = jnp.dot(m, wk_ref[...], preferred_element_type=F32)
    mv = jnp.dot(m, wv_ref[...], preferred_element_type=F32)
    n_lt = hd // LANES
    for h in range(mem_heads):
        for lt in range(n_lt):
            rows = pl.ds(lt * mem_heads + h, n, stride=mem_heads * n_lt)
            col = h * hd + lt * LANES
            mk_ref[rows, :] = mk[:, col:col + LANES]
            mv_ref[rows, :] = mv[:, col:col + LANES]


def _memory_kv(mem, g, wk, wv, *, mem_heads):
    n, d = mem.shape
    out = jax.ShapeDtypeStruct((n * d // LANES, LANES), F32)
    return pl.pallas_call(
        functools.partial(_memory_kv_kernel, mem_heads=mem_heads),
        out_shape=(out, out),
        name="memory_kv",
        compiler_params=pltpu.CompilerParams(vmem_limit_bytes=VMEM_LIMIT),
    )(mem, g, wk, wv)


def _inproj_prompt_kernel(x_ref, g_ref, w_ref, rope_tile_ref, rope_row_ref, cw_ref, cb_ref, lng_ref, lnb_ref,
                          kout_ref, vout_ref, qt_ref, kst_ref, vt_ref, c_ref, tail_ref, qn2_ref, kn2_ref,
                          gpad_ref, gsh_ref, *, tm, aw, cwid, n_heads, scale, row_chunk):
    i = pl.program_id(0)

    @pl.when(i == 0)
    def _():
        gpad_ref[0:CONV_HALO, :] = jnp.zeros((CONV_HALO, cwid), F32)

    h = _rms(x_ref[...], g_ref[...]).astype(BF16)
    cos_a = rope_tile_ref[0, pl.ds(i, 1), :]
    sin_a = rope_tile_ref[1, pl.ds(i, 1), :]
    cos_b = rope_row_ref[0]
    sin_b = rope_row_ref[1]
    cos = cos_a * cos_b - sin_a * sin_b
    sin = sin_a * cos_b + cos_a * sin_b
    lane = lax.broadcasted_iota(jnp.int32, (tm, LANES), 1)
    first_half = (lane % HEAD_DIM) < (HEAD_DIM // 2)
    lo = lane < HEAD_DIM
    sum_rows = (lax.broadcasted_iota(jnp.int32, (SUM_ROWS, tm), 0) == 0).astype(BF16)

    q = jnp.dot(h, w_ref[:, 0:aw], preferred_element_type=F32)
    k = jnp.dot(h, w_ref[:, aw:2 * aw], preferred_element_type=F32)
    v = jnp.dot(h, w_ref[:, 2 * aw:3 * aw], preferred_element_type=F32)
    for hd in range(n_heads):
        sl = slice(hd * LANES, (hd + 1) * LANES)
        qh = _rope_lanes(q[:, sl], cos, sin, first_half) * scale
        kh = _rope_lanes(k[:, sl], cos, sin, first_half)
        kht = kh.T
        qht = qh.T
        kout_ref[sl, :] = kht
        vout_ref[pl.ds(hd, tm, stride=n_heads), :] = v[:, sl]
        qt_ref[hd, 0] = qht.astype(BF16)
        for c in range(2):
            half = slice(c * HEAD_DIM, (c + 1) * HEAD_DIM)
            qn2_ref[hd, 0, c:c + 1, :] = jnp.sum(qht[half, :] * qht[half, :], axis=0, keepdims=True)
            kn2 = jnp.max(jnp.sum(kht[half, :] * kht[half, :], axis=0, keepdims=True), axis=1, keepdims=True)
            kn2_ref[0, 2 * hd + c:2 * hd + c + 1, :] = jnp.broadcast_to(kn2, (1, LANES))
        kst_ref[hd, 0, 0:tm, :] = jnp.where(lo, kh, 0.0).astype(BF16)
        kst_ref[hd, 0, tm:2 * tm, :] = jnp.where(lo, 0.0, kh).astype(BF16)
        vt_ref[hd, 0, 0:V_DIM, :] = v[:, sl].T.astype(BF16)
        vt_ref[hd, 0, V_DIM:V_DIM + SUM_ROWS, :] = sum_rows

    u1 = jnp.dot(h, w_ref[:, 3 * aw:3 * aw + cwid], preferred_element_type=F32)
    u2 = jnp.dot(h, w_ref[:, 3 * aw + cwid:3 * aw + 2 * cwid], preferred_element_type=F32)
    gpad_ref[CONV_HALO:CONV_HALO + tm, :] = u1 * jax.nn.sigmoid(u2)

    base = CONV_HALO - (CONV_K - 1)
    cb = cb_ref[...]
    lng = lng_ref[...]
    lnb = lnb_ref[...]
    n_sh = gsh_ref.shape[1]
    for b in range(1, SUBLANES):
        gsh_ref[b - 1] = gpad_ref[pl.ds(b, n_sh), :]
    for r0 in range(0, tm, row_chunk):
        acc = jnp.zeros((row_chunk, cwid), F32)
        for j in range(CONV_K):
            a, b = divmod(base + j, SUBLANES)
            rows = pl.ds(r0 + a * SUBLANES, row_chunk)
            tap = gpad_ref[rows, :] if b == 0 else gsh_ref[b - 1, rows, :]
            acc = acc + cw_ref[j:j + 1, :] * tap
        acc = acc + cb
        mu = jnp.mean(acc, axis=-1, keepdims=True)
        d = acc - mu
        var = jnp.mean(d * d, axis=-1, keepdims=True)
        y = d * lax.rsqrt(var + LN_EPS) * lng + lnb
        c_ref[r0:r0 + row_chunk, :] = (y * jax.nn.sigmoid(y)).astype(c_ref.dtype)

    tail_ref[...] = gpad_ref[tm:tm + CONV_HALO, :]
    gpad_ref[0:CONV_HALO, :] = gpad_ref[tm:tm + CONV_HALO, :]


def _inproj_prompt(x, g, w_in, rope_tile, rope_row, conv_w, conv_b, ln_g, ln_b, *, tm, aw, cwid, n_heads):
    t, d = x.shape
    nt = t // tm
    kern = functools.partial(_inproj_prompt_kernel, tm=tm, aw=aw, cwid=cwid, n_heads=n_heads,
                             scale=HEAD_DIM ** -0.5 * math.log2(math.e), row_chunk=min(tm, 32))
    row = lambda i: (i, 0)
    per_head = lambda i: (0, i, 0, 0)
    return pl.pallas_call(
        kern,
        grid=(nt,),
        in_specs=[
            pl.BlockSpec((tm, d), row),
            _const_spec((1, d)),
            _const_spec(w_in.shape),
            _const_spec(rope_tile.shape),
            _const_spec(rope_row.shape),
            _const_spec(conv_w.shape),
            _const_spec((1, cwid)),
            _const_spec((1, cwid)),
            _const_spec((1, cwid)),
        ],
        out_specs=[
            pl.BlockSpec((aw, tm), lambda i: (0, i)),
            pl.BlockSpec((tm * n_heads, V_DIM), row),
            pl.BlockSpec((n_heads, 1, LANES, tm), per_head),
            pl.BlockSpec((n_heads, 1, 2 * tm, LANES), per_head),
            pl.BlockSpec((n_heads, 1, V_DIM + SUM_ROWS, tm), per_head),
            pl.BlockSpec((tm, cwid), row),
            pl.BlockSpec((CONV_HALO, cwid), lambda i: (0, 0)),
            pl.BlockSpec((n_heads, 1, 2, tm), per_head),
            pl.BlockSpec((1, 2 * n_heads, LANES), lambda i: (i, 0, 0)),
        ],
        out_shape=(
            jax.ShapeDtypeStruct((aw, t), F32),
            jax.ShapeDtypeStruct((t * n_heads, V_DIM), F32),
            jax.ShapeDtypeStruct((n_heads, nt, LANES, tm), BF16),
            jax.ShapeDtypeStruct((n_heads, nt, 2 * tm, LANES), BF16),
            jax.ShapeDtypeStruct((n_heads, nt, V_DIM + SUM_ROWS, tm), BF16),
            jax.ShapeDtypeStruct((t, cwid), BF16),
            jax.ShapeDtypeStruct((CONV_HALO, cwid), F32),
            jax.ShapeDtypeStruct((n_heads, nt, 2, tm), F32),
            jax.ShapeDtypeStruct((nt, 2 * n_heads, LANES), F32),
        ),
        scratch_shapes=[pltpu.VMEM((tm + CONV_HALO, cwid), F32),
                        pltpu.VMEM((SUBLANES - 1, tm + CONV_HALO - SUBLANES, cwid), F32)],
        name="inproj_prompt",
        compiler_params=_params(("arbitrary",)),
    )(x, g, w_in, rope_tile, rope_row, conv_w, conv_b, ln_g, ln_b)


def _inproj_sample_kernel(x_ref, g_ref, w_ref, cos_ref, sin_ref, cc_ref, toep_ref, cb_ref, lng_ref, lnb_ref,
                          kout_ref, vout_ref, q_ref, glu_ref, c_ref, *, aw, cwid, n_heads, scale, n_b, n_t):
    rows = n_b * n_t
    h = _rms(x_ref[...], g_ref[...]).astype(BF16)
    cos = cos_ref[...]
    sin = sin_ref[...]
    lane = lax.broadcasted_iota(jnp.int32, (rows, LANES), 1)
    first_half = (lane % HEAD_DIM) < (HEAD_DIM // 2)
    q = jnp.dot(h, w_ref[:, 0:aw], preferred_element_type=F32)
    k = jnp.dot(h, w_ref[:, aw:2 * aw], preferred_element_type=F32)
    vout_ref[...] = jnp.dot(h, w_ref[:, 2 * aw:3 * aw], preferred_element_type=F32)
    for hd in range(n_heads):
        sl = slice(hd * LANES, (hd + 1) * LANES)
        q_ref[:, sl] = _rope_lanes(q[:, sl], cos, sin, first_half) * scale
        kout_ref[:, sl] = _rope_lanes(k[:, sl], cos, sin, first_half)
    u1 = jnp.dot(h, w_ref[:, 3 * aw:3 * aw + cwid], preferred_element_type=F32)
    u2 = jnp.dot(h, w_ref[:, 3 * aw + cwid:3 * aw + 2 * cwid], preferred_element_type=F32)
    glu_ref[...] = (u1 * jax.nn.sigmoid(u2)).reshape(n_b, n_t, cwid)

    acc = jnp.zeros((n_b, n_t, cwid), F32)
    for r in range(CONV_K - 1):
        acc = acc + cc_ref[:, r:r + 1, :] * toep_ref[r]
    for r in range(n_t):
        acc = acc + glu_ref[:, r:r + 1, :] * toep_ref[CONV_K - 1 + r]
    acc = acc + cb_ref[...]
    mu = jnp.mean(acc, axis=-1, keepdims=True)
    d = acc - mu
    var = jnp.mean(d * d, axis=-1, keepdims=True)
    y = d * lax.rsqrt(var + LN_EPS) * lng_ref[...] + lnb_ref[...]
    c_ref[...] = y * jax.nn.sigmoid(y)


def _inproj_sample(x, g, w_in, cos, sin, cache_conv, toep, conv_b, ln_g, ln_b, *, aw, cwid, n_heads, n_b, n_t):
    rows = n_b * n_t
    kern = functools.partial(_inproj_sample_kernel, aw=aw, cwid=cwid, n_heads=n_heads,
                             scale=HEAD_DIM ** -0.5, n_b=n_b, n_t=n_t)
    return pl.pallas_call(
        kern,
        out_shape=(
            jax.ShapeDtypeStruct((rows, aw), F32),
            jax.ShapeDtypeStruct((rows, aw), F32),
            jax.ShapeDtypeStruct((rows, aw), F32),
            jax.ShapeDtypeStruct((n_b, n_t, cwid), F32),
            jax.ShapeDtypeStruct((n_b, n_t, cwid), F32),
        ),
        name="inproj_sample",
        compiler_params=pltpu.CompilerParams(vmem_limit_bytes=VMEM_LIMIT),
    )(x, g, w_in, cos, sin, cache_conv, toep, conv_b, ln_g, ln_b)


def _masked_queries(q, n_heads, n_t):
    n_rows = n_heads * 2 * n_t
    aw = q.shape[-1]
    qrep = jnp.concatenate([q] * (n_heads * 2), axis=0)
    sel = (lax.broadcasted_iota(jnp.int32, (n_rows, aw), 0) // n_t
           == lax.broadcasted_iota(jnp.int32, (n_rows, aw), 1) // HEAD_DIM)
    return jnp.where(sel, qrep, 0.0).astype(BF16)


def _sample_init_kernel(q_ref, kn_ref, vn_ref, m_ref, l_ref, acc_ref, *, n_heads, n_t, group):
    aw = q_ref.shape[-1]
    n_rows = n_heads * 2 * n_t
    pad = jnp.zeros((LANES - n_t, aw), F32)
    col = lax.broadcasted_iota(jnp.int32, (n_rows, LANES), 1)
    qt = lax.broadcasted_iota(jnp.int32, (n_rows, LANES), 0) % n_t
    for b in range(group):
        rows = slice(b * n_t, (b + 1) * n_t)
        wq = _masked_queries(q_ref[rows, :], n_heads, n_t)
        kn = jnp.concatenate([kn_ref[rows, :], pad], axis=0).astype(BF16)
        vn = jnp.concatenate([vn_ref[rows, :], pad], axis=0).astype(BF16)
        sc = lax.dot_general(wq, kn, (((1,), (1,)), ((), ())), preferred_element_type=F32)
        sc = jnp.where(col <= qt, sc, -jnp.inf)
        m = jnp.max(sc, axis=-1, keepdims=True)
        p = jnp.exp(sc - m)
        m_ref[b] = m
        l_ref[b] = jnp.sum(p, axis=-1, keepdims=True)
        acc_ref[b] = jnp.dot(p.astype(BF16), vn, preferred_element_type=F32)


def _sample_init(q, k_new, v_new, *, n_b, n_t, n_heads):
    aw = q.shape[-1]
    n_rows = n_heads * 2 * n_t
    group = math.gcd(n_b, SUBLANES)
    tok = pl.BlockSpec((group * n_t, aw), lambda g: (g, 0))
    stat = pl.BlockSpec((group, n_rows, 1), lambda g: (g, 0, 0))
    return pl.pallas_call(
        functools.partial(_sample_init_kernel, n_heads=n_heads, n_t=n_t, group=group),
        grid=(n_b // group,),
        in_specs=[tok, tok, tok],
        out_specs=[stat, stat, pl.BlockSpec((group, n_rows, aw), lambda g: (g, 0, 0))],
        out_shape=(jax.ShapeDtypeStruct((n_b, n_rows, 1), F32), jax.ShapeDtypeStruct((n_b, n_rows, 1), F32),
                   jax.ShapeDtypeStruct((n_b, n_rows, aw), F32)),
        name="sample_init",
        compiler_params=_params(("arbitrary",)),
    )(q, k_new, v_new)


def _attn_kernel(pt_ref, th_ref, ti_ref, tt_ref, tl_ref, te_ref, tb_ref, qt_ref, kst_ref, vt_ref, bound_ref,
                 lam_ref, sgcol_ref, sgrow_ref, qs_ref, m0_ref, l0_ref, a0_ref, *rest,
                 n_pages, n_units, units_per_entry, tk, page, n_heads, n_t, lam_init):
    kt_hbm, v_hbm, op_ref, os_ref = rest[:4]
    (m_ref, acc_ref, q_scr, ref_scr, sa_ref, sb_ref, xa_ref, xb_ref, pend_ref,
     kpg, vpg, page_sem, kbuf, vbuf, ms_ref, ls_ref, accs_ref) = rest[4:]
    g = pl.program_id(0)
    n_steps = pl.num_programs(0)
    i = ti_ref[g]
    t = tt_ref[g]
    tq = 2 * tk
    lam = _lam(lam_ref, lam_init)

    def page_copies(step, slot):
        copies = []
        for p in range(n_pages):
            pid = pt_ref[step * n_pages + p]
            copies.append(pltpu.make_async_copy(kt_hbm.at[pid], kpg.at[slot, p], page_sem.at[slot]))
            copies.append(pltpu.make_async_copy(v_hbm.at[pid], vpg.at[slot, p], page_sem.at[slot]))
        return copies

    slot = g % 2

    @pl.when(g == 0)
    def _():
        for cp in page_copies(0, 0):
            cp.start()

    @pl.when(g + 1 < n_steps)
    def _():
        for cp in page_copies(g + 1, 1 - slot):
            cp.start()

    for cp in page_copies(g, slot):
        cp.wait()

    unit = jnp.minimum(g, n_units - 1)
    active = g < n_units
    first = unit % units_per_entry == 0

    def sample_scores():
        for p in range(n_pages):
            kbuf[:, p * page:(p + 1) * page] = kpg[slot, p].astype(BF16)
            for hd in range(n_heads):
                vbuf[p * page:(p + 1) * page, hd * V_DIM:(hd + 1) * V_DIM] = (
                    vpg[slot, p, pl.ds(hd, page, stride=n_heads), :].astype(BF16))
        wq = _masked_queries(qs_ref[...], n_heads, n_t)
        sc = jnp.dot(wq, kbuf[...], preferred_element_type=F32)
        return jnp.where(active, sc, -jnp.inf)

    def sample_softmax(sc):
        m_prev = jnp.where(first, m0_ref[0], ms_ref[...])
        l_prev = jnp.where(first, l0_ref[0], ls_ref[...])
        m_new = jnp.maximum(m_prev, jnp.max(sc, axis=-1, keepdims=True))
        p = jnp.exp(sc - m_new)
        alpha = jnp.exp(m_prev - m_new)
        l_new = alpha * l_prev + jnp.sum(p, axis=-1, keepdims=True)
        ms_ref[...] = m_new
        ls_ref[...] = l_new
        return p.astype(BF16), alpha, l_new

    def sample_values(p, alpha, l_new):
        acc = (alpha * jnp.where(first, a0_ref[0], accs_ref[...])
               + jnp.dot(p, vbuf[...], preferred_element_type=F32))
        accs_ref[...] = acc
        o = acc / l_new
        for hd in range(n_heads):
            r0 = hd * 2 * n_t
            sl = slice(hd * V_DIM, (hd + 1) * V_DIM)
            a = o[r0:r0 + n_t, sl] - lam * o[r0 + n_t:r0 + 2 * n_t, sl]
            os_ref[:, sl] = _rms(a, sgrow_ref[...]) * (1.0 - lam_init)

    def scores(j, s_ref, smax_ref, key_offset=None):
        st = jnp.dot(kst_ref[0, j], q_scr[...], preferred_element_type=F32)
        if key_offset is not None:
            keep = (lax.broadcasted_iota(jnp.int32, (tk, tq), 0) + key_offset
                    <= lax.broadcasted_iota(jnp.int32, (tk, tq), 1))
        for c in range(2):
            sc = st[c * tk:(c + 1) * tk, :]
            if key_offset is not None:
                sc = jnp.where(keep, sc, -jnp.inf)
            s_ref[c * tk:(c + 1) * tk, :] = sc
            smax_ref[c] = jnp.max(sc, axis=0, keepdims=True)

    def absorb(j, s_ref, smax_ref):
        vt = vt_ref[0, j]
        for c in range(2):
            sc = s_ref[c * tk:(c + 1) * tk, :]
            m_prev = m_ref[c]
            m_new = jnp.maximum(m_prev, smax_ref[c])
            p = jnp.exp2(sc - m_new).astype(BF16)
            alpha = jnp.exp2(m_prev - m_new)
            acc_ref[c] = alpha * acc_ref[c] + jnp.dot(vt, p, preferred_element_type=F32)
            m_ref[c] = m_new

    def bounded_probs(j, key_offset=None):
        st = jnp.dot(kst_ref[0, j], q_scr[...], preferred_element_type=F32)
        if key_offset is not None:
            keep = (lax.broadcasted_iota(jnp.int32, (tk, tq), 0) + key_offset
                    <= lax.broadcasted_iota(jnp.int32, (tk, tq), 1))
        probs = []
        for c in range(2):
            p = jnp.exp2(st[c * tk:(c + 1) * tk, :] - ref_scr[c])
            if key_offset is not None:
                p = jnp.where(keep, p, 0.0)
            probs.append(p.astype(BF16))
        return probs

    def bounded_values(j, pa, pb):
        vt = jnp.concatenate([vt_ref[0, j], vt_ref[0, j + 1]], axis=1)
        for c in range(2):
            acc_ref[c] += jnp.dot(vt, jnp.concatenate([pa[c], pb[c]], axis=0), preferred_element_type=F32)

    def start_query_block():
        q_scr[:, 0:tk] = qt_ref[0, 0]
        q_scr[:, tk:tq] = qt_ref[0, 1]
        acc_ref[...] = jnp.zeros(acc_ref.shape, F32)

    bounded = tb_ref[g] == 1
    diag = t < 0

    @pl.when(jnp.logical_and(diag, bounded))
    def _():
        start_query_block()
        for c in range(2):
            ref_scr[c, :, 0:tk] = bound_ref[0, 0, c:c + 1, :]
            ref_scr[c, :, tk:tq] = bound_ref[0, 1, c:c + 1, :]
        sc = sample_scores()
        pa = bounded_probs(2 * i, key_offset=0)
        soft = sample_softmax(sc)
        pb = bounded_probs(2 * i + 1, key_offset=tk)
        sample_values(*soft)
        bounded_values(2 * i, pa, pb)

    @pl.when(jnp.logical_and(jnp.logical_not(diag), bounded))
    def _():
        sc = sample_scores()
        pa = bounded_probs(2 * t)
        soft = sample_softmax(sc)
        pb = bounded_probs(2 * t + 1)
        sample_values(*soft)
        bounded_values(2 * t, pa, pb)

    @pl.when(jnp.logical_and(diag, jnp.logical_not(bounded)))
    def _():
        start_query_block()
        m_ref[...] = jnp.full(m_ref.shape, -jnp.inf, F32)
        sc = sample_scores()
        scores(2 * i, sa_ref, xa_ref, key_offset=0)
        soft = sample_softmax(sc)
        scores(2 * i + 1, sb_ref, xb_ref, key_offset=tk)
        sample_values(*soft)
        absorb(2 * i, sa_ref, xa_ref)
        pend_ref[0] = 2 * i + 1

    @pl.when(jnp.logical_and(jnp.logical_not(diag), jnp.logical_not(bounded)))
    def _():
        pending = pend_ref[0]
        sc = sample_scores()
        scores(2 * t, sa_ref, xa_ref)
        soft = sample_softmax(sc)
        absorb(pending, sb_ref, xb_ref)
        sample_values(*soft)
        scores(2 * t + 1, sb_ref, xb_ref)
        absorb(2 * t, sa_ref, xa_ref)
        pend_ref[0] = 2 * t + 1

    last = tl_ref[g] == 1

    @pl.when(jnp.logical_and(last, jnp.logical_not(bounded)))
    def _():
        absorb(pend_ref[0], sb_ref, xb_ref)

    @pl.when(last)
    def _():
        o1 =acc_ref[0, 0:V_DIM, :] / acc_ref[0, V_DIM:V_DIM + 1, :]
        o2 = acc_ref[1, 0:V_DIM, :] / acc_ref[1, V_DIM:V_DIM + 1, :]
        a = o1 - lam * o2
        y = a * lax.rsqrt(jnp.mean(a * a, axis=0, keepdims=True) + RMS_EPS) * sgcol_ref[...] * (1.0 - lam_init)
        op_ref[...] = y.T.astype(op_ref.dtype)


def _prompt_tasks(n_heads, nq):
    tasks = [(h, i, t, int(t == i - 1)) for h in range(n_heads) for i in range(nq) for t in range(-1, i)]
    return tuple(np.asarray(col, np.int32) for col in zip(*tasks))


def _attention(page_table, qt, kst, vt, qn2, kn2, lam_vecs, subln_g, q_s, k_new, v_new, cache_kt, cache_v,
               *, n_b, n_t, lam_init):
    n_heads, nt, _, tk = qt.shape
    assert nt % 2 == 0, "a query block spans two key blocks"
    nq = nt // 2
    tq = 2 * tk
    t = nt * tk
    n_pool, aw, page = cache_kt.shape
    n_rows = n_heads * 2 * n_t
    th, ti, tt, tl = _prompt_tasks(n_heads, nq)
    n_tasks = len(th)
    kmax2 = jnp.max(kn2[:, :, 0], axis=0).reshape(n_heads, 1, 2, 1)
    bound = jnp.sqrt(qn2 * kmax2)
    block_bound = jnp.max(bound.reshape(n_heads, nq, -1), axis=-1)
    tb = (block_bound <= SAFE_BOUND).astype(jnp.int32)[th, ti]
    n_pages_entry = page_table.shape[1]
    pages_per_unit = min(p for p in range(1, n_pages_entry + 1)
                         if n_pages_entry % p == 0 and n_b * (n_pages_entry // p) <= n_tasks)
    units_per_entry = n_pages_entry // pages_per_unit
    n_units = n_b * units_per_entry
    m0, l0, a0 = _sample_init(q_s, k_new, v_new, n_b=n_b, n_t=n_t, n_heads=n_heads)
    kern = functools.partial(_attn_kernel, n_pages=pages_per_unit, n_units=n_units,
                             units_per_entry=units_per_entry, tk=tk, page=page, n_heads=n_heads, n_t=n_t,
                             lam_init=lam_init)

    unit_of_step = np.minimum(np.arange(n_tasks), n_units - 1)
    te = (unit_of_step // units_per_entry).astype(np.int32)
    step_pages = page_table.reshape(n_units, pages_per_unit)[unit_of_step].reshape(-1)

    tok = pl.BlockSpec((n_t, aw), lambda g, pt, th, ti, tt, tl, te, tb: (te[g], 0))
    stat = pl.BlockSpec((1, n_rows, 1), lambda g, pt, th, ti, tt, tl, te, tb: (te[g], 0, 0))

    head_map = lambda g, pt, th, *_: (th[g], 0, 0, 0)
    grid_spec = pltpu.PrefetchScalarGridSpec(
        num_scalar_prefetch=7,
        grid=(n_tasks,),
        in_specs=[
            pl.BlockSpec((1, 2, LANES, tk), lambda g, pt, th, ti, *_: (th[g], ti[g], 0, 0)),
            pl.BlockSpec((1, nt, 2 * tk, LANES), head_map, pipeline_mode=pl.Buffered(1)),
            pl.BlockSpec((1, nt, V_DIM + SUM_ROWS, tk), head_map, pipeline_mode=pl.Buffered(1)),
            pl.BlockSpec((1, 2, 2, tk), lambda g, pt, th, ti, *_: (th[g], ti[g], 0, 0)),
            pl.BlockSpec(lam_vecs.shape, lambda g, *_: (0, 0)),
            pl.BlockSpec((V_DIM, 1), lambda g, *_: (0, 0)),
            pl.BlockSpec((1, V_DIM), lambda g, *_: (0, 0)),
            tok, stat, stat,
            pl.BlockSpec((1, n_rows, aw), lambda g, pt, th, ti, tt, tl, te, tb: (te[g], 0, 0)),
            pl.BlockSpec(memory_space=pl.ANY),
            pl.BlockSpec(memory_space=pl.ANY),
        ],
        out_specs=[
            pl.BlockSpec((tq, V_DIM), lambda g, pt, th, ti, *_: (ti[g], th[g])),
            tok,
        ],
        scratch_shapes=[
            pltpu.VMEM((2, 1, tq), F32),
            pltpu.VMEM((2, V_DIM + SUM_ROWS, tq), F32),
            pltpu.VMEM((LANES, tq), BF16),
            pltpu.VMEM((2, 1, tq), F32),
            pltpu.VMEM((2 * tk, tq), F32),
            pltpu.VMEM((2 * tk, tq), F32),
            pltpu.VMEM((2, 1, tq), F32),
            pltpu.VMEM((2, 1, tq), F32),
            pltpu.SMEM((1,), jnp.int32),
            pltpu.VMEM((2, pages_per_unit, aw, page), F32),
            pltpu.VMEM((2, pages_per_unit, page * n_heads, V_DIM), F32),
            pltpu.SemaphoreType.DMA((2,)),
            pltpu.VMEM((aw, pages_per_unit * page), BF16),
            pltpu.VMEM((pages_per_unit * page, aw), BF16),
            pltpu.VMEM((n_rows, 1), F32),
            pltpu.VMEM((n_rows, 1), F32),
            pltpu.VMEM((n_rows, aw), F32),
        ],
    )
    return pl.pallas_call(
        kern,
        grid_spec=grid_spec,
        out_shape=(jax.ShapeDtypeStruct((t, n_heads * V_DIM), BF16),
                   jax.ShapeDtypeStruct((n_b * n_t, aw), F32)),
        name="attention",
        compiler_params=_params(("arbitrary",)),
    )(step_pages, th, ti, tt, tl, te, tb, qt, kst, vt, bound, lam_vecs, subln_g.reshape(-1, 1),
      subln_g.reshape(1, -1), q_s, m0, l0, a0, cache_kt, cache_v)


def _cross_kernel(o_ref, c_ref, x_ref, wout_ref, gpost_ref, gpre_ref, wmq_ref, mk_ref, mv_ref, wmo_ref,
                  gcpost_ref, y_ref, *, mem_heads, row_splits):
    tm, d = x_ref.shape
    hd = d // mem_heads
    n_lt = hd // LANES
    n_mem = mk_ref.shape[1] // (mem_heads * n_lt)

    def head_rows(ref, h):
        return jnp.concatenate(
            [ref[0, pl.ds(lt * mem_heads + h, n_mem, stride=mem_heads * n_lt), :] for lt in range(n_lt)],
            axis=-1).astype(BF16)

    mks = [head_rows(mk_ref, h) for h in range(mem_heads)]
    mvs = [head_rows(mv_ref, h) for h in range(mem_heads)]
    rs = tm // row_splits
    groups = [slice(r * rs, (r + 1) * rs) for r in range(row_splits)]
    scale = hd ** -0.5
    mix = [jnp.dot(jnp.concatenate([o_ref[g, :].astype(BF16), c_ref[g, :].astype(BF16)], axis=-1),
                   wout_ref[...], preferred_element_type=F32) for g in groups]
    x1 = [x_ref[g, :] + _rms(m, gpost_ref[...]) for g, m in zip(groups, mix)]
    q = [(jnp.dot(_rms(x, gpre_ref[...]).astype(BF16), wmq_ref[...], preferred_element_type=F32)
          * scale).astype(BF16) for x in x1]
    outs = [[] for _ in groups]
    for h in range(mem_heads):
        sl = slice(h * hd, (h + 1) * hd)
        s = [lax.dot_general(qr[:, sl], mks[h], (((1,), (1,)), ((), ())), preferred_element_type=F32)
             for qr in q]
        for r, sr in enumerate(s):
            p = jnp.exp(sr - jnp.max(sr, axis=-1, keepdims=True))
            p = p / jnp.sum(p, axis=-1, keepdims=True)
            outs[r].append(jnp.dot(p.astype(BF16), mvs[h], preferred_element_type=F32).astype(BF16))
    att = [jnp.dot(jnp.concatenate(o, axis=-1), wmo_ref[...], preferred_element_type=F32) for o in outs]
    for g, x, a in zip(groups, x1, att):
        y_ref[g, :] = x + _rms(a, gcpost_ref[...])


def _cross(o, c, x, w_out, g_post, g_pre, w_mq, mem_k, mem_v, w_mo, g_cpost, *, tm, mem_heads):
    t, d = x.shape
    aw = o.shape[1]
    cwid = c.shape[1]
    mem_rows, mem_hd = mem_k.shape[1:]
    row = lambda i: (i, 0)
    return pl.pallas_call(
        functools.partial(_cross_kernel, mem_heads=mem_heads, row_splits=1),
        grid=(t // tm,),
        in_specs=[
            pl.BlockSpec((tm, aw), row),
            pl.BlockSpec((tm, cwid), row),
            pl.BlockSpec((tm, d), row),
            _const_spec(w_out.shape),
            _const_spec((1, d)),
            _const_spec((1, d)),
            _const_spec(w_mq.shape),
            _const_spec((1, mem_rows, mem_hd)),
            _const_spec((1, mem_rows, mem_hd)),
            _const_spec(w_mo.shape),
            _const_spec((1, d)),
        ],
        out_specs=pl.BlockSpec((tm, d), row),
        out_shape=jax.ShapeDtypeStruct((t, d), F32),
        name="cross_prompt",
        compiler_params=_params(("arbitrary",)),
    )(o, c, x, w_out, g_post, g_pre, w_mq, mem_k, mem_v, w_mo, g_cpost)


def _cross_sample_kernel(o_ref, c_ref, x_ref, wout_ref, gpost_ref, gpre_ref, wmq_ref, mk_ref, mv_ref, wmo_ref,
                         gcpost_ref, y_ref, x1_scr, q_scr, att_scr, *, mem_heads, n_t):
    b = pl.program_id(0)
    d = x_ref.shape[-1]
    hd = d // mem_heads
    n_lt = hd // LANES
    n_mem = mk_ref.shape[1] // (mem_heads * n_lt)

    @pl.when(b == 0)
    def _():
        mix_in = jnp.concatenate([o_ref[...].astype(BF16), c_ref[...].astype(BF16)], axis=-1)
        mix = jnp.dot(mix_in, wout_ref[...], preferred_element_type=F32)
        x1 = x_ref[...] + _rms(mix, gpost_ref[...])
        x1_scr[...] = x1
        hq = _rms(x1, gpre_ref[...]).astype(BF16)
        q_scr[...] = jnp.dot(hq, wmq_ref[...], preferred_element_type=F32) * (hd ** -0.5)

    def head_rows(ref, h):
        return jnp.concatenate(
            [ref[0, pl.ds(lt * mem_heads + h, n_mem, stride=mem_heads * n_lt), :] for lt in range(n_lt)],
            axis=-1).astype(BF16)

    rows = pl.ds(pl.multiple_of(b * n_t, n_t), n_t)
    q = q_scr[rows, :].astype(BF16)
    for h in range(mem_heads):
        sl = slice(h * hd, (h + 1) * hd)
        s = lax.dot_general(q[:, sl], head_rows(mk_ref, h), (((1,), (1,)), ((), ())),
                            preferred_element_type=F32)
        p = jnp.exp(s - jnp.max(s, axis=-1, keepdims=True))
        p = p / jnp.sum(p, axis=-1, keepdims=True)
        att_scr[rows, sl] = jnp.dot(p.astype(BF16), head_rows(mv_ref, h), preferred_element_type=F32)

    @pl.when(b == pl.num_programs(0) - 1)
    def _():
        att = jnp.dot(att_scr[...].astype(BF16), wmo_ref[...], preferred_element_type=F32)
        y_ref[...] = x1_scr[...] + _rms(att, gcpost_ref[...])


def _cross_sample(o, c, x, w_out, g_post, g_pre, w_mq, mem_k, mem_v, w_mo, g_cpost, *, n_t, mem_heads):
    t, d = x.shape
    n_b, mem_rows, mem_hd = mem_k.shape
    assert n_t == SUBLANES and t == n_b * n_t
    mem_map = lambda b: (b, 0, 0)
    return pl.pallas_call(
        functools.partial(_cross_sample_kernel, mem_heads=mem_heads, n_t=n_t),
        grid=(n_b,),
        in_specs=[
            _const_spec(o.shape),
            _const_spec(c.shape),
            _const_spec(x.shape),
            _const_spec(w_out.shape),
            _const_spec((1, d)),
            _const_spec((1, d)),
            _const_spec(w_mq.shape),
            pl.BlockSpec((1, mem_rows, mem_hd), mem_map),
            pl.BlockSpec((1, mem_rows, mem_hd), mem_map),
            _const_spec(w_mo.shape),
            _const_spec((1, d)),
        ],
        out_specs=pl.BlockSpec((t, d), lambda b: (0, 0)),
        out_shape=jax.ShapeDtypeStruct((t, d), F32),
        scratch_shapes=[pltpu.VMEM((t, d), F32), pltpu.VMEM((t, d), F32), pltpu.VMEM((t, d), F32)],
        name="cross_sample",
        compiler_params=_params(("arbitrary",)),
    )(o, c, x, w_out, g_post, g_pre, w_mq, mem_k, mem_v, w_mo, g_cpost)


def _mlp_kernel(x_ref, gpre_ref, w1_ref, w2_ref, gpost_ref, y_ref):
    x = x_ref[...]
    h = _rms(x, gpre_ref[...]).astype(BF16)
    hf = jnp.dot(h, w1_ref[...], preferred_element_type=F32)
    r = jnp.maximum(hf, 0.0)
    f = jnp.dot((r * r).astype(BF16), w2_ref[...], preferred_element_type=F32)
    y_ref[...] = x + _rms(f, gpost_ref[...])


def _mlp(x, g_pre, w1, w2, g_post, *, tm):
    t, d = x.shape
    row = lambda i: (i, 0)
    return pl.pallas_call(
        _mlp_kernel,
        grid=(t // tm,),
        in_specs=[pl.BlockSpec((tm, d), row), _const_spec((1, d)), _const_spec(w1.shape),
                  _const_spec(w2.shape), _const_spec((1, d))],
        out_specs=pl.BlockSpec((tm, d), row),
        out_shape=jax.ShapeDtypeStruct((t, d), F32),
        name="mlp_tile%d" % tm,
        compiler_params=_params(("arbitrary",)),
    )(x, g_pre, w1, w2, g_post)


def _rope_tables(pos):
    half = HEAD_DIM // 2
    inv = ROPE_THETA ** (-jnp.arange(0, half, dtype=F32) * 2.0 / HEAD_DIM)
    ang = pos.astype(F32)[:, None] * inv[None, :]
    cos = jnp.cos(ang)
    sin = jnp.sin(ang)
    reps = LANES // HEAD_DIM
    return (jnp.tile(jnp.concatenate([cos, cos], axis=-1), (1, reps)),
            jnp.tile(jnp.concatenate([-sin, sin], axis=-1), (1, reps)))


def _conv_toeplitz(conv_w, n_t):
    n_r = CONV_K - 1 + n_t
    r = jnp.arange(n_r)[:, None]
    t = jnp.arange(n_t)[None, :]
    j = r - t
    ok = (j >= 0) & (j < CONV_K)
    return jnp.where(ok[:, :, None], conv_w[jnp.clip(j, 0, CONV_K - 1)], 0.0)


def _layer(i, depth_inputs, x_prompt, x_sample, mem_prompt, cache_k, cache_v, cache_conv, cache_mem_k,
           cache_mem_v, page_table, *, tile):
    (w_in, lq1, lk1, lq2, lk2, subln_g, conv_w, conv_b, ln_g, ln_b, w_out, mix_pre_g, mix_post_g, mem_norm_g,
     w_mq, w_mk, w_mv, w_mo, cross_pre_g, cross_post_g, w_ff1, w_ff2, mlp_pre_g, mlp_post_g) = depth_inputs
    lam_init = 0.8 - 0.6 * math.exp(-0.3 * i)
    _, seq, d = x_prompt.shape
    n_b, n_t, _ = x_sample.shape
    n_heads, v_dim = cache_v.shape[-2:]
    aw = n_heads * v_dim
    cwid = cache_conv.shape[-1]
    mem_heads = cache_mem_k.shape[-2]
    n_mem = mem_prompt.shape[1]
    past = page_table.shape[1] * cache_k.shape[1]

    row2 = lambda a: a.reshape(1, -1)
    w_in_b, w_out_b, w_mq_b, w_mk_b, w_mv_b, w_mo_b, w_ff1_b, w_ff2_b = (
        w.astype(BF16) for w in (w_in, w_out, w_mq, w_mk, w_mv, w_mo, w_ff1, w_ff2))
    lam_vecs = jnp.stack([lq1, lk1, lq2, lk2])

    mk_p, mv_p = _memory_kv(mem_prompt[0], row2(mem_norm_g), w_mk_b, w_mv_b, mem_heads=mem_heads)
    rope_tile = jnp.stack(_rope_tables(jnp.arange(0, seq, tile, dtype=jnp.int32)))
    rope_row = jnp.stack(_rope_tables(jnp.arange(tile, dtype=jnp.int32)))
    k_p, v_p, qt, kst, vt, c_p, tail_p, qn2, kn2 = _inproj_prompt(
        x_prompt[0], row2(mix_pre_g), w_in_b, rope_tile, rope_row, conv_w, row2(conv_b), row2(ln_g), row2(ln_b),
        tm=tile, aw=aw, cwid=cwid, n_heads=n_heads)

    pos_s = jnp.tile(past + jnp.arange(n_t, dtype=jnp.int32), n_b)
    cos_s, sin_s = _rope_tables(pos_s)
    xs = x_sample.reshape(n_b * n_t, d)
    k_s, v_s, q_s, glu_s, c_s = _inproj_sample(
        xs, row2(mix_pre_g), w_in_b, cos_s, sin_s, cache_conv, _conv_toeplitz(conv_w, n_t), row2(conv_b),
        row2(ln_g), row2(ln_b), aw=aw, cwid=cwid, n_heads=n_heads, n_b=n_b, n_t=n_t)
    n_pool, page = cache_k.shape[:2]
    cache_kt = jnp.transpose(cache_k, (0, 2, 3, 4, 1)).reshape(n_pool, aw, page)
    cache_vr = cache_v.reshape(n_pool, page * n_heads, v_dim)
    o_p, o_s = _attention(page_table, qt, kst, vt, qn2, kn2, lam_vecs, subln_g, q_s, k_s, v_s, cache_kt, cache_vr,
                          n_b=n_b, n_t=n_t, lam_init=lam_init)
    x2_p = _cross(o_p, c_p, x_prompt[0], w_out_b, row2(mix_post_g), row2(cross_pre_g), w_mq_b,
                  mk_p[None], mv_p[None], w_mo_b, row2(cross_post_g), tm=tile, mem_heads=mem_heads)
    y_p = _mlp(x2_p, row2(mlp_pre_g), w_ff1_b, w_ff2_b, row2(mlp_post_g), tm=tile)
    n_lt = d // mem_heads // LANES

    def mem_rows(a):
        a = a.reshape(a.shape[0], n_mem, mem_heads, n_lt, LANES)
        return jnp.transpose(a, (0, 1, 3, 2, 4)).reshape(a.shape[0], n_mem * n_lt * mem_heads, LANES)

    x2_s = _cross_sample(o_s, c_s.reshape(n_b * n_t, cwid), xs, w_out_b, row2(mix_post_g), row2(cross_pre_g),
                         w_mq_b, mem_rows(cache_mem_k), mem_rows(cache_mem_v), w_mo_b,
                         row2(cross_post_g), n_t=n_t, mem_heads=mem_heads)

    def mem_out(a):
        a = a.reshape(n_mem, n_lt, mem_heads, LANES)
        return jnp.transpose(a, (0, 2, 1, 3)).reshape(1, n_mem, mem_heads, n_lt * LANES)
    y_s = _mlp(x2_s, row2(mlp_pre_g), w_ff1_b, w_ff2_b, row2(mlp_post_g), tm=n_b * n_t)

    hd2 = (n_heads, 2, HEAD_DIM)
    outs = dict(
        y_p=y_p[None], y_s=y_s.reshape(n_b, n_t, d),
        k_p=jnp.transpose(k_p.reshape(*hd2, seq), (3, 0, 1, 2))[None],
        v_p=v_p.reshape(1, seq, n_heads, v_dim),
        conv_p=tail_p[CONV_HALO - (CONV_K - 1):][None],
        mk_p=mem_out(mk_p), mv_p=mem_out(mv_p),
        k_s=k_s.reshape(n_b, n_t, *hd2), v_s=v_s.reshape(n_b, n_t, n_heads, v_dim),
        conv_s=jnp.concatenate([cache_conv[:, n_t:], glu_s], axis=1),
    )
    return outs


def kernel(x_prompt, x_sample, mem_prompt, cache_k, cache_v, cache_conv, cache_mem_k, cache_mem_v, page_table, w_in, lambda_q1, lambda_k1, lambda_q2, lambda_k2, subln_g, conv_w, conv_b, conv_ln_g, conv_ln_b, w_out, mix_pre_g, mix_post_g, mem_norm_g, w_mq, w_mk, w_mv, w_mo, cross_pre_g, cross_post_g, w_ff1, w_ff2, mlp_pre_g, mlp_post_g):
    depth = w_in.shape[0]
    stacked = (w_in, lambda_q1, lambda_k1, lambda_q2, lambda_k2, subln_g, conv_w, conv_b, conv_ln_g, conv_ln_b,
               w_out, mix_pre_g, mix_post_g, mem_norm_g, w_mq, w_mk, w_mv, w_mo, cross_pre_g, cross_post_g,
               w_ff1, w_ff2, mlp_pre_g, mlp_post_g)
    tile = min(512, x_prompt.shape[1])
    y_p, y_s = x_prompt, x_sample
    per_layer = []
    for i in range(depth):
        o = _layer(i, tuple(a[i] for a in stacked), y_p, y_s, mem_prompt, cache_k[i], cache_v[i], cache_conv[i],
                   cache_mem_k[i], cache_mem_v[i], page_table, tile=tile)
        y_p, y_s = o["y_p"], o["y_s"]
        per_layer.append(o)
    stack = lambda name: jnp.stack([o[name] for o in per_layer], axis=0)
    return (y_p, y_s, stack("k_p"), stack("v_p"), stack("conv_p"), stack("mk_p"), stack("mv_p"),
            stack("k_s"), stack("v_s"), stack("conv_s"))
```

```python
import functools
import math

import jax
import numpy as np
import jax.numpy as jnp
from jax import lax
from jax.experimental import pallas as pl
from jax.experimental.pallas import tpu as pltpu

F32 = jnp.float32
BF16 = jnp.bfloat16

RMS_EPS = 1e-6
LN_EPS = 1e-5
ROPE_THETA = 10000.0
LANES = 128
SUBLANES = 8
HEAD_DIM = 64
V_DIM = 2 * HEAD_DIM
CONV_K = 31
CONV_HALO = 32
SUM_ROWS = 16
SAFE_BOUND = 60.0
VMEM_LIMIT = 56 * 1024 * 1024


def _rms(x, g):
    return x * lax.rsqrt(jnp.mean(x * x, axis=-1, keepdims=True) + RMS_EPS) * g


def _lam(lam_ref, lam_init):
    a = jnp.sum(lam_ref[0:1, :] * lam_ref[1:2, :], axis=-1, keepdims=True)
    b = jnp.sum(lam_ref[2:3, :] * lam_ref[3:4, :], axis=-1, keepdims=True)
    return jnp.exp(a) - jnp.exp(b) + lam_init


def _rope_lanes(x, cos, sin_signed, first_half):
    rot = jnp.where(first_half, pltpu.roll(x, LANES - HEAD_DIM // 2, 1), pltpu.roll(x, HEAD_DIM // 2, 1))
    return x * cos + rot * sin_signed


def _const_spec(shape):
    nd = len(shape)
    return pl.BlockSpec(shape, lambda *_: (0,) * nd, pipeline_mode=pl.Buffered(1))


def _params(sem):
    return pltpu.CompilerParams(dimension_semantics=sem, vmem_limit_bytes=VMEM_LIMIT)


def _memory_kv_kernel(mem_ref, g_ref, wk_ref, wv_ref, mk_ref, mv_ref, *, mem_heads):
    m = _rms(mem_ref[...], g_ref[...]).astype(BF16)
    n, d = mem_ref.shape
    hd = d // mem_heads
    mk = jnp.dot(m, wk_ref[...], preferred_element_type=F32)
    mv = jnp.dot(m, wv_ref[...], preferred_element_type=F32)
    n_lt = hd // LANES
    for h in range(mem_heads):
        for lt in range(n_lt):
            rows = pl.ds(lt * mem_heads + h, n, stride=mem_heads * n_lt)
            col = h * hd + lt * LANES
            mk_ref[rows, :] = mk[:, col:col + LANES]
            mv_ref[rows, :] = mv[:, col:col + LANES]


def _memory_kv(mem, g, wk, wv, *, mem_heads):
    n, d = mem.shape
    out = jax.ShapeDtypeStruct((n * d // LANES, LANES), F32)
    return pl.pallas_call(
        functools.partial(_memory_kv_kernel, mem_heads=mem_heads),
        out_shape=(out, out),
        name="memory_kv",
        compiler_params=pltpu.CompilerParams(vmem_limit_bytes=VMEM_LIMIT),
    )(mem, g, wk, wv)


def _inproj_prompt_kernel(x_ref, g_ref, w_ref, rope_tile_ref, rope_row_ref, cw_ref, cb_ref, lng_ref, lnb_ref,
                          kout_ref, vout_ref, qt_ref, kst_ref, vt_ref, c_ref, tail_ref, qn2_ref, kn2_ref,
                          gpad_ref, gsh_ref, *, tm, aw, cwid, n_heads, scale, row_chunk):
    i = pl.program_id(0)

    @pl.when(i == 0)
    def _():
        gpad_ref[0:CONV_HALO, :] = jnp.zeros((CONV_HALO, cwid), F32)

    h = _rms(x_ref[...], g_ref[...]).astype(BF16)
    cos_a = rope_tile_ref[0, pl.ds(i, 1), :]
    sin_a = rope_tile_ref[1, pl.ds(i, 1), :]
    cos_b = rope_row_ref[0]
    sin_b = rope_row_ref[1]
    cos = cos_a * cos_b - sin_a * sin_b
    sin = sin_a * cos_b + cos_a * sin_b
    lane = lax.broadcasted_iota(jnp.int32, (tm, LANES), 1)
    first_half = (lane % HEAD_DIM) < (HEAD_DIM // 2)
    lo = lane < HEAD_DIM
    sum_rows = (lax.broadcasted_iota(jnp.int32, (SUM_ROWS, tm), 0) == 0).astype(BF16)

    q = jnp.dot(h, w_ref[:, 0:aw], preferred_element_type=F32)
    k = jnp.dot(h, w_ref[:, aw:2 * aw], preferred_element_type=F32)
    v = jnp.dot(h, w_ref[:, 2 * aw:3 * aw], preferred_element_type=F32)
    for hd in range(n_heads):
        sl = slice(hd * LANES, (hd + 1) * LANES)
        qh = _rope_lanes(q[:, sl], cos, sin, first_half) * scale
        kh = _rope_lanes(k[:, sl], cos, sin, first_half)
        kht = kh.T
        qht = qh.T
        kout_ref[sl, :] = kht
        vout_ref[pl.ds(hd, tm, stride=n_heads), :] = v[:, sl]
        qt_ref[hd, 0] = qht.astype(BF16)
        for c in range(2):
            half = slice(c * HEAD_DIM, (c + 1) * HEAD_DIM)
            qn2_ref[hd, 0, c:c + 1, :] = jnp.sum(qht[half, :] * qht[half, :], axis=0, keepdims=True)
            kn2 = jnp.max(jnp.sum(kht[half, :] * kht[half, :], axis=0, keepdims=True), axis=1, keepdims=True)
            kn2_ref[0, 2 * hd + c:2 * hd + c + 1, :] = jnp.broadcast_to(kn2, (1, LANES))
        kst_ref[hd, 0, 0:tm, :] = jnp.where(lo, kh, 0.0).astype(BF16)
        kst_ref[hd, 0, tm:2 * tm, :] = jnp.where(lo, 0.0, kh).astype(BF16)
        vt_ref[hd, 0, 0:V_DIM, :] = v[:, sl].T.astype(BF16)
        vt_ref[hd, 0, V_DIM:V_DIM + SUM_ROWS, :] = sum_rows

    u1 = jnp.dot(h, w_ref[:, 3 * aw:3 * aw + cwid], preferred_element_type=F32)
    u2 = jnp.dot(h, w_ref[:, 3 * aw + cwid:3 * aw + 2 * cwid], preferred_element_type=F32)
    gpad_ref[CONV_HALO:CONV_HALO + tm, :] = u1 * jax.nn.sigmoid(u2)

    base = CONV_HALO - (CONV_K - 1)
    cb = cb_ref[...]
    lng = lng_ref[...]
    lnb = lnb_ref[...]
    n_sh = gsh_ref.shape[1]
    for b in range(1, SUBLANES):
        gsh_ref[b - 1] = gpad_ref[pl.ds(b, n_sh), :]
    for r0 in range(0, tm, row_chunk):
        acc = jnp.zeros((row_chunk, cwid), F32)
        for j in range(CONV_K):
            a, b = divmod(base + j, SUBLANES)
            rows = pl.ds(r0 + a * SUBLANES, row_chunk)
            tap = gpad_ref[rows, :] if b == 0 else gsh_ref[b - 1, rows, :]
            acc = acc + cw_ref[j:j + 1, :] * tap
        acc = acc + cb
        mu = jnp.mean(acc, axis=-1, keepdims=True)
        d = acc - mu
        var = jnp.mean(d * d, axis=-1, keepdims=True)
        y = d * lax.rsqrt(var + LN_EPS) * lng + lnb
        c_ref[r0:r0 + row_chunk, :] = (y * jax.nn.sigmoid(y)).astype(c_ref.dtype)

    tail_ref[...] = gpad_ref[tm:tm + CONV_HALO, :]
    gpad_ref[0:CONV_HALO, :] = gpad_ref[tm:tm + CONV_HALO, :]


def _inproj_prompt(x, g, w_in, rope_tile, rope_row, conv_w, conv_b, ln_g, ln_b, *, tm, aw, cwid, n_heads):
    t, d = x.shape
    nt = t // tm
    kern = functools.partial(_inproj_prompt_kernel, tm=tm, aw=aw, cwid=cwid, n_heads=n_heads,
                             scale=HEAD_DIM ** -0.5 * math.log2(math.e), row_chunk=min(tm, 32))
    row = lambda i: (i, 0)
    per_head = lambda i: (0, i, 0, 0)
    return pl.pallas_call(
        kern,
        grid=(nt,),
        in_specs=[
            pl.BlockSpec((tm, d), row),
            _const_spec((1, d)),
            _const_spec(w_in.shape),
            _const_spec(rope_tile.shape),
            _const_spec(rope_row.shape),
            _const_spec(conv_w.shape),
            _const_spec((1, cwid)),
            _const_spec((1, cwid)),
            _const_spec((1, cwid)),
        ],
        out_specs=[
            pl.BlockSpec((aw, tm), lambda i: (0, i)),
            pl.BlockSpec((tm * n_heads, V_DIM), row),
            pl.BlockSpec((n_heads, 1, LANES, tm), per_head),
            pl.BlockSpec((n_heads, 1, 2 * tm, LANES), per_head),
            pl.BlockSpec((n_heads, 1, V_DIM + SUM_ROWS, tm), per_head),
            pl.BlockSpec((tm, cwid), row),
            pl.BlockSpec((CONV_HALO, cwid), lambda i: (0, 0)),
            pl.BlockSpec((n_heads, 1, 2, tm), per_head),
            pl.BlockSpec((1, 2 * n_heads, LANES), lambda i: (i, 0, 0)),
        ],
        out_shape=(
            jax.ShapeDtypeStruct((aw, t), F32),
            jax.ShapeDtypeStruct((t * n_heads, V_DIM), F32),
            jax.ShapeDtypeStruct((n_heads, nt, LANES, tm), BF16),
            jax.ShapeDtypeStruct((n_heads, nt, 2 * tm, LANES), BF16),
            jax.ShapeDtypeStruct((n_heads, nt, V_DIM + SUM_ROWS, tm), BF16),
            jax.ShapeDtypeStruct((t, cwid), BF16),
            jax.ShapeDtypeStruct((CONV_HALO, cwid), F32),
            jax.ShapeDtypeStruct((n_heads, nt, 2, tm), F32),
            jax.ShapeDtypeStruct((nt, 2 * n_heads, LANES), F32),
        ),
        scratch_shapes=[pltpu.VMEM((tm + CONV_HALO, cwid), F32),
                        pltpu.VMEM((SUBLANES - 1, tm + CONV_HALO - SUBLANES, cwid), F32)],
        name="inproj_prompt",
        compiler_params=_params(("arbitrary",)),
    )(x, g, w_in, rope_tile, rope_row, conv_w, conv_b, ln_g, ln_b)


def _inproj_sample_kernel(x_ref, g_ref, w_ref, cos_ref, sin_ref, cc_ref, toep_ref, cb_ref, lng_ref, lnb_ref,
                          kout_ref, vout_ref, q_ref, glu_ref, c_ref, *, aw, cwid, n_heads, scale, n_b, n_t):
    rows = n_b * n_t
    h = _rms(x_ref[...], g_ref[...]).astype(BF16)
    cos = cos_ref[...]
    sin = sin_ref[...]
    lane = lax.broadcasted_iota(jnp.int32, (rows, LANES), 1)
    first_half = (lane % HEAD_DIM) < (HEAD_DIM // 2)
    q = jnp.dot(h, w_ref[:, 0:aw], preferred_element_type=F32)
    k = jnp.dot(h, w_ref[:, aw:2 * aw], preferred_element_type=F32)
    vout_ref[...] = jnp.dot(h, w_ref[:, 2 * aw:3 * aw], preferred_element_type=F32)
    for hd in range(n_heads):
        sl = slice(hd * LANES, (hd + 1) * LANES)
        q_ref[:, sl] = _rope_lanes(q[:, sl], cos, sin, first_half) * scale
        kout_ref[:, sl] = _rope_lanes(k[:, sl], cos, sin, first_half)
    u1 = jnp.dot(h, w_ref[:, 3 * aw:3 * aw + cwid], preferred_element_type=F32)
    u2 = jnp.dot(h, w_ref[:, 3 * aw + cwid:3 * aw + 2 * cwid], preferred_element_type=F32)
    glu_ref[...] = (u1 * jax.nn.sigmoid(u2)).reshape(n_b, n_t, cwid)

    acc = jnp.zeros((n_b, n_t, cwid), F32)
    for r in range(CONV_K - 1):
        acc = acc + cc_ref[:, r:r + 1, :] * toep_ref[r]
    for r in range(n_t):
        acc = acc + glu_ref[:, r:r + 1, :] * toep_ref[CONV_K - 1 + r]
    acc = acc + cb_ref[...]
    mu = jnp.mean(acc, axis=-1, keepdims=True)
    d = acc - mu
    var = jnp.mean(d * d, axis=-1, keepdims=True)
    y = d * lax.rsqrt(var + LN_EPS) * lng_ref[...] + lnb_ref[...]
    c_ref[...] = y * jax.nn.sigmoid(y)


def _inproj_sample(x, g, w_in, cos, sin, cache_conv, toep, conv_b, ln_g, ln_b, *, aw, cwid, n_heads, n_b, n_t):
    rows = n_b * n_t
    kern = functools.partial(_inproj_sample_kernel, aw=aw, cwid=cwid, n_heads=n_heads,
                             scale=HEAD_DIM ** -0.5, n_b=n_b, n_t=n_t)
    return pl.pallas_call(
        kern,
        out_shape=(
            jax.ShapeDtypeStruct((rows, aw), F32),
            jax.ShapeDtypeStruct((rows, aw), F32),
            jax.ShapeDtypeStruct((rows, aw), F32),
            jax.ShapeDtypeStruct((n_b, n_t, cwid), F32),
            jax.ShapeDtypeStruct((n_b, n_t, cwid), F32),
        ),
        name="inproj_sample",
        compiler_params=pltpu.CompilerParams(vmem_limit_bytes=VMEM_LIMIT),
    )(x, g, w_in, cos, sin, cache_conv, toep, conv_b, ln_g, ln_b)


def _masked_queries(q, n_heads, n_t):
    n_rows = n_heads * 2 * n_t
    aw = q.shape[-1]
    qrep = jnp.concatenate([q] * (n_heads * 2), axis=0)
    sel = (lax.broadcasted_iota(jnp.int32, (n_rows, aw), 0) // n_t
           == lax.broadcasted_iota(jnp.int32, (n_rows, aw), 1) // HEAD_DIM)
    return jnp.where(sel, qrep, 0.0).astype(BF16)


def _sample_init_kernel(q_ref, kn_ref, vn_ref, m_ref, l_ref, acc_ref, *, n_heads, n_t, group):
    aw = q_ref.shape[-1]
    n_rows = n_heads * 2 * n_t
    pad = jnp.zeros((LANES - n_t, aw), F32)
    col = lax.broadcasted_iota(jnp.int32, (n_rows, LANES), 1)
    qt = lax.broadcasted_iota(jnp.int32, (n_rows, LANES), 0) % n_t
    for b in range(group):
        rows = slice(b * n_t, (b + 1) * n_t)
        wq = _masked_queries(q_ref[rows, :], n_heads, n_t)
        kn = jnp.concatenate([kn_ref[rows, :], pad], axis=0).astype(BF16)
        vn = jnp.concatenate([vn_ref[rows, :], pad], axis=0).astype(BF16)
        sc = lax.dot_general(wq, kn, (((1,), (1,)), ((), ())), preferred_element_type=F32)
        sc = jnp.where(col <= qt, sc, -jnp.inf)
        m = jnp.max(sc, axis=-1, keepdims=True)
        p = jnp.exp(sc - m)
        m_ref[b] = m
        l_ref[b] = jnp.sum(p, axis=-1, keepdims=True)
        acc_ref[b] = jnp.dot(p.astype(BF16), vn, preferred_element_type=F32)


def _sample_init(q, k_new, v_new, *, n_b, n_t, n_heads):
    aw = q.shape[-1]
    n_rows = n_heads * 2 * n_t
    group = math.gcd(n_b, SUBLANES)
    tok = pl.BlockSpec((group * n_t, aw), lambda g: (g, 0))
    stat = pl.BlockSpec((group, n_rows, 1), lambda g: (g, 0, 0))
    return pl.pallas_call(
        functools.partial(_sample_init_kernel, n_heads=n_heads, n_t=n_t, group=group),
        grid=(n_b // group,),
        in_specs=[tok, tok, tok],
        out_specs=[stat, stat, pl.BlockSpec((group, n_rows, aw), lambda g: (g, 0, 0))],
        out_shape=(jax.ShapeDtypeStruct((n_b, n_rows, 1), F32), jax.ShapeDtypeStruct((n_b, n_rows, 1), F32),
                   jax.ShapeDtypeStruct((n_b, n_rows, aw), F32)),
        name="sample_init",
        compiler_params=_params(("arbitrary",)),
    )(q, k_new, v_new)


def _attn_kernel(pt_ref, th_ref, ti_ref, tt_ref, tl_ref, te_ref, tb_ref, tn_ref, qt_ref, kst_ref, vt_ref, bound_ref,
                 lam_ref, sgcol_ref, sgrow_ref, qs_ref, m0_ref, l0_ref, a0_ref, *rest,
                 n_pages, n_units, units_per_entry, tk, page, n_heads, n_t, lam_init):
    kt_hbm, v_hbm, op_ref, os_ref = rest[:4]
    (m_ref, acc_ref, q_scr, ref_scr, sa_ref, sb_ref, xa_ref, xb_ref, pend_ref,
     kpg, vpg, page_sem, kbuf, vbuf, ms_ref, ls_ref, accs_ref) = rest[4:]
    g = pl.program_id(0)
    n_steps = pl.num_programs(0)
    i = ti_ref[g]
    t = tt_ref[g]
    tq = 2 * tk
    lam = _lam(lam_ref, lam_init)

    def page_copies(step, slot):
        copies = []
        for p in range(n_pages):
            pid = pt_ref[step * n_pages + p]
            copies.append(pltpu.make_async_copy(kt_hbm.at[pid], kpg.at[slot, p], page_sem.at[slot]))
            copies.append(pltpu.make_async_copy(v_hbm.at[pid], vpg.at[slot, p], page_sem.at[slot]))
        return copies

    slot = g % 2

    @pl.when(g == 0)
    def _():
        for cp in page_copies(0, 0):
            cp.start()

    @pl.when(g + 1 < n_steps)
    def _():
        for cp in page_copies(g + 1, 1 - slot):
            cp.start()

    for cp in page_copies(g, slot):
        cp.wait()

    unit = jnp.minimum(g, n_units - 1)
    active = g < n_units
    first = unit % units_per_entry == 0

    def sample_scores():
        for p in range(n_pages):
            kbuf[:, p * page:(p + 1) * page] = kpg[slot, p].astype(BF16)
            for hd in range(n_heads):
                vbuf[p * page:(p + 1) * page, hd * V_DIM:(hd + 1) * V_DIM] = (
                    vpg[slot, p, pl.ds(hd, page, stride=n_heads), :].astype(BF16))
        wq = _masked_queries(qs_ref[...], n_heads, n_t)
        sc = jnp.dot(wq, kbuf[...], preferred_element_type=F32)
        return jnp.where(active, sc, -jnp.inf)

    def sample_softmax(sc):
        m_prev = jnp.where(first, m0_ref[0], ms_ref[...])
        l_prev = jnp.where(first, l0_ref[0], ls_ref[...])
        m_new = jnp.maximum(m_prev, jnp.max(sc, axis=-1, keepdims=True))
        p = jnp.exp(sc - m_new)
        alpha = jnp.exp(m_prev - m_new)
        l_new = alpha * l_prev + jnp.sum(p, axis=-1, keepdims=True)
        ms_ref[...] = m_new
        ls_ref[...] = l_new
        return p.astype(BF16), alpha, l_new

    def sample_values(p, alpha, l_new):
        acc = (alpha * jnp.where(first, a0_ref[0], accs_ref[...])
               + jnp.dot(p, vbuf[...], preferred_element_type=F32))
        accs_ref[...] = acc
        o = acc / l_new
        for hd in range(n_heads):
            r0 = hd * 2 * n_t
            sl = slice(hd * V_DIM, (hd + 1) * V_DIM)
            a = o[r0:r0 + n_t, sl] - lam * o[r0 + n_t:r0 + 2 * n_t, sl]
            os_ref[:, sl] = _rms(a, sgrow_ref[...]) * (1.0 - lam_init)

    def scores(j, s_ref, smax_ref, key_offset=None):
        st = jnp.dot(kst_ref[0, j], q_scr[...], preferred_element_type=F32)
        if key_offset is not None:
            keep = (lax.broadcasted_iota(jnp.int32, (tk, tq), 0) + key_offset
                    <= lax.broadcasted_iota(jnp.int32, (tk, tq), 1))
        for c in range(2):
            sc = st[c * tk:(c + 1) * tk, :]
            if key_offset is not None:
                sc = jnp.where(keep, sc, -jnp.inf)
            s_ref[c * tk:(c + 1) * tk, :] = sc
            smax_ref[c] = jnp.max(sc, axis=0, keepdims=True)

    def absorb(j, s_ref, smax_ref):
        vt = vt_ref[0, j]
        for c in range(2):
            sc = s_ref[c * tk:(c + 1) * tk, :]
            m_prev = m_ref[c]
            m_new = jnp.maximum(m_prev, smax_ref[c])
            p = jnp.exp2(sc - m_new).astype(BF16)
            alpha = jnp.exp2(m_prev - m_new)
            acc_ref[c] = alpha * acc_ref[c] + jnp.dot(vt, p, preferred_element_type=F32)
            m_ref[c] = m_new

    def bounded_probs(j, key_offset=None):
        st = jnp.dot(kst_ref[0, j], q_scr[...], preferred_element_type=F32)
        if key_offset is not None:
            keep = (lax.broadcasted_iota(jnp.int32, (tk, tq), 0) + key_offset
                    <= lax.broadcasted_iota(jnp.int32, (tk, tq), 1))
        probs = []
        for c in range(2):
            p = jnp.exp2(st[c * tk:(c + 1) * tk, :] - ref_scr[c])
            if key_offset is not None:
                p = jnp.where(keep, p, 0.0)
            probs.append(p.astype(BF16))
        return probs

    def bounded_values(j, pa, pb):
        vt = jnp.concatenate([vt_ref[0, j], vt_ref[0, j + 1]], axis=1)
        for c in range(2):
            acc_ref[c] += jnp.dot(vt, jnp.concatenate([pa[c], pb[c]], axis=0), preferred_element_type=F32)

    def start_query_block():
        q_scr[:, 0:tk] = qt_ref[0, 0]
        q_scr[:, tk:tq] = qt_ref[0, 1]
        acc_ref[...] = jnp.zeros(acc_ref.shape, F32)

    bounded = tb_ref[g] == 1
    diag = t < 0

    @pl.when(jnp.logical_and(diag, bounded))
    def _():
        start_query_block()
        for c in range(2):
            ref_scr[c, :, 0:tk] = bound_ref[0, 0, c:c + 1, :]
            ref_scr[c, :, tk:tq] = bound_ref[0, 1, c:c + 1, :]
        sc = sample_scores()
        pa = bounded_probs(2 * i, key_offset=0)
        soft = sample_softmax(sc)
        pb = bounded_probs(2 * i + 1, key_offset=tk)
        sample_values(*soft)
        bounded_values(2 * i, pa, pb)

    n_pairs = tn_ref[g]

    @pl.when(jnp.logical_and(n_pairs == 1, bounded))
    def _():
        sc = sample_scores()
        pa = bounded_probs(2 * t)
        soft = sample_softmax(sc)
        pb = bounded_probs(2 * t + 1)
        sample_values(*soft)
        bounded_values(2 * t, pa, pb)

    @pl.when(jnp.logical_and(n_pairs == 2, bounded))
    def _():
        sc = sample_scores()
        pa = bounded_probs(2 * t)
        soft = sample_softmax(sc)
        pb = bounded_probs(2 * t + 1)
        sample_values(*soft)
        bounded_values(2 * t, pa, pb)
        pc = bounded_probs(2 * t + 2)
        pd = bounded_probs(2 * t + 3)
        bounded_values(2 * t + 2, pc, pd)

    @pl.when(jnp.logical_and(diag, jnp.logical_not(bounded)))
    def _():
        start_query_block()
        m_ref[...] = jnp.full(m_ref.shape, -jnp.inf, F32)
        sc = sample_scores()
        scores(2 * i, sa_ref, xa_ref, key_offset=0)
        soft = sample_softmax(sc)
        scores(2 * i + 1, sb_ref, xb_ref, key_offset=tk)
        sample_values(*soft)
        absorb(2 * i, sa_ref, xa_ref)
        pend_ref[0] = 2 * i + 1

    @pl.when(jnp.logical_and(n_pairs >= 1, jnp.logical_not(bounded)))
    def _():
        pending = pend_ref[0]
        sc = sample_scores()
        scores(2 * t, sa_ref, xa_ref)
        soft = sample_softmax(sc)
        absorb(pending, sb_ref, xb_ref)
        sample_values(*soft)
        scores(2 * t + 1, sb_ref, xb_ref)
        absorb(2 * t, sa_ref, xa_ref)
        pend_ref[0] = 2 * t + 1

    @pl.when(jnp.logical_and(n_pairs == 2, jnp.logical_not(bounded)))
    def _():
        scores(2 * t + 2, sa_ref, xa_ref)
        absorb(2 * t + 1, sb_ref, xb_ref)
        scores(2 * t + 3, sb_ref, xb_ref)
        absorb(2 * t + 2, sa_ref, xa_ref)
        pend_ref[0] = 2 * t + 3

    last = tl_ref[g] == 1

    @pl.when(jnp.logical_and(last, jnp.logical_not(bounded)))
    def _():
        absorb(pend_ref[0], sb_ref, xb_ref)

    @pl.when(last)
    def _():
        o1 =acc_ref[0, 0:V_DIM, :] / acc_ref[0, V_DIM:V_DIM + 1, :]
        o2 = acc_ref[1, 0:V_DIM, :] / acc_ref[1, V_DIM:V_DIM + 1, :]
        a = o1 - lam * o2
        y = a * lax.rsqrt(jnp.mean(a * a, axis=0, keepdims=True) + RMS_EPS) * sgcol_ref[...] * (1.0 - lam_init)
        op_ref[...] = y.T.astype(op_ref.dtype)


def _prompt_tasks(n_heads, nq):
    tasks = []
    for h in range(n_heads):
        for i in range(nq):
            tasks.append((h, i, -1, int(i == 0), 0))
            for t in range(0, i, 2):
                n = min(2, i - t)
                tasks.append((h, i, t, int(t + n == i), n))
    return tuple(np.asarray(col, np.int32) for col in zip(*tasks))


def _attention(page_table, qt, kst, vt, qn2, kn2, lam_vecs, subln_g, q_s, k_new, v_new, cache_kt, cache_v,
               *, n_b, n_t, lam_init):
    n_heads, nt, _, tk = qt.shape
    assert nt % 2 == 0, "a query block spans two key blocks"
    nq = nt // 2
    tq = 2 * tk
    t = nt * tk
    n_pool, aw, page = cache_kt.shape
    n_rows = n_heads * 2 * n_t
    th, ti, tt, tl, tn = _prompt_tasks(n_heads, nq)
    n_tasks = len(th)
    kmax2 = jnp.max(kn2[:, :, 0], axis=0).reshape(n_heads, 1, 2, 1)
    bound = jnp.sqrt(qn2 * kmax2)
    block_bound = jnp.max(bound.reshape(n_heads, nq, -1), axis=-1)
    tb = (block_bound <= SAFE_BOUND).astype(jnp.int32)[th, ti]
    n_pages_entry = page_table.shape[1]
    pages_per_unit = min(p for p in range(1, n_pages_entry + 1)
                         if n_pages_entry % p == 0 and n_b * (n_pages_entry // p) <= n_tasks)
    units_per_entry = n_pages_entry // pages_per_unit
    n_units = n_b * units_per_entry
    m0, l0, a0 = _sample_init(q_s, k_new, v_new, n_b=n_b, n_t=n_t, n_heads=n_heads)
    kern = functools.partial(_attn_kernel, n_pages=pages_per_unit, n_units=n_units,
                             units_per_entry=units_per_entry, tk=tk, page=page, n_heads=n_heads, n_t=n_t,
                             lam_init=lam_init)

    unit_of_step = np.minimum(np.arange(n_tasks), n_units - 1)
    te = (unit_of_step // units_per_entry).astype(np.int32)
    step_pages = page_table.reshape(n_units, pages_per_unit)[unit_of_step].reshape(-1)

    tok = pl.BlockSpec((n_t, aw), lambda g, pt, th, ti, tt, tl, te, *_: (te[g], 0))
    stat = pl.BlockSpec((1, n_rows, 1), lambda g, pt, th, ti, tt, tl, te, *_: (te[g], 0, 0))

    head_map = lambda g, pt, th, *_: (th[g], 0, 0, 0)
    grid_spec = pltpu.PrefetchScalarGridSpec(
        num_scalar_prefetch=8,
        grid=(n_tasks,),
        in_specs=[
            pl.BlockSpec((1, 2, LANES, tk), lambda g, pt, th, ti, *_: (th[g], ti[g], 0, 0)),
            pl.BlockSpec((1, nt, 2 * tk, LANES), head_map, pipeline_mode=pl.Buffered(1)),
            pl.BlockSpec((1, nt, V_DIM + SUM_ROWS, tk), head_map, pipeline_mode=pl.Buffered(1)),
            pl.BlockSpec((1, 2, 2, tk), lambda g, pt, th, ti, *_: (th[g], ti[g], 0, 0)),
            pl.BlockSpec(lam_vecs.shape, lambda g, *_: (0, 0)),
            pl.BlockSpec((V_DIM, 1), lambda g, *_: (0, 0)),
            pl.BlockSpec((1, V_DIM), lambda g, *_: (0, 0)),
            tok, stat, stat,
            pl.BlockSpec((1, n_rows, aw), lambda g, pt, th, ti, tt, tl, te, *_: (te[g], 0, 0)),
            pl.BlockSpec(memory_space=pl.ANY),
            pl.BlockSpec(memory_space=pl.ANY),
        ],
        out_specs=[
            pl.BlockSpec((tq, V_DIM), lambda g, pt, th, ti, *_: (ti[g], th[g])),
            tok,
        ],
        scratch_shapes=[
            pltpu.VMEM((2, 1, tq), F32),
            pltpu.VMEM((2, V_DIM + SUM_ROWS, tq), F32),
            pltpu.VMEM((LANES, tq), BF16),
            pltpu.VMEM((2, 1, tq), F32),
            pltpu.VMEM((2 * tk, tq), F32),
            pltpu.VMEM((2 * tk, tq), F32),
            pltpu.VMEM((2, 1, tq), F32),
            pltpu.VMEM((2, 1, tq), F32),
            pltpu.SMEM((1,), jnp.int32),
            pltpu.VMEM((2, pages_per_unit, aw, page), F32),
            pltpu.VMEM((2, pages_per_unit, page * n_heads, V_DIM), F32),
            pltpu.SemaphoreType.DMA((2,)),
            pltpu.VMEM((aw, pages_per_unit * page), BF16),
            pltpu.VMEM((pages_per_unit * page, aw), BF16),
            pltpu.VMEM((n_rows, 1), F32),
            pltpu.VMEM((n_rows, 1), F32),
            pltpu.VMEM((n_rows, aw), F32),
        ],
    )
    return pl.pallas_call(
        kern,
        grid_spec=grid_spec,
        out_shape=(jax.ShapeDtypeStruct((t, n_heads * V_DIM), BF16),
                   jax.ShapeDtypeStruct((n_b * n_t, aw), F32)),
        name="attention",
        compiler_params=_params(("arbitrary",)),
    )(step_pages, th, ti, tt, tl, te, tb, tn, qt, kst, vt, bound, lam_vecs, subln_g.reshape(-1, 1),
      subln_g.reshape(1, -1), q_s, m0, l0, a0, cache_kt, cache_v)


def _cross_kernel(o_ref, c_ref, x_ref, wout_ref, gpost_ref, gpre_ref, wmq_ref, mk_ref, mv_ref, wmo_ref,
                  gcpost_ref, y_ref, *, mem_heads, row_splits):
    tm, d = x_ref.shape
    hd = d // mem_heads
    n_lt = hd // LANES
    n_mem = mk_ref.shape[1] // (mem_heads * n_lt)

    def head_rows(ref, h):
        return jnp.concatenate(
            [ref[0, pl.ds(lt * mem_heads + h, n_mem, stride=mem_heads * n_lt), :] for lt in range(n_lt)],
            axis=-1).astype(BF16)

    mks = [head_rows(mk_ref, h) for h in range(mem_heads)]
    mvs = [head_rows(mv_ref, h) for h in range(mem_heads)]
    rs = tm // row_splits
    groups = [slice(r * rs, (r + 1) * rs) for r in range(row_splits)]
    scale = hd ** -0.5
    mix = [jnp.dot(jnp.concatenate([o_ref[g, :].astype(BF16), c_ref[g, :].astype(BF16)], axis=-1),
                   wout_ref[...], preferred_element_type=F32) for g in groups]
    x1 = [x_ref[g, :] + _rms(m, gpost_ref[...]) for g, m in zip(groups, mix)]
    q = [(jnp.dot(_rms(x, gpre_ref[...]).astype(BF16), wmq_ref[...], preferred_element_type=F32)
          * scale).astype(BF16) for x in x1]
    outs = [[] for _ in groups]
    for h in range(mem_heads):
        sl = slice(h * hd, (h + 1) * hd)
        s = [lax.dot_general(qr[:, sl], mks[h], (((1,), (1,)), ((), ())), preferred_element_type=F32)
             for qr in q]
        for r, sr in enumerate(s):
            p = jnp.exp(sr - jnp.max(sr, axis=-1, keepdims=True))
            p = p / jnp.sum(p, axis=-1, keepdims=True)
            outs[r].append(jnp.dot(p.astype(BF16), mvs[h], preferred_element_type=F32).astype(BF16))
    att = [jnp.dot(jnp.concatenate(o, axis=-1), wmo_ref[...], preferred_element_type=F32) for o in outs]
    for g, x, a in zip(groups, x1, att):
        y_ref[g, :] = x + _rms(a, gcpost_ref[...])


def _cross(o, c, x, w_out, g_post, g_pre, w_mq, mem_k, mem_v, w_mo, g_cpost, *, tm, mem_heads):
    t, d = x.shape
    aw = o.shape[1]
    cwid = c.shape[1]
    mem_rows, mem_hd = mem_k.shape[1:]
    row = lambda i: (i, 0)
    return pl.pallas_call(
        functools.partial(_cross_kernel, mem_heads=mem_heads, row_splits=1),
        grid=(t // tm,),
        in_specs=[
            pl.BlockSpec((tm, aw), row),
            pl.BlockSpec((tm, cwid), row),
            pl.BlockSpec((tm, d), row),
            _const_spec(w_out.shape),
            _const_spec((1, d)),
            _const_spec((1, d)),
            _const_spec(w_mq.shape),
            _const_spec((1, mem_rows, mem_hd)),
            _const_spec((1, mem_rows, mem_hd)),
            _const_spec(w_mo.shape),
            _const_spec((1, d)),
        ],
        out_specs=pl.BlockSpec((tm, d), row),
        out_shape=jax.ShapeDtypeStruct((t, d), F32),
        name="cross_prompt",
        compiler_params=_params(("arbitrary",)),
    )(o, c, x, w_out, g_post, g_pre, w_mq, mem_k, mem_v, w_mo, g_cpost)


def _cross_sample_kernel(o_ref, c_ref, x_ref, wout_ref, gpost_ref, gpre_ref, wmq_ref, mk_ref, mv_ref, wmo_ref,
                         gcpost_ref, y_ref, x1_scr, q_scr, att_scr, *, mem_heads, n_t):
    b = pl.program_id(0)
    d = x_ref.shape[-1]
    hd = d // mem_heads
    n_lt = hd // LANES
    n_mem = mk_ref.shape[1] // (mem_heads * n_lt)

    @pl.when(b == 0)
    def _():
        mix_in = jnp.concatenate([o_ref[...].astype(BF16), c_ref[...].astype(BF16)], axis=-1)
        mix = jnp.dot(mix_in, wout_ref[...], preferred_element_type=F32)
        x1 = x_ref[...] + _rms(mix, gpost_ref[...])
        x1_scr[...] = x1
        hq = _rms(x1, gpre_ref[...]).astype(BF16)
        q_scr[...] = jnp.dot(hq, wmq_ref[...], preferred_element_type=F32) * (hd ** -0.5)

    def head_rows(ref, h):
        return jnp.concatenate(
            [ref[0, pl.ds(lt * mem_heads + h, n_mem, stride=mem_heads * n_lt), :] for lt in range(n_lt)],
            axis=-1).astype(BF16)

    rows = pl.ds(pl.multiple_of(b * n_t, n_t), n_t)
    q = q_scr[rows, :].astype(BF16)
    for h in range(mem_heads):
        sl = slice(h * hd, (h + 1) * hd)
        s = lax.dot_general(q[:, sl], head_rows(mk_ref, h), (((1,), (1,)), ((), ())),
                            preferred_element_type=F32)
        p = jnp.exp(s - jnp.max(s, axis=-1, keepdims=True))
        p = p / jnp.sum(p, axis=-1, keepdims=True)
        att_scr[rows, sl] = jnp.dot(p.astype(BF16), head_rows(mv_ref, h), preferred_element_type=F32)

    @pl.when(b == pl.num_programs(0) - 1)
    def _():
        att = jnp.dot(att_scr[...].astype(BF16), wmo_ref[...], preferred_element_type=F32)
        y_ref[...] = x1_scr[...] + _rms(att, gcpost_ref[...])


def _cross_sample(o, c, x, w_out, g_post, g_pre, w_mq, mem_k, mem_v, w_mo, g_cpost, *, n_t, mem_heads):
    t, d = x.shape
    n_b, mem_rows, mem_hd = mem_k.shape
    assert n_t == SUBLANES and t == n_b * n_t
    mem_map = lambda b: (b, 0, 0)
    return pl.pallas_call(
        functools.partial(_cross_sample_kernel, mem_heads=mem_heads, n_t=n_t),
        grid=(n_b,),
        in_specs=[
            _const_spec(o.shape),
            _const_spec(c.shape),
            _const_spec(x.shape),
            _const_spec(w_out.shape),
            _const_spec((1, d)),
            _const_spec((1, d)),
            _const_spec(w_mq.shape),
            pl.BlockSpec((1, mem_rows, mem_hd), mem_map),
            pl.BlockSpec((1, mem_rows, mem_hd), mem_map),
            _const_spec(w_mo.shape),
            _const_spec((1, d)),
        ],
        out_specs=pl.BlockSpec((t, d), lambda b: (0, 0)),
        out_shape=jax.ShapeDtypeStruct((t, d), F32),
        scratch_shapes=[pltpu.VMEM((t, d), F32), pltpu.VMEM((t, d), F32), pltpu.VMEM((t, d), F32)],
        name="cross_sample",
        compiler_params=_params(("arbitrary",)),
    )(o, c, x, w_out, g_post, g_pre, w_mq, mem_k, mem_v, w_mo, g_cpost)


def _mlp_kernel(x_ref, gpre_ref, w1_ref, w2_ref, gpost_ref, y_ref):
    x = x_ref[...]
    h = _rms(x, gpre_ref[...]).astype(BF16)
    hf = jnp.dot(h, w1_ref[...], preferred_element_type=F32)
    r = jnp.maximum(hf, 0.0)
    f = jnp.dot((r * r).astype(BF16), w2_ref[...], preferred_element_type=F32)
    y_ref[...] = x + _rms(f, gpost_ref[...])


def _mlp(x, g_pre, w1, w2, g_post, *, tm):
    t, d = x.shape
    row = lambda i: (i, 0)
    return pl.pallas_call(
        _mlp_kernel,
        grid=(t // tm,),
        in_specs=[pl.BlockSpec((tm, d), row), _const_spec((1, d)), _const_spec(w1.shape),
                  _const_spec(w2.shape), _const_spec((1, d))],
        out_specs=pl.BlockSpec((tm, d), row),
        out_shape=jax.ShapeDtypeStruct((t, d), F32),
        name="mlp_tile%d" % tm,
        compiler_params=_params(("arbitrary",)),
    )(x, g_pre, w1, w2, g_post)


def _rope_tables(pos):
    half = HEAD_DIM // 2
    inv = ROPE_THETA ** (-jnp.arange(0, half, dtype=F32) * 2.0 / HEAD_DIM)
    ang = pos.astype(F32)[:, None] * inv[None, :]
    cos = jnp.cos(ang)
    sin = jnp.sin(ang)
    reps = LANES // HEAD_DIM
    return (jnp.tile(jnp.concatenate([cos, cos], axis=-1), (1, reps)),
            jnp.tile(jnp.concatenate([-sin, sin], axis=-1), (1, reps)))


def _conv_toeplitz(conv_w, n_t):
    n_r = CONV_K - 1 + n_t
    r = jnp.arange(n_r)[:, None]
    t = jnp.arange(n_t)[None, :]
    j = r - t
    ok = (j >= 0) & (j < CONV_K)
    return jnp.where(ok[:, :, None], conv_w[jnp.clip(j, 0, CONV_K - 1)], 0.0)


def _layer(i, depth_inputs, x_prompt, x_sample, mem_prompt, cache_k, cache_v, cache_conv, cache_mem_k,
           cache_mem_v, page_table, *, tile):
    (w_in, lq1, lk1, lq2, lk2, subln_g, conv_w, conv_b, ln_g, ln_b, w_out, mix_pre_g, mix_post_g, mem_norm_g,
     w_mq, w_mk, w_mv, w_mo, cross_pre_g, cross_post_g, w_ff1, w_ff2, mlp_pre_g, mlp_post_g) = depth_inputs
    lam_init = 0.8 - 0.6 * math.exp(-0.3 * i)
    _, seq, d = x_prompt.shape
    n_b, n_t, _ = x_sample.shape
    n_heads, v_dim = cache_v.shape[-2:]
    aw = n_heads * v_dim
    cwid = cache_conv.shape[-1]
    mem_heads = cache_mem_k.shape[-2]
    n_mem = mem_prompt.shape[1]
    past = page_table.shape[1] * cache_k.shape[1]

    row2 = lambda a: a.reshape(1, -1)
    w_in_b, w_out_b, w_mq_b, w_mk_b, w_mv_b, w_mo_b, w_ff1_b, w_ff2_b = (
        w.astype(BF16) for w in (w_in, w_out, w_mq, w_mk, w_mv, w_mo, w_ff1, w_ff2))
    lam_vecs = jnp.stack([lq1, lk1, lq2, lk2])

    mk_p, mv_p = _memory_kv(mem_prompt[0], row2(mem_norm_g), w_mk_b, w_mv_b, mem_heads=mem_heads)
    rope_tile = jnp.stack(_rope_tables(jnp.arange(0, seq, tile, dtype=jnp.int32)))
    rope_row = jnp.stack(_rope_tables(jnp.arange(tile, dtype=jnp.int32)))
    k_p, v_p, qt, kst, vt, c_p, tail_p, qn2, kn2 = _inproj_prompt(
        x_prompt[0], row2(mix_pre_g), w_in_b, rope_tile, rope_row, conv_w, row2(conv_b), row2(ln_g), row2(ln_b),
        tm=tile, aw=aw, cwid=cwid, n_heads=n_heads)

    pos_s = jnp.tile(past + jnp.arange(n_t, dtype=jnp.int32), n_b)
    cos_s, sin_s = _rope_tables(pos_s)
    xs = x_sample.reshape(n_b * n_t, d)
    k_s, v_s, q_s, glu_s, c_s = _inproj_sample(
        xs, row2(mix_pre_g), w_in_b, cos_s, sin_s, cache_conv, _conv_toeplitz(conv_w, n_t), row2(conv_b),
        row2(ln_g), row2(ln_b), aw=aw, cwid=cwid, n_heads=n_heads, n_b=n_b, n_t=n_t)
    n_pool, page = cache_k.shape[:2]
    cache_kt = jnp.transpose(cache_k, (0, 2, 3, 4, 1)).reshape(n_pool, aw, page)
    cache_vr = cache_v.reshape(n_pool, page * n_heads, v_dim)
    o_p, o_s = _attention(page_table, qt, kst, vt, qn2, kn2, lam_vecs, subln_g, q_s, k_s, v_s, cache_kt, cache_vr,
                          n_b=n_b, n_t=n_t, lam_init=lam_init)
    x2_p = _cross(o_p, c_p, x_prompt[0], w_out_b, row2(mix_post_g), row2(cross_pre_g), w_mq_b,
                  mk_p[None], mv_p[None], w_mo_b, row2(cross_post_g), tm=tile, mem_heads=mem_heads)
    y_p = _mlp(x2_p, row2(mlp_pre_g), w_ff1_b, w_ff2_b, row2(mlp_post_g), tm=tile)
    n_lt = d // mem_heads // LANES

    def mem_rows(a):
        a = a.reshape(a.shape[0], n_mem, mem_heads, n_lt, LANES)
        return jnp.transpose(a, (0, 1, 3, 2, 4)).reshape(a.shape[0], n_mem * n_lt * mem_heads, LANES)

    x2_s = _cross_sample(o_s, c_s.reshape(n_b * n_t, cwid), xs, w_out_b, row2(mix_post_g), row2(cross_pre_g),
                         w_mq_b, mem_rows(cache_mem_k), mem_rows(cache_mem_v), w_mo_b,
                         row2(cross_post_g), n_t=n_t, mem_heads=mem_heads)

    def mem_out(a):
        a = a.reshape(n_mem, n_lt, mem_heads, LANES)
        return jnp.transpose(a, (0, 2, 1, 3)).reshape(1, n_mem, mem_heads, n_lt * LANES)
    y_s = _mlp(x2_s, row2(mlp_pre_g), w_ff1_b, w_ff2_b, row2(mlp_post_g), tm=n_b * n_t)

    hd2 = (n_heads, 2, HEAD_DIM)
    outs = dict(
        y_p=y_p[None], y_s=y_s.reshape(n_b, n_t, d),
        k_p=jnp.transpose(k_p.reshape(*hd2, seq), (3, 0, 1, 2))[None],
        v_p=v_p.reshape(1, seq, n_heads, v_dim),
        conv_p=tail_p[CONV_HALO - (CONV_K - 1):][None],
        mk_p=mem_out(mk_p), mv_p=mem_out(mv_p),
        k_s=k_s.reshape(n_b, n_t, *hd2), v_s=v_s.reshape(n_b, n_t, n_heads, v_dim),
        conv_s=jnp.concatenate([cache_conv[:, n_t:], glu_s], axis=1),
    )
    return outs


def kernel(x_prompt, x_sample, mem_prompt, cache_k, cache_v, cache_conv, cache_mem_k, cache_mem_v, page_table, w_in, lambda_q1, lambda_k1, lambda_q2, lambda_k2, subln_g, conv_w, conv_b, conv_ln_g, conv_ln_b, w_out, mix_pre_g, mix_post_g, mem_norm_g, w_mq, w_mk, w_mv, w_mo, cross_pre_g, cross_post_g, w_ff1, w_ff2, mlp_pre_g, mlp_post_g):
    depth = w_in.shape[0]
    stacked = (w_in, lambda_q1, lambda_k1, lambda_q2, lambda_k2, subln_g, conv_w, conv_b, conv_ln_g, conv_ln_b,
               w_out, mix_pre_g, mix_post_g, mem_norm_g, w_mq, w_mk, w_mv, w_mo, cross_pre_g, cross_post_g,
               w_ff1, w_ff2, mlp_pre_g, mlp_post_g)
    tile = min(512, x_prompt.shape[1])
    y_p, y_s = x_prompt, x_sample
    per_layer = []
    for i in range(depth):
        o = _layer(i, tuple(a[i] for a in stacked), y_p, y_s, mem_prompt, cache_k[i], cache_v[i], cache_conv[i],
                   cache_mem_k[i], cache_mem_v[i], page_table, tile=tile)
        y_p, y_s = o["y_p"], o["y_s"]
        per_layer.append(o)
    stack = lambda name: jnp.stack([o[name] for o in per_layer], axis=0)
    return (y_p, y_s, stack("k_p"), stack("v_p"), stack("conv_p"), stack("mk_p"), stack("mv_p"),
            stack("k_s"), stack("v_s"), stack("conv_s"))
```
